```python
import jax, jax.numpy as jnp
from jax import lax
import numpy as np

D_MODEL = 1024
BATCH = 2
SEQ = 8192
DEPTH = 2
DEC_BATCH = 16
DEC_SEQ = 16
PAST_LEN = 4096

CHUNK = 64
MIX_WIDTH = D_MODEL
DN_WIDTH = MIX_WIDTH // 2
DN_HEADS = 4
DN_HEAD_DIM = DN_WIDTH // DN_HEADS
CONV_WIDTH = 4
CONV_CH = 3 * DN_WIDTH
POOL_WIDTH = MIX_WIDTH - DN_WIDTH
POOL_WINDOWS = (2, 4, 8, 16)
POOL_GROUPS = len(POOL_WINDOWS)
POOL_GROUP_DIM = POOL_WIDTH // POOL_GROUPS
POOL_HIST = max(POOL_WINDOWS) - 1
D_FF = 2816
N_IN = 4 * DN_WIDTH + 2 * DN_HEADS + POOL_WIDTH
EPS = 1e-6

kernel_name = "hybrid_deltanet_pool_macaron_step"


def rms_norm(x, w):
    x32 = x.astype(jnp.float32)
    y = x32 * lax.rsqrt(jnp.mean(x32 * x32, axis=-1, keepdims=True) + EPS)
    return (y * w.astype(jnp.float32)).astype(x.dtype)


def l2_norm(x):
    x32 = x.astype(jnp.float32)
    return x32 * lax.rsqrt(jnp.sum(x32 * x32, axis=-1, keepdims=True) + EPS)


def swiglu(x, w_gate, w_up, w_down):
    return (jax.nn.silu(x @ w_gate) * (x @ w_up)) @ w_down


def causal_dwconv(x, buf, w):
    L = x.shape[1]
    xpad = jnp.concatenate([buf.astype(x.dtype), x], axis=1)
    y = xpad[:, 0:L] * w[0]
    for j in range(1, CONV_WIDTH):
        y = y + xpad[:, j:j + L] * w[j]
    return y, xpad[:, -(CONV_WIDTH - 1):]


def gated_delta_rule(q, k, v, g, beta, s0, chunk):
    B, L, H, dk = q.shape
    dv = v.shape[-1]
    n = L // chunk
    f32 = jnp.float32

    def blocks(t):
        t = t.astype(f32).reshape((B, n, chunk) + t.shape[2:])
        return jnp.moveaxis(t, 2, 3)

    q, k, v, g, beta = blocks(q), blocks(k), blocks(v), blocks(g), blocks(beta)
    gc = jnp.cumsum(g, axis=-1)
    idx = jnp.arange(chunk)
    incl = idx[:, None] >= idx[None, :]
    strict = idx[:, None] > idx[None, :]
    decay = jnp.exp(jnp.where(incl, gc[..., :, None] - gc[..., None, :], -jnp.inf))
    kk = jnp.einsum('bnhid,bnhjd->bnhij', k, k)
    a_mat = jnp.where(strict, beta[..., :, None] * kk * decay, 0.0) + jnp.eye(chunk, dtype=f32)
    rhs = jnp.concatenate([v * beta[..., None], k * (beta * jnp.exp(gc))[..., None]], axis=-1)
    sol = lax.linalg.triangular_solve(a_mat, rhs, left_side=True, lower=True, unit_diagonal=True)
    u, w = sol[..., :dv], sol[..., dv:]
    qk = jnp.einsum('bnhid,bnhjd->bnhij', q, k) * decay
    g_last = gc[..., -1]
    q_dec = q * jnp.exp(gc)[..., None]
    k_dec = k * jnp.exp(g_last[..., None] - gc)[..., None]

    def step(S, xs):
        q_c, qk_c, u_c, w_c, k_c, gl = xs
        v_new = u_c - jnp.einsum('bhcd,bhde->bhce', w_c, S)
        o_c = jnp.einsum('bhcd,bhde->bhce', q_c, S) + jnp.einsum('bhij,bhje->bhie', qk_c, v_new)
        S = S * jnp.exp(gl)[..., None, None] + jnp.einsum('bhcd,bhce->bhde', k_c, v_new)
        return S, o_c

    xs = (jnp.moveaxis(q_dec, 1, 0), jnp.moveaxis(qk, 1, 0), jnp.moveaxis(u, 1, 0),
          jnp.moveaxis(w, 1, 0), jnp.moveaxis(k_dec, 1, 0), jnp.moveaxis(g_last, 1, 0))
    S, o = lax.scan(step, s0.astype(f32), xs)
    o = jnp.transpose(o, (1, 0, 3, 2, 4)).reshape(B, L, H, dv)
    return o, S


def multi_scale_pool(u, buf, w_pool, scale, pos0):
    B, L, _ = u.shape
    upad = jnp.concatenate([buf.astype(u.dtype), u], axis=1)
    cs = jnp.cumsum(upad.astype(jnp.float32), axis=1)
    cs = jnp.concatenate([jnp.zeros((B, 1, POOL_WIDTH), jnp.float32), cs], axis=1)
    pos = pos0 + jnp.arange(L)
    means = []
    for gi, win in enumerate(POOL_WINDOWS):
        sl = slice(gi * POOL_GROUP_DIM, (gi + 1) * POOL_GROUP_DIM)
        s = cs[:, POOL_HIST + 1:POOL_HIST + 1 + L, sl] - cs[:, POOL_HIST + 1 - win:POOL_HIST + 1 - win + L, sl]
        cnt = jnp.minimum(win, pos + 1).astype(jnp.float32)
        means.append(s / cnt[None, :, None])
    d = (jnp.concatenate(means, axis=-1) - u.astype(jnp.float32)).reshape(B, L, POOL_GROUPS, POOL_GROUP_DIM)
    z = jnp.einsum('blgc,gcd->blgd', d, w_pool.astype(jnp.float32)).reshape(B, L, POOL_WIDTH)
    z = z * scale.astype(jnp.float32)
    return z.astype(u.dtype), upad[:, -POOL_HIST:]


def mixer(h, w_in, conv_w, a_log, dt_bias, o_norm, w_pool, pool_scale, w_out,
          s0, conv_buf, pool_buf, pos0, chunk):
    B, L, _ = h.shape
    p = h @ w_in
    c1, c2, c3 = 3 * DN_WIDTH, 4 * DN_WIDTH, 4 * DN_WIDTH + DN_HEADS
    c4 = c3 + DN_HEADS
    qkv, gate, a, b, u = p[..., :c1], p[..., c1:c2], p[..., c2:c3], p[..., c3:c4], p[..., c4:]
    qkv, conv_new = causal_dwconv(qkv, conv_buf, conv_w)
    qkv = jax.nn.silu(qkv)
    q = qkv[..., :DN_WIDTH].reshape(B, L, DN_HEADS, DN_HEAD_DIM)
    k = qkv[..., DN_WIDTH:2 * DN_WIDTH].reshape(B, L, DN_HEADS, DN_HEAD_DIM)
    v = qkv[..., 2 * DN_WIDTH:].reshape(B, L, DN_HEADS, DN_HEAD_DIM)
    q = l2_norm(q) * (DN_HEAD_DIM ** -0.5)
    k = l2_norm(k)
    beta = jax.nn.sigmoid(b.astype(jnp.float32))
    g = -jnp.exp(a_log.astype(jnp.float32)) * jax.nn.softplus(a.astype(jnp.float32) + dt_bias.astype(jnp.float32))
    o, s_new = gated_delta_rule(q, k, v, g, beta, s0, chunk)
    o = rms_norm(o.astype(h.dtype), o_norm) * jax.nn.silu(gate.reshape(B, L, DN_HEADS, DN_HEAD_DIM))
    o = o.reshape(B, L, DN_WIDTH)
    z, pool_new = multi_scale_pool(u, pool_buf, w_pool, pool_scale, pos0)
    out = jnp.concatenate([o, z.astype(o.dtype)], axis=-1) @ w_out
    return out, s_new.astype(s0.dtype), conv_new, pool_new


def setup_inputs(seed: int = 0) -> dict:
    key = jax.random.key(seed)
    ks = jax.random.split(key, 24)
    f32 = jnp.float32
    nrm = lambda k, shape, s: jax.random.normal(k, shape, f32) * s
    dt = jnp.exp(jax.random.uniform(ks[10], (DEPTH, DN_HEADS), f32, np.log(1e-3), np.log(1e-1)))
    return {
        "x_prompt": nrm(ks[0], (BATCH, SEQ, D_MODEL), 1.0),
        "x_sample": nrm(ks[1], (DEC_BATCH, DEC_SEQ, D_MODEL), 1.0),
        "state_delta": nrm(ks[2], (DEPTH, DEC_BATCH, DN_HEADS, DN_HEAD_DIM, DN_HEAD_DIM), 0.05),
        "state_conv": nrm(ks[3], (DEPTH, DEC_BATCH, CONV_WIDTH - 1, CONV_CH), 1.0),
        "state_pool": nrm(ks[4], (DEPTH, DEC_BATCH, POOL_HIST, POOL_WIDTH), 1.0),
        "norm_ffn1": 1.0 + nrm(ks[5], (DEPTH, D_MODEL), 0.02),
        "w_ffn1_gate": nrm(ks[6], (DEPTH, D_MODEL, D_FF), D_MODEL ** -0.5),
        "w_ffn1_up": nrm(ks[7], (DEPTH, D_MODEL, D_FF), D_MODEL ** -0.5),
        "w_ffn1_down": nrm(ks[8], (DEPTH, D_FF, D_MODEL), D_FF ** -0.5),
        "norm_mix": 1.0 + nrm(ks[9], (DEPTH, D_MODEL), 0.02),
        "w_in": nrm(ks[11], (DEPTH, D_MODEL, N_IN), D_MODEL ** -0.5),
        "conv_w": nrm(ks[12], (DEPTH, CONV_WIDTH, CONV_CH), CONV_WIDTH ** -0.5),
        "a_log": jnp.log(jax.random.uniform(ks[13], (DEPTH, DN_HEADS), f32, 1.0, 16.0)),
        "dt_bias": dt + jnp.log(-jnp.expm1(-dt)),
        "o_norm": 1.0 + nrm(ks[14], (DEPTH, DN_HEAD_DIM), 0.02),
        "w_pool": nrm(ks[15], (DEPTH, POOL_GROUPS, POOL_GROUP_DIM, POOL_GROUP_DIM), POOL_GROUP_DIM ** -0.5),
        "pool_scale": 1.0 + nrm(ks[16], (DEPTH, POOL_WIDTH), 0.02),
        "w_out": nrm(ks[17], (DEPTH, MIX_WIDTH, D_MODEL), MIX_WIDTH ** -0.5),
        "norm_ffn2": 1.0 + nrm(ks[18], (DEPTH, D_MODEL), 0.02),
        "w_ffn2_gate": nrm(ks[19], (DEPTH, D_MODEL, D_FF), D_MODEL ** -0.5),
        "w_ffn2_up": nrm(ks[20], (DEPTH, D_MODEL, D_FF), D_MODEL ** -0.5),
        "w_ffn2_down": nrm(ks[21], (DEPTH, D_FF, D_MODEL), D_FF ** -0.5),
        "norm_final": 1.0 + nrm(ks[22], (D_MODEL,), 0.02),
    }


def reference(x_prompt, x_sample, state_delta, state_conv, state_pool,
              norm_ffn1, w_ffn1_gate, w_ffn1_up, w_ffn1_down, norm_mix, w_in, conv_w,
              a_log, dt_bias, o_norm, w_pool, pool_scale, w_out,
              norm_ffn2, w_ffn2_gate, w_ffn2_up, w_ffn2_down, norm_final):
    dt = x_prompt.dtype

    def layer(x, l, s0, conv_buf, pool_buf, pos0, chunk):
        x = x + 0.5 * swiglu(rms_norm(x, norm_ffn1[l]), w_ffn1_gate[l], w_ffn1_up[l], w_ffn1_down[l])
        m, s_new, conv_new, pool_new = mixer(rms_norm(x, norm_mix[l]), w_in[l], conv_w[l], a_log[l],
                                             dt_bias[l], o_norm[l], w_pool[l], pool_scale[l], w_out[l],
                                             s0, conv_buf, pool_buf, pos0, chunk)
        x = x + m
        x = x + 0.5 * swiglu(rms_norm(x, norm_ffn2[l]), w_ffn2_gate[l], w_ffn2_up[l], w_ffn2_down[l])
        return x, s_new, conv_new, pool_new

    xp = x_prompt
    p_delta, p_conv, p_pool = [], [], []
    for l in range(DEPTH):
        xp, s_new, c_new, q_new = layer(
            xp, l,
            jnp.zeros((BATCH, DN_HEADS, DN_HEAD_DIM, DN_HEAD_DIM), dt),
            jnp.zeros((BATCH, CONV_WIDTH - 1, CONV_CH), dt),
            jnp.zeros((BATCH, POOL_HIST, POOL_WIDTH), dt),
            0, CHUNK)
        p_delta.append(s_new); p_conv.append(c_new); p_pool.append(q_new)
    y_prompt = rms_norm(xp, norm_final)

    xs = x_sample
    s_delta, s_conv, s_pool = [], [], []
    for l in range(DEPTH):
        xs, s_new, c_new, q_new = layer(xs, l, state_delta[l], state_conv[l], state_pool[l],
                                        PAST_LEN, xs.shape[1])
        s_delta.append(s_new); s_conv.append(c_new); s_pool.append(q_new)
    y_sample = rms_norm(xs, norm_final)

    return (y_prompt, y_sample,
            jnp.stack(p_delta), jnp.stack(p_conv), jnp.stack(p_pool),
            jnp.stack(s_delta), jnp.stack(s_conv), jnp.stack(s_pool))
```

```python
import functools

import jax
import jax.numpy as jnp
from jax import lax
from jax.experimental import pallas as pl
from jax.experimental.pallas import tpu as pltpu

EPS = 1e-6
DN_HEADS = 4
HEAD_DIM = 128
DN_WIDTH = DN_HEADS * HEAD_DIM
CONV_WIDTH = 4
CONV_CH = 3 * DN_WIDTH
POOL_WINDOWS = (2, 4, 8, 16)
POOL_GROUP_DIM = 128
POOL_WIDTH = len(POOL_WINDOWS) * POOL_GROUP_DIM
POOL_HIST = max(POOL_WINDOWS) - 1
LANES = 128
SUBLANES = 8
INV_BASE = 16
COL_GATE = CONV_CH
COL_POOL = COL_GATE + DN_WIDTH
COL_AB = COL_POOL + POOL_WIDTH
N_IN_PAD = COL_AB + LANES
CONV_PAD = SUBLANES
POOL_PAD = 2 * SUBLANES
VMEM_LIMIT_BYTES = 56 * 1024 * 1024

F32 = jnp.float32
BF16 = jnp.bfloat16


def _rms(x, w):
    return x * lax.rsqrt(jnp.mean(x * x, axis=-1, keepdims=True) + EPS) * w


def _mm(a, b):
    return jnp.dot(a.astype(BF16), b.astype(BF16), preferred_element_type=F32)


def _mm_nt(a, b):
    return lax.dot_general(a.astype(BF16), b.astype(BF16), (((1,), (1,)), ((), ())),
                           preferred_element_type=F32)


def _mm_tn(a, b):
    return lax.dot_general(a.astype(BF16), b.astype(BF16), (((0,), (0,)), ((), ())),
                           preferred_element_type=F32)


def _mm_exact(a, b):
    return jnp.dot(a, b, precision=lax.Precision.HIGHEST, preferred_element_type=F32)


def _ffn_kernel(*refs, f_chunk, final_norm):
    if final_norm:
        x_ref, nw_ref, wg_ref, wu_ref, wd_ref, fw_ref, o_ref, h_ref = refs
    else:
        x_ref, nw_ref, wg_ref, wu_ref, wd_ref, o_ref, h_ref = refs
    x = x_ref[...]
    xn = _rms(x, nw_ref[...]).astype(BF16)
    d_ff = wg_ref.shape[1]
    for c in range(d_ff // f_chunk):
        sl = slice(c * f_chunk, (c + 1) * f_chunk)
        g = jnp.dot(xn, wg_ref[:, sl], preferred_element_type=F32)
        u = jnp.dot(xn, wu_ref[:, sl], preferred_element_type=F32)
        h_ref[:, sl] = (g * jax.nn.sigmoid(g) * u).astype(BF16)
    y = x + 0.5 * jnp.dot(h_ref[...], wd_ref[...], preferred_element_type=F32)
    if final_norm:
        y = _rms(y, fw_ref[...])
    o_ref[...] = y


def _resident(shape):
    zeros = (0,) * len(shape)
    return pl.BlockSpec(shape, lambda *_: zeros, pipeline_mode=pl.Buffered(1))


def _ffn(x, nw, wg, wu, wd, fw=None, *, tm):
    m, d = x.shape
    d_ff = wg.shape[1]
    final_norm = fw is not None
    args = [x, nw, wg, wu, wd] + ([fw] if final_norm else [])
    in_specs = [pl.BlockSpec((tm, d), lambda i: (i, 0)), _resident((1, d)), _resident((d, d_ff)),
                _resident((d, d_ff)), _resident((d_ff, d))] + ([_resident((1, d))] if final_norm else [])
    return pl.pallas_call(
        functools.partial(_ffn_kernel, f_chunk=256, final_norm=final_norm),
        grid=(m // tm,),
        in_specs=in_specs,
        out_specs=pl.BlockSpec((tm, d), lambda i: (i, 0)),
        out_shape=jax.ShapeDtypeStruct((m, d), x.dtype),
        scratch_shapes=[pltpu.VMEM((tm, d_ff), BF16)],
        compiler_params=pltpu.CompilerParams(dimension_semantics=("arbitrary",),
                                             vmem_limit_bytes=VMEM_LIMIT_BYTES),
        name="ffn",
    )(*args)


def _unit_lower_inverse(n, chunk):
    r = lax.broadcasted_iota(jnp.int32, (chunk, chunk), 0)
    c = lax.broadcasted_iota(jnp.int32, (chunk, chunk), 1)
    base = min(chunk, INV_BASE)
    shift = base.bit_length() - 1
    nd = jnp.where((r >> shift) == (c >> shift), n, 0.0) if chunk > base else n
    inv = jnp.where(r == c, 1.0, 0.0) - nd
    power, order = nd, 1
    while 2 * order < base:
        power = _mm(power, power)
        inv = inv + _mm(inv, power)
        order *= 2
    blk = base
    while blk < chunk:
        s = blk.bit_length() - 1
        off = jnp.where(((r >> (s + 1)) == (c >> (s + 1))) & ((r >> s) > (c >> s)), n, 0.0)
        inv = inv - _mm(_mm(inv, off), inv)
        blk *= 2
    return inv


def _mixer_kernel(x_ref, nw_ref, win_ref, cw_ref, alog_ref, dtb_ref, on_ref, wp_ref, ps_ref, wout_ref,
                  s0_ref, c0_ref, p0_ref,
                  y_ref, sn_ref, cn_ref, pn_ref,
                  s_scr, xpad, upad, *, tile, chunk, pos0):
    step = pl.program_id(1)

    @pl.when(step == 0)
    def _():
        s_scr[...] = s0_ref[0]
        xpad[CONV_PAD - (CONV_WIDTH - 1):CONV_PAD, :] = c0_ref[0]
        upad[POOL_PAD - POOL_HIST:POOL_PAD, :] = p0_ref[0]

    x = x_ref[0]
    h = _rms(x, nw_ref[...]).astype(BF16)
    p = jnp.dot(h, win_ref[...], preferred_element_type=F32)
    pre = p[:, :CONV_CH]
    gate = p[:, COL_GATE:COL_POOL]
    u = p[:, COL_POOL:COL_AB]
    ab = p[:, COL_AB:]
    xpad[CONV_PAD:CONV_PAD + tile, :] = pre
    upad[POOL_PAD:POOL_PAD + tile, :] = u

    cw = cw_ref[...]
    conv = xpad[CONV_PAD - 3:CONV_PAD - 3 + tile, :] * cw[0:1]
    conv = conv + xpad[CONV_PAD - 2:CONV_PAD - 2 + tile, :] * cw[1:2]
    conv = conv + xpad[CONV_PAD - 1:CONV_PAD - 1 + tile, :] * cw[2:3]
    conv = conv + pre * cw[3:4]
    qkv = conv * jax.nn.sigmoid(conv)

    a_pre = ab + dtb_ref[...]
    softplus = jnp.maximum(a_pre, 0.0) + jnp.log1p(jnp.exp(-jnp.abs(a_pre)))
    g = -jnp.exp(alog_ref[...]) * softplus
    beta = jax.nn.sigmoid(ab)

    cshift = chunk.bit_length() - 1
    tr = lax.broadcasted_iota(jnp.int32, (tile, tile), 0)
    tc = lax.broadcasted_iota(jnp.int32, (tile, tile), 1)
    same = (tr >> cshift) == (tc >> cshift)
    gc = _mm_exact(jnp.where(same & (tr >= tc), 1.0, 0.0), g)
    gs = _mm_exact(jnp.where(same, 1.0, 0.0), g)
    gc_t = gc.T
    e_gc = jnp.exp(gc)
    e_rem = jnp.exp(gs - gc)
    e_tot = jnp.exp(gs)

    cr = lax.broadcasted_iota(jnp.int32, (chunk, chunk), 0)
    cc = lax.broadcasted_iota(jnp.int32, (chunk, chunk), 1)
    incl = cr >= cc
    strict = cr > cc

    o_heads = []
    for hh in range(DN_HEADS):
        ls = slice(hh * HEAD_DIM, (hh + 1) * HEAD_DIM)
        qh = qkv[:, hh * HEAD_DIM:(hh + 1) * HEAD_DIM]
        kh = qkv[:, DN_WIDTH + hh * HEAD_DIM:DN_WIDTH + (hh + 1) * HEAD_DIM]
        vh = qkv[:, 2 * DN_WIDTH + hh * HEAD_DIM:2 * DN_WIDTH + (hh + 1) * HEAD_DIM]
        qh = qh * lax.rsqrt(jnp.sum(qh * qh, axis=-1, keepdims=True) + EPS) * (HEAD_DIM ** -0.5)
        kh = kh * lax.rsqrt(jnp.sum(kh * kh, axis=-1, keepdims=True) + EPS)
        g_col = gc[:, hh:hh + 1]
        e_col = e_gc[:, hh:hh + 1]
        b_col = beta[:, DN_HEADS + hh:DN_HEADS + hh + 1]
        q_dec = (qh * e_col).astype(BF16)
        k_dec = (kh * e_rem[:, hh:hh + 1]).astype(BF16)
        rhs = jnp.concatenate([vh * b_col, kh * (b_col * e_col)], axis=1).astype(BF16)
        qb = qh.astype(BF16)
        kb = kh.astype(BF16)
        g_row = gc_t[hh:hh + 1, :]
        state = s_scr[hh]
        o_chunks = []
        for ci in range(tile // chunk):
            rs = slice(ci * chunk, (ci + 1) * chunk)
            decay = jnp.exp(jnp.where(incl, g_col[rs] - g_row[:, rs], -1e30))
            kk = _mm_nt(kb[rs], kb[rs])
            qk = _mm_nt(qb[rs], kb[rs])
            a_inv = _unit_lower_inverse(jnp.where(strict, b_col[rs] * kk * decay, 0.0), chunk)
            sol = _mm(a_inv, rhs[rs])
            state_b = state.astype(BF16)
            v_new = sol[:, :HEAD_DIM] - _mm(sol[:, HEAD_DIM:], state_b)
            o_chunks.append(_mm(q_dec[rs], state_b) + _mm(qk * decay, v_new))
            state = state * e_tot[ci * chunk:ci * chunk + 1, hh:hh + 1] + _mm_tn(k_dec[rs], v_new)
        s_scr[hh] = state
        o_h = jnp.concatenate(o_chunks, axis=0) if len(o_chunks) > 1 else o_chunks[0]
        g_h = gate[:, ls]
        o_heads.append(_rms(o_h, on_ref[...]) * (g_h * jax.nn.sigmoid(g_h)))

    pos = pos0 + step * tile + lax.broadcasted_iota(jnp.int32, (tile, 1), 0)
    z_groups = []
    for gi, win in enumerate(POOL_WINDOWS):
        ls = slice(gi * POOL_GROUP_DIM, (gi + 1) * POOL_GROUP_DIM)
        acc = u[:, ls]
        for back in range(1, win):
            acc = acc + upad[POOL_PAD - back:POOL_PAD - back + tile, ls]
        cnt = jnp.minimum(win, pos + 1).astype(F32)
        z_groups.append(_mm(acc / cnt - u[:, ls], wp_ref[gi]) * ps_ref[:, ls])

    mixed = jnp.concatenate(o_heads + z_groups, axis=1).astype(BF16)
    y_ref[0] = x + jnp.dot(mixed, wout_ref[...], preferred_element_type=F32)

    conv_tail = xpad[CONV_PAD + tile - (CONV_WIDTH - 1):CONV_PAD + tile, :]
    pool_tail = upad[POOL_PAD + tile - POOL_HIST:POOL_PAD + tile, :]
    xpad[CONV_PAD - (CONV_WIDTH - 1):CONV_PAD, :] = conv_tail
    upad[POOL_PAD - POOL_HIST:POOL_PAD, :] = pool_tail
    cn_ref[0] = conv_tail
    pn_ref[0] = pool_tail
    sn_ref[0] = s_scr[...]


def _mixer(x, nw, w_in, conv_w, alog, dtb, o_norm, w_pool, pool_scale, w_out, s0, c0, p0, *, tile, chunk, pos0):
    b, l, d = x.shape
    per_seq = lambda shape: pl.BlockSpec((1,) + shape, lambda i, j: (i,) + (0,) * len(shape))
    out_shapes = (jax.ShapeDtypeStruct(x.shape, x.dtype), jax.ShapeDtypeStruct(s0.shape, s0.dtype),
                  jax.ShapeDtypeStruct(c0.shape, c0.dtype), jax.ShapeDtypeStruct(p0.shape, p0.dtype))
    return pl.pallas_call(
        functools.partial(_mixer_kernel, tile=tile, chunk=chunk, pos0=pos0),
        grid=(b, l // tile),
        in_specs=[pl.BlockSpec((1, tile, d), lambda i, j: (i, j, 0)),
                  _resident(nw.shape), _resident(w_in.shape), _resident(conv_w.shape), _resident(alog.shape),
                  _resident(dtb.shape), _resident(o_norm.shape), _resident(w_pool.shape),
                  _resident(pool_scale.shape), _resident(w_out.shape),
                  per_seq(s0.shape[1:]), per_seq(c0.shape[1:]), per_seq(p0.shape[1:])],
        out_specs=(pl.BlockSpec((1, tile, d), lambda i, j: (i, j, 0)),
                   per_seq(s0.shape[1:]), per_seq(c0.shape[1:]), per_seq(p0.shape[1:])),
        out_shape=out_shapes,
        scratch_shapes=[pltpu.VMEM((DN_HEADS, HEAD_DIM, HEAD_DIM), F32),
                        pltpu.VMEM((CONV_PAD + tile, CONV_CH), F32),
                        pltpu.VMEM((POOL_PAD + tile, POOL_WIDTH), F32)],
        compiler_params=pltpu.CompilerParams(dimension_semantics=("arbitrary", "arbitrary"),
                                             vmem_limit_bytes=VMEM_LIMIT_BYTES),
        name="mixer",
    )(x, nw, w_in, conv_w, alog, dtb, o_norm, w_pool, pool_scale, w_out, s0, c0, p0)


def _head_row(v):
    return jnp.pad(v.astype(F32), ((0, 0), (0, LANES - DN_HEADS)))[:, None, :]


def kernel(x_prompt, x_sample, state_delta, state_conv, state_pool, norm_ffn1, w_ffn1_gate, w_ffn1_up, w_ffn1_down, norm_mix, w_in, conv_w, a_log, dt_bias, o_norm, w_pool, pool_scale, w_out, norm_ffn2, w_ffn2_gate, w_ffn2_up, w_ffn2_down, norm_final):
    depth = w_in.shape[0]
    batch, seq, d_model = x_prompt.shape
    dec_batch, dec_seq, _ = x_sample.shape
    past_len = 4096

    c_ab = CONV_CH + DN_WIDTH
    w_in_r = jnp.concatenate(
        [w_in[..., :c_ab], w_in[..., c_ab + 2 * DN_HEADS:], w_in[..., c_ab:c_ab + 2 * DN_HEADS],
         jnp.zeros(w_in.shape[:2] + (LANES - 2 * DN_HEADS,), w_in.dtype)], axis=-1).astype(BF16)
    wg1, wu1, wd1 = w_ffn1_gate.astype(BF16), w_ffn1_up.astype(BF16), w_ffn1_down.astype(BF16)
    wg2, wu2, wd2 = w_ffn2_gate.astype(BF16), w_ffn2_up.astype(BF16), w_ffn2_down.astype(BF16)
    w_out_b = w_out.astype(BF16)
    w_pool_b = w_pool.astype(BF16)
    alog_r, dtb_r = _head_row(a_log), _head_row(dt_bias)
    row = lambda v: v[:, None, :]
    n1, nm, n2, onr, psr = row(norm_ffn1), row(norm_mix), row(norm_ffn2), row(o_norm), row(pool_scale)
    nf = norm_final[None, :]

    zeros_like_state = lambda s: jnp.zeros((batch,) + s.shape[2:], s.dtype)
    groups = [
        dict(x=x_prompt, tm=512, tile=256, chunk=64, pos0=0,
             state=[(zeros_like_state(state_delta), zeros_like_state(state_conv), zeros_like_state(state_pool))] * depth),
        dict(x=x_sample, tm=dec_batch * dec_seq, tile=dec_seq, chunk=dec_seq, pos0=past_len,
             state=[(state_delta[l], state_conv[l], state_pool[l]) for l in range(depth)]),
    ]
    results = []
    for grp in groups:
        x = grp["x"]
        b, l, _ = x.shape
        new_s, new_c, new_p = [], [], []
        for layer in range(depth):
            x2 = _ffn(x.reshape(b * l, d_model), n1[layer], wg1[layer], wu1[layer], wd1[layer], tm=grp["tm"])
            s0, c0, p0 = grp["state"][layer]
            x, s_n, c_n, p_n = _mixer(x2.reshape(b, l, d_model), nm[layer], w_in_r[layer], conv_w[layer],
                                      alog_r[layer], dtb_r[layer], onr[layer], w_pool_b[layer], psr[layer],
                                      w_out_b[layer], s0, c0, p0,
                                      tile=grp["tile"], chunk=grp["chunk"], pos0=grp["pos0"])
            fw = nf if layer == depth - 1 else None
            x = _ffn(x.reshape(b * l, d_model), n2[layer], wg2[layer], wu2[layer], wd2[layer], fw,
                     tm=grp["tm"]).reshape(b, l, d_model)
            new_s.append(s_n); new_c.append(c_n); new_p.append(p_n)
        results.append((x, jnp.stack(new_s), jnp.stack(new_c), jnp.stack(new_p)))
    (y_p, s_p, c_p, q_p), (y_s, s_s, c_s, q_s) = results
    return (y_p, y_s, s_p, c_p, q_p, s_s, c_s, q_s)
```

```python
import functools

import jax
import jax.numpy as jnp
from jax import lax
from jax.experimental import pallas as pl
from jax.experimental.pallas import tpu as pltpu

EPS = 1e-6
PAST_LEN = 4096
DN_HEADS = 4
HEAD_DIM = 128
DN_WIDTH = DN_HEADS * HEAD_DIM
CONV_WIDTH = 4
CONV_CH = 3 * DN_WIDTH
POOL_WINDOWS = (2, 4, 8, 16)
POOL_GROUP_DIM = 128
POOL_WIDTH = len(POOL_WINDOWS) * POOL_GROUP_DIM
POOL_HIST = max(POOL_WINDOWS) - 1
PROMPT_CHUNK = 64
LANES = 128
SUBLANES = 8
INV_BASE = 16
COL_GATE = CONV_CH
COL_POOL = COL_GATE + DN_WIDTH
COL_AB = COL_POOL + POOL_WIDTH
N_IN_PAD = COL_AB + LANES
CONV_PAD = SUBLANES
POOL_PAD = 2 * SUBLANES
VMEM_LIMIT_BYTES = 56 * 1024 * 1024

F32 = jnp.float32
BF16 = jnp.bfloat16


def _rms(x, w):
    return x * lax.rsqrt(jnp.mean(x * x, axis=-1, keepdims=True) + EPS) * w


def _mm(a, b):
    return jnp.dot(a.astype(BF16), b.astype(BF16), preferred_element_type=F32)


def _mm_nt(a, b):
    return lax.dot_general(a.astype(BF16), b.astype(BF16), (((1,), (1,)), ((), ())),
                           preferred_element_type=F32)


def _mm_tn(a, b):
    return lax.dot_general(a.astype(BF16), b.astype(BF16), (((0,), (0,)), ((), ())),
                           preferred_element_type=F32)


def _mm_exact(a, b):
    return jnp.dot(a, b, precision=lax.Precision.HIGHEST, preferred_element_type=F32)


def _ffn_kernel(*refs, f_chunk, final_norm):
    if final_norm:
        x_ref, nw_ref, wg_ref, wu_ref, wd_ref, fw_ref, o_ref, h_ref = refs
    else:
        x_ref, nw_ref, wg_ref, wu_ref, wd_ref, o_ref, h_ref = refs
    x = x_ref[...]
    xn = _rms(x, nw_ref[...]).astype(BF16)
    d_ff = wg_ref.shape[1]
    for c in range(d_ff // f_chunk):
        sl = slice(c * f_chunk, (c + 1) * f_chunk)
        g = jnp.dot(xn, wg_ref[:, sl], preferred_element_type=F32)
        u = jnp.dot(xn, wu_ref[:, sl], preferred_element_type=F32)
        h_ref[:, sl] = (g * jax.nn.sigmoid(g) * u).astype(BF16)
    y = x + 0.5 * jnp.dot(h_ref[...], wd_ref[...], preferred_element_type=F32)
    if final_norm:
        y = _rms(y, fw_ref[...])
    o_ref[...] = y


def _resident(shape):
    zeros = (0,) * len(shape)
    return pl.BlockSpec(shape, lambda *_: zeros, pipeline_mode=pl.Buffered(1))


def _ffn(x, nw, wg, wu, wd, fw=None, *, tm):
    m, d = x.shape
    d_ff = wg.shape[1]
    final_norm = fw is not None
    args = [x, nw, wg, wu, wd] + ([fw] if final_norm else [])
    in_specs = [pl.BlockSpec((tm, d), lambda i: (i, 0)), _resident((1, d)), _resident((d, d_ff)),
                _resident((d, d_ff)), _resident((d_ff, d))] + ([_resident((1, d))] if final_norm else [])
    return pl.pallas_call(
        functools.partial(_ffn_kernel, f_chunk=256, final_norm=final_norm),
        grid=(m // tm,),
        in_specs=in_specs,
        out_specs=pl.BlockSpec((tm, d), lambda i: (i, 0)),
        out_shape=jax.ShapeDtypeStruct((m, d), x.dtype),
        scratch_shapes=[pltpu.VMEM((tm, d_ff), BF16)],
        compiler_params=pltpu.CompilerParams(dimension_semantics=("arbitrary",),
                                             vmem_limit_bytes=VMEM_LIMIT_BYTES),
        name="ffn",
    )(*args)


def _unit_lower_inverses(ns, chunk):
    r = lax.broadcasted_iota(jnp.int32, (chunk, chunk), 0)
    c = lax.broadcasted_iota(jnp.int32, (chunk, chunk), 1)
    base = min(chunk, INV_BASE)
    shift = base.bit_length() - 1
    if chunk > base:
        diag = (r >> shift) == (c >> shift)
        nds = [jnp.where(diag, n, 0.0) for n in ns]
    else:
        nds = ns
    eye = jnp.where(r == c, 1.0, 0.0)
    invs = [eye - nd for nd in nds]
    pws = [nd.astype(BF16) for nd in nds]
    pws = [jnp.dot(pw, pw, preferred_element_type=F32).astype(BF16) for pw in pws]
    order = 2
    while 2 * order < base:
        both = [jnp.dot(jnp.concatenate([inv.astype(BF16), pw], axis=0), pw, preferred_element_type=F32)
                for inv, pw in zip(invs, pws)]
        invs = [inv + bo[:chunk] for inv, bo in zip(invs, both)]
        pws = [bo[chunk:].astype(BF16) for bo in both]
        order *= 2
    invs = [inv + _mm(inv, pw) for inv, pw in zip(invs, pws)]
    blk = base
    while blk < chunk:
        s = blk.bit_length() - 1
        lower = ((r >> (s + 1)) == (c >> (s + 1))) & ((r >> s) > (c >> s))
        tmp = [_mm(inv, jnp.where(lower, n, 0.0)) for inv, n in zip(invs, ns)]
        invs = [inv - _mm(t, inv) for inv, t in zip(invs, tmp)]
        blk *= 2
    return invs


def _mixer_kernel(x_ref, nw_ref, win_ref, cw_ref, alog_ref, dtb_ref, on_ref, wp_ref, ps_ref, wout_ref,
                  s0_ref, c0_ref, p0_ref,
                  y_ref, sn_ref, cn_ref, pn_ref,
                  s_scr, xpad, upad, *, nseq, ls, chunk, pos0):
    step = pl.program_id(1)
    tile = nseq * ls
    cps = ls // chunk

    @pl.when(step == 0)
    def _():
        s_scr[...] = s0_ref[...]
        xpad[:, CONV_PAD - (CONV_WIDTH - 1):CONV_PAD, :] = c0_ref[...]
        upad[:, POOL_PAD - POOL_HIST:POOL_PAD, :] = p0_ref[...]

    x = x_ref[...].reshape(tile, x_ref.shape[-1])
    h = _rms(x, nw_ref[...]).astype(BF16)
    p = jnp.dot(h, win_ref[...], preferred_element_type=F32)
    pre = p[:, :CONV_CH]
    gate = p[:, COL_GATE:COL_POOL]
    u = p[:, COL_POOL:COL_AB]
    ab = p[:, COL_AB:]
    xpad[:, CONV_PAD:CONV_PAD + ls, :] = pre.reshape(nseq, ls, CONV_CH)
    upad[:, POOL_PAD:POOL_PAD + ls, :] = u.reshape(nseq, ls, POOL_WIDTH)

    cw = cw_ref[...]
    tap = lambda j: xpad[:, CONV_PAD - 3 + j:CONV_PAD - 3 + j + ls, :].reshape(tile, CONV_CH)
    conv = tap(0) * cw[0:1]
    conv = conv + tap(1) * cw[1:2]
    conv = conv + tap(2) * cw[2:3]
    conv = conv + pre * cw[3:4]
    qkv = conv * jax.nn.sigmoid(conv)

    a_pre = ab + dtb_ref[...]
    softplus = jnp.maximum(a_pre, 0.0) + jnp.log1p(jnp.exp(-jnp.abs(a_pre)))
    g = -jnp.exp(alog_ref[...]) * softplus
    beta = jax.nn.sigmoid(ab)

    cshift = chunk.bit_length() - 1
    tr = lax.broadcasted_iota(jnp.int32, (tile, tile), 0)
    tc = lax.broadcasted_iota(jnp.int32, (tile, tile), 1)
    same = (tr >> cshift) == (tc >> cshift)
    gc = _mm_exact(jnp.where(same & (tr >= tc), 1.0, 0.0), g)
    gs = _mm_exact(jnp.where(same, 1.0, 0.0), g)
    gc_t = gc.T
    e_gc = jnp.exp(gc)
    e_rem = jnp.exp(gs - gc)
    e_tot = jnp.exp(gs)

    cr = lax.broadcasted_iota(jnp.int32, (chunk, chunk), 0)
    cc = lax.broadcasted_iota(jnp.int32, (chunk, chunk), 1)
    incl = cr >= cc
    strict = cr > cc

    probs = [(hh, s, ci) for hh in range(DN_HEADS) for s in range(nseq) for ci in range(cps)]
    rows = lambda s, ci: slice(s * ls + ci * chunk, s * ls + (ci + 1) * chunk)
    q_dec, k_dec, rhs, qb, kb, b_col = {}, {}, {}, {}, {}, {}
    for hh in range(DN_HEADS):
        qh = qkv[:, hh * HEAD_DIM:(hh + 1) * HEAD_DIM]
        kh = qkv[:, DN_WIDTH + hh * HEAD_DIM:DN_WIDTH + (hh + 1) * HEAD_DIM]
        vh = qkv[:, 2 * DN_WIDTH + hh * HEAD_DIM:2 * DN_WIDTH + (hh + 1) * HEAD_DIM]
        qh = qh * lax.rsqrt(jnp.sum(qh * qh, axis=-1, keepdims=True) + EPS) * (HEAD_DIM ** -0.5)
        kh = kh * lax.rsqrt(jnp.sum(kh * kh, axis=-1, keepdims=True) + EPS)
        e_col = e_gc[:, hh:hh + 1]
        b_col[hh] = beta[:, DN_HEADS + hh:DN_HEADS + hh + 1]
        q_dec[hh] = qh * e_col
        k_dec[hh] = (kh * e_rem[:, hh:hh + 1]).astype(BF16)
        rhs[hh] = jnp.concatenate([vh * b_col[hh], kh * (b_col[hh] * e_col)], axis=1).astype(BF16)
        qb[hh] = qh.astype(BF16)
        kb[hh] = kh.astype(BF16)
    decay = [jnp.exp(jnp.where(incl, gc[rows(s, ci), hh:hh + 1] - gc_t[hh:hh + 1, rows(s, ci)], -1e30))
             for hh, s, ci in probs]
    kk = [_mm_nt(kb[hh][rows(s, ci)], kb[hh][rows(s, ci)]) for hh, s, ci in probs]
    qk = [_mm_nt(qb[hh][rows(s, ci)], kb[hh][rows(s, ci)]) for hh, s, ci in probs]
    a_inv = _unit_lower_inverses(
        [jnp.where(strict, b_col[hh][rows(s, ci)] * kk_p * dec_p, 0.0)
         for (hh, s, ci), kk_p, dec_p in zip(probs, kk, decay)], chunk)
    sol = [_mm(inv_p, rhs[hh][rows(s, ci)]).astype(BF16) for (hh, s, ci), inv_p in zip(probs, a_inv)]
    qw = [_mm(qk_p * dec_p, sol_p) for qk_p, dec_p, sol_p in zip(qk, decay, sol)]
    kw = [_mm_tn(k_dec[hh][rows(s, ci)], sol_p) for (hh, s, ci), sol_p in zip(probs, sol)]

    lhs = {pr: jnp.concatenate([q_dec[pr[0]][rows(pr[1], pr[2])] - qw_p[:, HEAD_DIM:], kw_p[:, HEAD_DIM:]],
                               axis=0).astype(BF16)
           for pr, qw_p, kw_p in zip(probs, qw, kw)}
    o_add = {pr: qw_p[:, :HEAD_DIM] for pr, qw_p in zip(probs, qw)}
    s_add = {pr: kw_p[:, :HEAD_DIM] for pr, kw_p in zip(probs, kw)}
    states = {(hh, s): s_scr[s, hh] for hh in range(DN_HEADS) for s in range(nseq)}
    o_chunk = {}
    for ci in range(cps):
        for hh in range(DN_HEADS):
            for s in range(nseq):
                pr = (hh, s, ci)
                both = jnp.dot(lhs[pr], states[hh, s].astype(BF16), preferred_element_type=F32)
                o_chunk[pr] = o_add[pr] + both[:chunk]
                first = s * ls + ci * chunk
                states[hh, s] = states[hh, s] * e_tot[first:first + 1, hh:hh + 1] - both[chunk:] + s_add[pr]
    o_heads = []
    for hh in range(DN_HEADS):
        for s in range(nseq):
            s_scr[s, hh] = states[hh, s]
        pieces = [o_chunk[hh, s, ci] for s in range(nseq) for ci in range(cps)]
        o_h = jnp.concatenate(pieces, axis=0) if len(pieces) > 1 else pieces[0]
        g_h = gate[:, hh * HEAD_DIM:(hh + 1) * HEAD_DIM]
        o_heads.append(_rms(o_h, on_ref[...]) * (g_h * jax.nn.sigmoid(g_h)))

    row = lax.broadcasted_iota(jnp.int32, (tile, 1), 0)
    pos = pos0 + step * ls + (row & (ls - 1))
    z_groups = []
    for gi, win in enumerate(POOL_WINDOWS):
        lanes = slice(gi * POOL_GROUP_DIM, (gi + 1) * POOL_GROUP_DIM)
        acc = u[:, lanes]
        for back in range(1, win):
            acc = acc + upad[:, POOL_PAD - back:POOL_PAD - back + ls, lanes].reshape(tile, POOL_GROUP_DIM)
        cnt = jnp.minimum(win, pos + 1).astype(F32)
        z_groups.append(_mm(acc / cnt - u[:, lanes], wp_ref[gi]) * ps_ref[:, lanes])

    mixed = jnp.concatenate(o_heads + z_groups, axis=1).astype(BF16)
    y = x + jnp.dot(mixed, wout_ref[...], preferred_element_type=F32)
    y_ref[...] = y.reshape(y_ref.shape)

    conv_tail = xpad[:, CONV_PAD + ls - (CONV_WIDTH - 1):CONV_PAD + ls, :]
    pool_tail = upad[:, POOL_PAD + ls - POOL_HIST:POOL_PAD + ls, :]
    xpad[:, CONV_PAD - (CONV_WIDTH - 1):CONV_PAD, :] = conv_tail
    upad[:, POOL_PAD - POOL_HIST:POOL_PAD, :] = pool_tail
    cn_ref[...] = conv_tail
    pn_ref[...] = pool_tail
    sn_ref[...] = s_scr[...]


def _mixer(x, nw, w_in, conv_w, alog, dtb, o_norm, w_pool, pool_scale, w_out, s0, c0, p0, *, nseq, ls, chunk, pos0):
    b, l, d = x.shape
    assert ls & (ls - 1) == 0 and chunk & (chunk - 1) == 0 and ls % chunk == 0
    assert b % nseq == 0 and l % ls == 0 and (nseq == 1 or l == ls)
    per_seq = lambda shape: pl.BlockSpec((nseq,) + shape, lambda i, j: (i,) + (0,) * len(shape))
    out_shapes = (jax.ShapeDtypeStruct(x.shape, x.dtype), jax.ShapeDtypeStruct(s0.shape, s0.dtype),
                  jax.ShapeDtypeStruct(c0.shape, c0.dtype), jax.ShapeDtypeStruct(p0.shape, p0.dtype))
    return pl.pallas_call(
        functools.partial(_mixer_kernel, nseq=nseq, ls=ls, chunk=chunk, pos0=pos0),
        grid=(b // nseq, l // ls),
        in_specs=[pl.BlockSpec((nseq, ls, d), lambda i, j: (i, j, 0)),
                  _resident(nw.shape), _resident(w_in.shape), _resident(conv_w.shape), _resident(alog.shape),
                  _resident(dtb.shape), _resident(o_norm.shape), _resident(w_pool.shape),
                  _resident(pool_scale.shape), _resident(w_out.shape),
                  per_seq(s0.shape[1:]), per_seq(c0.shape[1:]), per_seq(p0.shape[1:])],
        out_specs=(pl.BlockSpec((nseq, ls, d), lambda i, j: (i, j, 0)),
                   per_seq(s0.shape[1:]), per_seq(c0.shape[1:]), per_seq(p0.shape[1:])),
        out_shape=out_shapes,
        scratch_shapes=[pltpu.VMEM((nseq, DN_HEADS, HEAD_DIM, HEAD_DIM), F32),
                        pltpu.VMEM((nseq, CONV_PAD + ls, CONV_CH), F32),
                        pltpu.VMEM((nseq, POOL_PAD + ls, POOL_WIDTH), F32)],
        compiler_params=pltpu.CompilerParams(dimension_semantics=("arbitrary", "arbitrary"),
                                             vmem_limit_bytes=VMEM_LIMIT_BYTES),
        name="mixer",
    )(x, nw, w_in, conv_w, alog, dtb, o_norm, w_pool, pool_scale, w_out, s0, c0, p0)


def _head_row(v):
    return jnp.pad(v.astype(F32), ((0, 0), (0, LANES - DN_HEADS)))[:, None, :]


def kernel(x_prompt, x_sample, state_delta, state_conv, state_pool, norm_ffn1, w_ffn1_gate, w_ffn1_up, w_ffn1_down, norm_mix, w_in, conv_w, a_log, dt_bias, o_norm, w_pool, pool_scale, w_out, norm_ffn2, w_ffn2_gate, w_ffn2_up, w_ffn2_down, norm_final):
    depth = w_in.shape[0]
    batch, seq, d_model = x_prompt.shape
    dec_batch, dec_seq, _ = x_sample.shape

    c_ab = CONV_CH + DN_WIDTH
    w_in_r = jnp.concatenate(
        [w_in[..., :c_ab], w_in[..., c_ab + 2 * DN_HEADS:], w_in[..., c_ab:c_ab + 2 * DN_HEADS],
         jnp.zeros(w_in.shape[:2] + (LANES - 2 * DN_HEADS,), w_in.dtype)], axis=-1).astype(BF16)
    wg1, wu1, wd1 = w_ffn1_gate.astype(BF16), w_ffn1_up.astype(BF16), w_ffn1_down.astype(BF16)
    wg2, wu2, wd2 = w_ffn2_gate.astype(BF16), w_ffn2_up.astype(BF16), w_ffn2_down.astype(BF16)
    w_out_b = w_out.astype(BF16)
    w_pool_b = w_pool.astype(BF16)
    alog_r, dtb_r = _head_row(a_log), _head_row(dt_bias)
    row = lambda v: v[:, None, :]
    n1, nm, n2, onr, psr = row(norm_ffn1), row(norm_mix), row(norm_ffn2), row(o_norm), row(pool_scale)
    nf = norm_final[None, :]

    zeros_like_state = lambda s: jnp.zeros((batch,) + s.shape[2:], s.dtype)
    groups = [
        dict(x=x_prompt, tm=512, nseq=1, ls=256, chunk=PROMPT_CHUNK, pos0=0,
             state=[(zeros_like_state(state_delta), zeros_like_state(state_conv), zeros_like_state(state_pool))] * depth),
        dict(x=x_sample, tm=dec_batch * dec_seq, nseq=dec_batch, ls=dec_seq, chunk=dec_seq, pos0=PAST_LEN,
             state=[(state_delta[l], state_conv[l], state_pool[l]) for l in range(depth)]),
    ]
    results = []
    for grp in groups:
        x = grp["x"]
        b, l, _ = x.shape
        new_s, new_c, new_p = [], [], []
        for layer in range(depth):
            x2 = _ffn(x.reshape(b * l, d_model), n1[layer], wg1[layer], wu1[layer], wd1[layer], tm=grp["tm"])
            s0, c0, p0 = grp["state"][layer]
            x, s_n, c_n, p_n = _mixer(x2.reshape(b, l, d_model), nm[layer], w_in_r[layer], conv_w[layer],
                                      alog_r[layer], dtb_r[layer], onr[layer], w_pool_b[layer], psr[layer],
                                      w_out_b[layer], s0, c0, p0,
                                      nseq=grp["nseq"], ls=grp["ls"], chunk=grp["chunk"], pos0=grp["pos0"])
            fw = nf if layer == depth - 1 else None
            x = _ffn(x.reshape(b * l, d_model), n2[layer], wg2[layer], wu2[layer], wd2[layer], fw,
                     tm=grp["tm"]).reshape(b, l, d_model)
            new_s.append(s_n); new_c.append(c_n); new_p.append(p_n)
        results.append((x, jnp.stack(new_s), jnp.stack(new_c), jnp.stack(new_p)))
    (y_p, s_p, c_p, q_p), (y_s, s_s, c_s, q_s) = results
    return (y_p, y_s, s_p, c_p, q_p, s_s, c_s, q_s)
```

```python
import functools

import numpy as np
import jax
import jax.numpy as jnp
from jax import lax
from jax.experimental import pallas as pl
from jax.experimental.pallas import tpu as pltpu

EPS = 1e-6
PAST_LEN = 4096
DN_HEADS = 4
HEAD_DIM = 128
DN_WIDTH = DN_HEADS * HEAD_DIM
CONV_WIDTH = 4
CONV_CH = 3 * DN_WIDTH
POOL_WINDOWS = (2, 4, 8, 16)
POOL_GROUP_DIM = 128
POOL_WIDTH = len(POOL_WINDOWS) * POOL_GROUP_DIM
POOL_HIST = max(POOL_WINDOWS) - 1
PROMPT_CHUNK = 64
LANES = 128
SUBLANES = 8
INV_BASE = 16
COL_GATE = CONV_CH
COL_POOL = COL_GATE + DN_WIDTH
N_IN_MAIN = COL_POOL + POOL_WIDTH
CONV_PAD = SUBLANES
POOL_PAD = 2 * SUBLANES
F_QN, F_KN, F_QDEC, F_KDEC, F_W, F_U, F_GC = (SUBLANES * i for i in range(7))
VMEM_LIMIT_BYTES = 56 * 1024 * 1024

F32 = jnp.float32
BF16 = jnp.bfloat16
NT_DIMS = (((1,), (1,)), ((), ()))
TN_DIMS = (((0,), (0,)), ((), ()))


def _rms(x, w):
    return x * lax.rsqrt(jnp.mean(x * x, axis=-1, keepdims=True) + EPS) * w


def _mm(a, b):
    return jnp.dot(a.astype(BF16), b.astype(BF16), preferred_element_type=F32)


def _mm_nt(a, b):
    return lax.dot_general(a.astype(BF16), b.astype(BF16), NT_DIMS, preferred_element_type=F32)


def _mm_tn(a, b):
    return lax.dot_general(a.astype(BF16), b.astype(BF16), TN_DIMS, preferred_element_type=F32)


def _resident(shape, layer=None):
    zeros = (0,) * len(shape)
    if layer is None:
        return pl.BlockSpec(shape, lambda *_: zeros, pipeline_mode=pl.Buffered(1))
    return pl.BlockSpec((None,) + shape, lambda *_: (layer,) + zeros, pipeline_mode=pl.Buffered(1))


def _ffn_kernel(*refs, f_chunk, final_norm):
    if final_norm:
        x_ref, nw_ref, wg_ref, wu_ref, wd_ref, fw_ref, o_ref, h_ref = refs
    else:
        x_ref, nw_ref, wg_ref, wu_ref, wd_ref, o_ref, h_ref = refs
    x = x_ref[...]
    xn = _rms(x, nw_ref[...]).astype(BF16)
    d_ff = wg_ref.shape[1]
    for c in range(d_ff // f_chunk):
        sl = slice(c * f_chunk, (c + 1) * f_chunk)
        g = jnp.dot(xn, wg_ref[:, sl], preferred_element_type=F32)
        u = jnp.dot(xn, wu_ref[:, sl], preferred_element_type=F32)
        h_ref[:, sl] = (g * jax.nn.sigmoid(g) * u).astype(BF16)
    y = x + 0.5 * jnp.dot(h_ref[...], wd_ref[...], preferred_element_type=F32)
    if final_norm:
        y = _rms(y, fw_ref[...])
    o_ref[...] = y


def _ffn(x, nw, wg, wu, wd, fw=None, *, layer, tm):
    m, d = x.shape
    d_ff = wg.shape[-1]
    final_norm = fw is not None
    args = [x, nw, wg, wu, wd] + ([fw] if final_norm else [])
    in_specs = [pl.BlockSpec((tm, d), lambda i: (i, 0)), _resident((1, d), layer), _resident((d, d_ff), layer),
                _resident((d, d_ff), layer), _resident((d_ff, d), layer)]
    in_specs += [_resident((1, d))] if final_norm else []
    return pl.pallas_call(
        functools.partial(_ffn_kernel, f_chunk=256, final_norm=final_norm),
        grid=(m // tm,),
        in_specs=in_specs,
        out_specs=pl.BlockSpec((tm, d), lambda i: (i, 0)),
        out_shape=jax.ShapeDtypeStruct((m, d), x.dtype),
        scratch_shapes=[pltpu.VMEM((tm, d_ff), BF16)],
        compiler_params=pltpu.CompilerParams(dimension_semantics=("arbitrary",),
                                             vmem_limit_bytes=VMEM_LIMIT_BYTES),
        name="ffn",
    )(*args)


def _unit_lower_inverses(ns, chunk):
    r = lax.broadcasted_iota(jnp.int32, (chunk, chunk), 0)
    c = lax.broadcasted_iota(jnp.int32, (chunk, chunk), 1)
    base = min(chunk, INV_BASE)
    shift = base.bit_length() - 1
    if chunk > base:
        diag = (r >> shift) == (c >> shift)
        nds = [jnp.where(diag, n, 0.0) for n in ns]
    else:
        nds = ns
    eye = jnp.where(r == c, 1.0, 0.0)
    invs = [eye - nd for nd in nds]
    pws = [nd.astype(BF16) for nd in nds]
    pws = [jnp.dot(pw, pw, preferred_element_type=F32).astype(BF16) for pw in pws]
    order = 2
    while 2 * order < base:
        both = [jnp.dot(jnp.concatenate([inv.astype(BF16), pw], axis=0), pw, preferred_element_type=F32)
                for inv, pw in zip(invs, pws)]
        invs = [inv + bo[:chunk] for inv, bo in zip(invs, both)]
        pws = [bo[chunk:].astype(BF16) for bo in both]
        order *= 2
    invs = [inv + _mm(inv, pw) for inv, pw in zip(invs, pws)]
    blk = base
    while blk < chunk:
        s = blk.bit_length() - 1
        lower = ((r >> (s + 1)) == (c >> (s + 1))) & ((r >> s) > (c >> s))
        tmp = [_mm(inv, jnp.where(lower, n, 0.0)) for inv, n in zip(invs, ns)]
        invs = [inv - _mm(t, inv) for inv, t in zip(invs, tmp)]
        blk *= 2
    return invs


def _mixer_kernel(x_ref, nw_ref, win_ref, wab_ref, sel_ref, cum_ref, cw_ref, alog_ref, dtb_ref, on_ref, wp_ref,
                  ps_ref, wout_ref, s0_ref, c0_ref, p0_ref,
                  y_ref, sn_ref, cn_ref, pn_ref,
                  s_scr, xpad, upad, *, nseq, ls, chunk, pos0):
    step = pl.program_id(1)
    tile = nseq * ls
    cps = ls // chunk

    @pl.when(step == 0)
    def _():
        s_scr[...] = s0_ref[...]
        xpad[:, CONV_PAD - (CONV_WIDTH - 1):CONV_PAD, :] = c0_ref[...]
        upad[:, POOL_PAD - POOL_HIST:POOL_PAD, :] = p0_ref[...]

    x = x_ref[...].reshape(tile, x_ref.shape[-1])
    h = _rms(x, nw_ref[...]).astype(BF16)
    p = jnp.dot(h, win_ref[...], preferred_element_type=F32)
    pre = p[:, :CONV_CH]
    gate = p[:, COL_GATE:COL_POOL]
    u = p[:, COL_POOL:]
    xpad[:, CONV_PAD:CONV_PAD + ls, :] = pre.reshape(nseq, ls, CONV_CH)
    upad[:, POOL_PAD:POOL_PAD + ls, :] = u.reshape(nseq, ls, POOL_WIDTH)

    cw = cw_ref[...]
    tap = lambda j: xpad[:, CONV_PAD - 3 + j:CONV_PAD - 3 + j + ls, :].reshape(tile, CONV_CH)
    conv = tap(0) * cw[0:1]
    conv = conv + tap(1) * cw[1:2]
    conv = conv + tap(2) * cw[2:3]
    conv = conv + pre * cw[3:4]
    qkv = conv * jax.nn.sigmoid(conv)

    ab_t = lax.dot_general(wab_ref[...], h, NT_DIMS, preferred_element_type=F32)
    qk_part = qkv[:, :2 * DN_WIDTH]
    ss_t = lax.dot_general(sel_ref[...], (qk_part * qk_part).astype(BF16), NT_DIMS,
                           preferred_element_type=F32)
    a_pre = ab_t[:SUBLANES] + dtb_ref[...]
    softplus = jnp.maximum(a_pre, 0.0) + jnp.log1p(jnp.exp(-jnp.abs(a_pre)))
    g = -jnp.exp(alog_ref[...]) * softplus
    beta = jax.nn.sigmoid(ab_t[SUBLANES:])
    r_q = lax.rsqrt(ss_t[:SUBLANES] + EPS) * (HEAD_DIM ** -0.5)
    r_k = lax.rsqrt(ss_t[SUBLANES:] + EPS)
    g_hi = g.astype(BF16).astype(F32)
    g_mid = (g - g_hi).astype(BF16).astype(F32)
    g_lo = g - g_hi - g_mid
    pieces = jnp.concatenate([g_hi, g_mid, g_lo, jnp.zeros_like(g)], axis=0).astype(BF16)
    sums = jnp.dot(pieces, cum_ref[...], preferred_element_type=F32)
    part = lambda k, half: sums[k * SUBLANES:(k + 1) * SUBLANES, half * tile:(half + 1) * tile]
    gc = part(0, 0) + part(1, 0) + part(2, 0)
    gs = part(0, 1) + part(1, 1) + part(2, 1)
    e_gc = jnp.exp(gc)
    e_tot = jnp.exp(gs)
    factors = [r_q, r_k, r_q * e_gc, r_k * jnp.exp(gs - gc), r_k * (beta * e_gc), beta, gc]
    f_t = jnp.concatenate(factors + [jnp.zeros((LANES - SUBLANES * len(factors), tile), F32)], axis=0).T

    cr = lax.broadcasted_iota(jnp.int32, (chunk, chunk), 0)
    cc = lax.broadcasted_iota(jnp.int32, (chunk, chunk), 1)
    incl = cr >= cc
    strict = cr > cc

    probs = [(hh, s, ci) for hh in range(DN_HEADS) for s in range(nseq) for ci in range(cps)]
    rows = lambda s, ci: slice(s * ls + ci * chunk, s * ls + (ci + 1) * chunk)
    col = lambda base, hh: f_t[:, base + hh:base + hh + 1]
    q_dec, k_dec, rhs, qb, kb = {}, {}, {}, {}, {}
    for hh in range(DN_HEADS):
        qh = qkv[:, hh * HEAD_DIM:(hh + 1) * HEAD_DIM]
        kh = qkv[:, DN_WIDTH + hh * HEAD_DIM:DN_WIDTH + (hh + 1) * HEAD_DIM]
        vh = qkv[:, 2 * DN_WIDTH + hh * HEAD_DIM:2 * DN_WIDTH + (hh + 1) * HEAD_DIM]
        q_dec[hh] = qh * col(F_QDEC, hh)
        k_dec[hh] = (kh * col(F_KDEC, hh)).astype(BF16)
        rhs[hh] = jnp.concatenate([vh * col(F_U, hh), kh * col(F_W, hh)], axis=1).astype(BF16)
        qb[hh] = (qh * col(F_QN, hh)).astype(BF16)
        kb[hh] = (kh * col(F_KN, hh)).astype(BF16)
    decay = [jnp.exp(jnp.where(incl, col(F_GC, hh)[rows(s, ci)] - gc[hh:hh + 1, rows(s, ci)], -1e30))
             for hh, s, ci in probs]
    kk = [_mm_nt(kb[hh][rows(s, ci)], kb[hh][rows(s, ci)]) for hh, s, ci in probs]
    qk = [_mm_nt(qb[hh][rows(s, ci)], kb[hh][rows(s, ci)]) for hh, s, ci in probs]
    a_inv = _unit_lower_inverses(
        [jnp.where(strict, col(F_U, hh)[rows(s, ci)] * kk_p * dec_p, 0.0)
         for (hh, s, ci), kk_p, dec_p in zip(probs, kk, decay)], chunk)
    sol = [_mm(inv_p, rhs[hh][rows(s, ci)]).astype(BF16) for (hh, s, ci), inv_p in zip(probs, a_inv)]
    qw = [_mm(qk_p * dec_p, sol_p) for qk_p, dec_p, sol_p in zip(qk, decay, sol)]
    kw = [_mm_tn(k_dec[hh][rows(s, ci)], sol_p) for (hh, s, ci), sol_p in zip(probs, sol)]

    lhs = {pr: jnp.concatenate([q_dec[pr[0]][rows(pr[1], pr[2])] - qw_p[:, HEAD_DIM:], kw_p[:, HEAD_DIM:]],
                               axis=0).astype(BF16)
           for pr, qw_p, kw_p in zip(probs, qw, kw)}
    o_add = {pr: qw_p[:, :HEAD_DIM] for pr, qw_p in zip(probs, qw)}
    s_add = {pr: kw_p[:, :HEAD_DIM] for pr, kw_p in zip(probs, kw)}
    states = {(hh, s): s_scr[s, hh] for hh in range(DN_HEADS) for s in range(nseq)}
    o_chunk = {}
    for ci in range(cps):
        for hh in range(DN_HEADS):
            for s in range(nseq):
                pr = (hh, s, ci)
                both = jnp.dot(lhs[pr], states[hh, s].astype(BF16), preferred_element_type=F32)
                o_chunk[pr] = o_add[pr] + both[:chunk]
                first = s * ls + ci * chunk
                states[hh, s] = states[hh, s] * e_tot[hh:hh + 1, first:first + 1] - both[chunk:] + s_add[pr]
    o_heads = []
    for hh in range(DN_HEADS):
        for s in range(nseq):
            s_scr[s, hh] = states[hh, s]
        o_parts = [o_chunk[hh, s, ci] for s in range(nseq) for ci in range(cps)]
        o_h = jnp.concatenate(o_parts, axis=0) if len(o_parts) > 1 else o_parts[0]
        g_h = gate[:, hh * HEAD_DIM:(hh + 1) * HEAD_DIM]
        o_heads.append(_rms(o_h, on_ref[...]) * (g_h * jax.nn.sigmoid(g_h)))

    row = lax.broadcasted_iota(jnp.int32, (tile, 1), 0)
    pos = pos0 + step * ls + (row & (ls - 1))
    z_groups = []
    for gi, win in enumerate(POOL_WINDOWS):
        lanes = slice(gi * POOL_GROUP_DIM, (gi + 1) * POOL_GROUP_DIM)
        acc = u[:, lanes]
        for back in range(1, win):
            acc = acc + upad[:, POOL_PAD - back:POOL_PAD - back + ls, lanes].reshape(tile, POOL_GROUP_DIM)
        cnt = jnp.minimum(win, pos + 1).astype(F32)
        z_groups.append(_mm(acc / cnt - u[:, lanes], wp_ref[gi]) * ps_ref[:, lanes])

    mixed = jnp.concatenate(o_heads + z_groups, axis=1).astype(BF16)
    y = x + jnp.dot(mixed, wout_ref[...], preferred_element_type=F32)
    y_ref[...] = y.reshape(y_ref.shape)

    conv_tail = xpad[:, CONV_PAD + ls - (CONV_WIDTH - 1):CONV_PAD + ls, :]
    pool_tail = upad[:, POOL_PAD + ls - POOL_HIST:POOL_PAD + ls, :]
    xpad[:, CONV_PAD - (CONV_WIDTH - 1):CONV_PAD, :] = conv_tail
    upad[:, POOL_PAD - POOL_HIST:POOL_PAD, :] = pool_tail
    cn_ref[...] = conv_tail
    pn_ref[...] = pool_tail
    sn_ref[...] = s_scr[...]


def _chunk_sum_matrix(tile, chunk):
    j = np.arange(tile)[:, None]
    i = np.arange(tile)[None, :]
    same = (j // chunk) == (i // chunk)
    return jnp.asarray(np.concatenate([same & (j <= i), same], axis=1), dtype=BF16)


def _head_select_matrix():
    sel = np.zeros((2 * SUBLANES, 2 * DN_WIDTH), np.float32)
    for hh in range(DN_HEADS):
        sel[hh, hh * HEAD_DIM:(hh + 1) * HEAD_DIM] = 1.0
        sel[SUBLANES + hh, DN_WIDTH + hh * HEAD_DIM:DN_WIDTH + (hh + 1) * HEAD_DIM] = 1.0
    return jnp.asarray(sel, dtype=BF16)


def _mixer(x, params, s0, c0, p0, *, layer, nseq, ls, chunk, pos0):
    nw, w_in, w_ab_t, conv_w, alog, dtb, o_norm, w_pool, pool_scale, w_out = params
    b, l, d = x.shape
    assert ls & (ls - 1) == 0 and chunk & (chunk - 1) == 0 and ls % chunk == 0
    assert b % nseq == 0 and l % ls == 0 and (nseq == 1 or l == ls)
    sel = _head_select_matrix()
    cum = _chunk_sum_matrix(nseq * ls, chunk)
    per_seq = lambda shape: pl.BlockSpec((nseq,) + shape, lambda i, j: (i,) + (0,) * len(shape))
    stacked = lambda a: _resident(a.shape[1:], layer)
    out_shapes = (jax.ShapeDtypeStruct(x.shape, x.dtype), jax.ShapeDtypeStruct(s0.shape, s0.dtype),
                  jax.ShapeDtypeStruct(c0.shape, c0.dtype), jax.ShapeDtypeStruct(p0.shape, p0.dtype))
    return pl.pallas_call(
        functools.partial(_mixer_kernel, nseq=nseq, ls=ls, chunk=chunk, pos0=pos0),
        grid=(b // nseq, l // ls),
        in_specs=[pl.BlockSpec((nseq, ls, d), lambda i, j: (i, j, 0)),
                  stacked(nw), stacked(w_in), stacked(w_ab_t), _resident(sel.shape), _resident(cum.shape),
                  stacked(conv_w), stacked(alog), stacked(dtb), stacked(o_norm), stacked(w_pool),
                  stacked(pool_scale), stacked(w_out),
                  per_seq(s0.shape[1:]), per_seq(c0.shape[1:]), per_seq(p0.shape[1:])],
        out_specs=(pl.BlockSpec((nseq, ls, d), lambda i, j: (i, j, 0)),
                   per_seq(s0.shape[1:]), per_seq(c0.shape[1:]), per_seq(p0.shape[1:])),
        out_shape=out_shapes,
        scratch_shapes=[pltpu.VMEM((nseq, DN_HEADS, HEAD_DIM, HEAD_DIM), F32),
                        pltpu.VMEM((nseq, CONV_PAD + ls, CONV_CH), F32),
                        pltpu.VMEM((nseq, POOL_PAD + ls, POOL_WIDTH), F32)],
        compiler_params=pltpu.CompilerParams(dimension_semantics=("arbitrary", "arbitrary"),
                                             vmem_limit_bytes=VMEM_LIMIT_BYTES),
        name="mixer",
    )(x, nw, w_in, w_ab_t, sel, cum, conv_w, alog, dtb, o_norm, w_pool, pool_scale, w_out, s0, c0, p0)


def _head_col(v):
    return jnp.pad(v.astype(F32), ((0, 0), (0, SUBLANES - DN_HEADS)))[:, :, None]


def kernel(x_prompt, x_sample, state_delta, state_conv, state_pool, norm_ffn1, w_ffn1_gate, w_ffn1_up, w_ffn1_down, norm_mix, w_in, conv_w, a_log, dt_bias, o_norm, w_pool, pool_scale, w_out, norm_ffn2, w_ffn2_gate, w_ffn2_up, w_ffn2_down, norm_final):
    depth = w_in.shape[0]
    batch, seq, d_model = x_prompt.shape
    dec_batch, dec_seq, _ = x_sample.shape

    c_ab = CONV_CH + DN_WIDTH
    c_u = c_ab + 2 * DN_HEADS
    w_in_main = jnp.concatenate([w_in[..., :c_ab], w_in[..., c_u:]], axis=-1).astype(BF16)
    w_a_t = jnp.swapaxes(w_in[..., c_ab:c_ab + DN_HEADS], 1, 2)
    w_b_t = jnp.swapaxes(w_in[..., c_ab + DN_HEADS:c_u], 1, 2)
    head_pad = jnp.zeros((depth, SUBLANES - DN_HEADS, d_model), w_in.dtype)
    w_ab_t = jnp.concatenate([w_a_t, head_pad, w_b_t, head_pad], axis=1).astype(BF16)
    row = lambda v: v[:, None, :]
    ffn1 = (row(norm_ffn1), w_ffn1_gate.astype(BF16), w_ffn1_up.astype(BF16), w_ffn1_down.astype(BF16))
    ffn2 = (row(norm_ffn2), w_ffn2_gate.astype(BF16), w_ffn2_up.astype(BF16), w_ffn2_down.astype(BF16))
    mix = (row(norm_mix), w_in_main, w_ab_t, conv_w, _head_col(a_log), _head_col(dt_bias), row(o_norm),
           w_pool.astype(BF16), row(pool_scale), w_out.astype(BF16))
    nf = norm_final[None, :]

    zeros_like_state = lambda s: jnp.zeros((batch,) + s.shape[2:], s.dtype)
    groups = [
        dict(x=x_prompt, tm=512, nseq=1, ls=256, chunk=PROMPT_CHUNK, pos0=0,
             state=[(zeros_like_state(state_delta), zeros_like_state(state_conv), zeros_like_state(state_pool))] * depth),
        dict(x=x_sample, tm=dec_batch * dec_seq, nseq=dec_batch, ls=dec_seq, chunk=dec_seq, pos0=PAST_LEN,
             state=[(state_delta[l], state_conv[l], state_pool[l]) for l in range(depth)]),
    ]
    results = []
    for grp in groups:
        x = grp["x"]
        b, l, _ = x.shape
        new_s, new_c, new_p = [], [], []
        for layer in range(depth):
            x2 = _ffn(x.reshape(b * l, d_model), *ffn1, layer=layer, tm=grp["tm"])
            s0, c0, p0 = grp["state"][layer]
            x, s_n, c_n, p_n = _mixer(x2.reshape(b, l, d_model), mix, s0, c0, p0, layer=layer,
                                      nseq=grp["nseq"], ls=grp["ls"], chunk=grp["chunk"], pos0=grp["pos0"])
            fw = nf if layer == depth - 1 else None
            x = _ffn(x.reshape(b * l, d_model), *ffn2, fw, layer=layer, tm=grp["tm"]).reshape(b, l, d_model)
            new_s.append(s_n); new_c.append(c_n); new_p.append(p_n)
        results.append((x, jnp.stack(new_s), jnp.stack(new_c), jnp.stack(new_p)))
    (y_p, s_p, c_p, q_p), (y_s, s_s, c_s, q_s) = results
    return (y_p, y_s, s_p, c_p, q_p, s_s, c_s, q_s)
```

```python
import functools

import numpy as np
import jax
import jax.numpy as jnp
from jax import lax
from jax.experimental import pallas as pl
from jax.experimental.pallas import tpu as pltpu

EPS = 1e-6
PAST_LEN = 4096
DN_HEADS = 4
HEAD_DIM = 128
DN_WIDTH = DN_HEADS * HEAD_DIM
CONV_WIDTH = 4
CONV_CH = 3 * DN_WIDTH
POOL_WINDOWS = (2, 4, 8, 16)
POOL_GROUP_DIM = 128
POOL_WIDTH = len(POOL_WINDOWS) * POOL_GROUP_DIM
POOL_HIST = max(POOL_WINDOWS) - 1
PROMPT_CHUNK = 64
LANES = 128
SUBLANES = 8
INV_BASE = 16
COL_GATE = CONV_CH
COL_POOL = COL_GATE + DN_WIDTH
N_IN_MAIN = COL_POOL + POOL_WIDTH
CONV_PAD = SUBLANES
POOL_PAD = 2 * SUBLANES
F_QN, F_KN, F_QDEC, F_KDEC, F_W, F_U, F_GC = (SUBLANES * i for i in range(7))
VMEM_LIMIT_BYTES = 56 * 1024 * 1024

F32 = jnp.float32
BF16 = jnp.bfloat16
NT_DIMS = (((1,), (1,)), ((), ()))
TN_DIMS = (((0,), (0,)), ((), ()))


def _rms(x, w):
    return x * lax.rsqrt(jnp.mean(x * x, axis=-1, keepdims=True) + EPS) * w


def _mm(a, b):
    return jnp.dot(a.astype(BF16), b.astype(BF16), preferred_element_type=F32)


def _resident(shape, layer=None):
    zeros = (0,) * len(shape)
    if layer is None:
        return pl.BlockSpec(shape, lambda *_: zeros, pipeline_mode=pl.Buffered(1))
    return pl.BlockSpec((None,) + shape, lambda *_: (layer,) + zeros, pipeline_mode=pl.Buffered(1))


def _ffn_kernel(*refs, f_chunk, final_norm):
    if final_norm:
        x_ref, nw_ref, wg_ref, wu_ref, wd_ref, fw_ref, o_ref, h_ref = refs
    else:
        x_ref, nw_ref, wg_ref, wu_ref, wd_ref, o_ref, h_ref = refs
    x = x_ref[...]
    xn = _rms(x, nw_ref[...]).astype(BF16)
    d_ff = wg_ref.shape[1]
    for c in range(d_ff // f_chunk):
        sl = slice(c * f_chunk, (c + 1) * f_chunk)
        g = jnp.dot(xn, wg_ref[:, sl], preferred_element_type=F32)
        u = jnp.dot(xn, wu_ref[:, sl], preferred_element_type=F32)
        h_ref[:, sl] = (g * jax.nn.sigmoid(g) * u).astype(BF16)
    y = x + 0.5 * jnp.dot(h_ref[...], wd_ref[...], preferred_element_type=F32)
    if final_norm:
        y = _rms(y, fw_ref[...])
    o_ref[...] = y


def _ffn(x, nw, wg, wu, wd, fw=None, *, layer, tm):
    m, d = x.shape
    d_ff = wg.shape[-1]
    final_norm = fw is not None
    args = [x, nw, wg, wu, wd] + ([fw] if final_norm else [])
    in_specs = [pl.BlockSpec((tm, d), lambda i: (i, 0)), _resident((1, d), layer), _resident((d, d_ff), layer),
                _resident((d, d_ff), layer), _resident((d_ff, d), layer)]
    in_specs += [_resident((1, d))] if final_norm else []
    return pl.pallas_call(
        functools.partial(_ffn_kernel, f_chunk=256, final_norm=final_norm),
        grid=(m // tm,),
        in_specs=in_specs,
        out_specs=pl.BlockSpec((tm, d), lambda i: (i, 0)),
        out_shape=jax.ShapeDtypeStruct((m, d), x.dtype),
        scratch_shapes=[pltpu.VMEM((tm, d_ff), BF16)],
        compiler_params=pltpu.CompilerParams(dimension_semantics=("arbitrary",),
                                             vmem_limit_bytes=VMEM_LIMIT_BYTES),
        name="ffn",
    )(*args)


def _unit_lower_inverses(ns, chunk):
    r = lax.broadcasted_iota(jnp.int32, (chunk, chunk), 0)
    c = lax.broadcasted_iota(jnp.int32, (chunk, chunk), 1)
    base = min(chunk, INV_BASE)
    shift = base.bit_length() - 1
    if chunk > base:
        diag = (r >> shift) == (c >> shift)
        nds = [jnp.where(diag, n, 0.0) for n in ns]
    else:
        nds = ns
    eye = jnp.where(r == c, 1.0, 0.0)
    invs = [eye - nd for nd in nds]
    pws = [nd.astype(BF16) for nd in nds]
    pws = [jnp.dot(pw, pw, preferred_element_type=F32).astype(BF16) for pw in pws]
    order = 2
    while 2 * order < base:
        both = [jnp.dot(jnp.concatenate([inv.astype(BF16), pw], axis=0), pw, preferred_element_type=F32)
                for inv, pw in zip(invs, pws)]
        invs = [inv + bo[:chunk] for inv, bo in zip(invs, both)]
        pws = [bo[chunk:].astype(BF16) for bo in both]
        order *= 2
    invs = [inv + _mm(inv, pw) for inv, pw in zip(invs, pws)]
    blk = base
    while blk < chunk:
        s = blk.bit_length() - 1
        lower = ((r >> (s + 1)) == (c >> (s + 1))) & ((r >> s) > (c >> s))
        tmp = [_mm(inv, jnp.where(lower, n, 0.0)) for inv, n in zip(invs, ns)]
        invs = [inv - _mm(t, inv) for inv, t in zip(invs, tmp)]
        blk *= 2
    return invs


def _mixer_stage1(slot, step_tile, x_ref, nw_ref, win_ref, wab_ref, sel_ref, cum_ref, cw_ref, alog_ref, dtb_ref,
                  wp_ref, ps_ref, cn_ref, pn_ref, xpad, upad, st, *, nseq, ls, chunk, pos0):
    tile = nseq * ls
    x = x_ref[...].reshape(tile, x_ref.shape[-1])
    h = _rms(x, nw_ref[...]).astype(BF16)
    p = jnp.dot(h, win_ref[...], preferred_element_type=F32)
    pre = p[:, :CONV_CH]
    gate = p[:, COL_GATE:COL_POOL]
    u = p[:, COL_POOL:]
    xpad[:, CONV_PAD:CONV_PAD + ls, :] = pre.reshape(nseq, ls, CONV_CH)
    upad[:, POOL_PAD:POOL_PAD + ls, :] = u.reshape(nseq, ls, POOL_WIDTH)
    st["gact"][slot] = gate * jax.nn.sigmoid(gate)

    cw = cw_ref[...]
    tap = lambda j: xpad[:, CONV_PAD - 3 + j:CONV_PAD - 3 + j + ls, :].reshape(tile, CONV_CH)
    conv = tap(0) * cw[0:1]
    conv = conv + tap(1) * cw[1:2]
    conv = conv + tap(2) * cw[2:3]
    conv = conv + pre * cw[3:4]
    qkv = conv * jax.nn.sigmoid(conv)

    ab_t = lax.dot_general(wab_ref[...], h, NT_DIMS, preferred_element_type=F32)
    qk_part = qkv[:, :2 * DN_WIDTH]
    ss_t = lax.dot_general(sel_ref[...], (qk_part * qk_part).astype(BF16), NT_DIMS,
                           preferred_element_type=F32)
    a_pre = ab_t[:SUBLANES] + dtb_ref[...]
    softplus = jnp.maximum(a_pre, 0.0) + jnp.log1p(jnp.exp(-jnp.abs(a_pre)))
    g = -jnp.exp(alog_ref[...]) * softplus
    beta = jax.nn.sigmoid(ab_t[SUBLANES:])
    r_q = lax.rsqrt(ss_t[:SUBLANES] + EPS) * (HEAD_DIM ** -0.5)
    r_k = lax.rsqrt(ss_t[SUBLANES:] + EPS)
    g_hi = g.astype(BF16).astype(F32)
    g_mid = (g - g_hi).astype(BF16).astype(F32)
    g_lo = g - g_hi - g_mid
    pieces = jnp.concatenate([g_hi, g_mid, g_lo, jnp.zeros_like(g)], axis=0).astype(BF16)
    sums = jnp.dot(pieces, cum_ref[...], preferred_element_type=F32)
    part = lambda k, half: sums[k * SUBLANES:(k + 1) * SUBLANES, half * tile:(half + 1) * tile]
    gc = part(0, 0) + part(1, 0) + part(2, 0)
    gs = part(0, 1) + part(1, 1) + part(2, 1)
    e_gc = jnp.exp(gc)
    st["gce"][slot] = jnp.concatenate([gc, jnp.exp(gs)], axis=0)
    factors = [r_q, r_k, r_q * e_gc, r_k * jnp.exp(gs - gc), r_k * (beta * e_gc), beta, gc]
    f_t = jnp.concatenate(factors + [jnp.zeros((LANES - SUBLANES * len(factors), tile), F32)], axis=0).T
    st["ft"][slot] = f_t

    col = lambda base, hh: f_t[:, base + hh:base + hh + 1]
    for hh in range(DN_HEADS):
        lanes = slice(hh * HEAD_DIM, (hh + 1) * HEAD_DIM)
        qh = qkv[:, hh * HEAD_DIM:(hh + 1) * HEAD_DIM]
        kh = qkv[:, DN_WIDTH + hh * HEAD_DIM:DN_WIDTH + (hh + 1) * HEAD_DIM]
        vh = qkv[:, 2 * DN_WIDTH + hh * HEAD_DIM:2 * DN_WIDTH + (hh + 1) * HEAD_DIM]
        st["qdec"][slot, :, lanes] = qh * col(F_QDEC, hh)
        st["kdec"][slot, :, lanes] = (kh * col(F_KDEC, hh)).astype(BF16)
        st["rhs"][slot, :, 2 * hh * HEAD_DIM:(2 * hh + 1) * HEAD_DIM] = (vh * col(F_U, hh)).astype(BF16)
        st["rhs"][slot, :, (2 * hh + 1) * HEAD_DIM:(2 * hh + 2) * HEAD_DIM] = (kh * col(F_W, hh)).astype(BF16)
        st["qb"][slot, :, lanes] = (qh * col(F_QN, hh)).astype(BF16)
        st["kb"][slot, :, lanes] = (kh * col(F_KN, hh)).astype(BF16)

    row = lax.broadcasted_iota(jnp.int32, (tile, 1), 0)
    pos = pos0 + step_tile * ls + (row & (ls - 1))
    for gi, win in enumerate(POOL_WINDOWS):
        lanes = slice(gi * POOL_GROUP_DIM, (gi + 1) * POOL_GROUP_DIM)
        acc = u[:, lanes]
        for back in range(1, win):
            acc = acc + upad[:, POOL_PAD - back:POOL_PAD - back + ls, lanes].reshape(tile, POOL_GROUP_DIM)
        cnt = jnp.minimum(win, pos + 1).astype(F32)
        st["z"][slot, :, lanes] = _mm(acc / cnt - u[:, lanes], wp_ref[gi]) * ps_ref[:, lanes]

    conv_tail = xpad[:, CONV_PAD + ls - (CONV_WIDTH - 1):CONV_PAD + ls, :]
    pool_tail = upad[:, POOL_PAD + ls - POOL_HIST:POOL_PAD + ls, :]
    xpad[:, CONV_PAD - (CONV_WIDTH - 1):CONV_PAD, :] = conv_tail
    upad[:, POOL_PAD - POOL_HIST:POOL_PAD, :] = pool_tail
    cn_ref[...] = conv_tail
    pn_ref[...] = pool_tail


def _mixer_stage2(slot, x_ref, on_ref, wout_ref, y_ref, sn_ref, s_scr, st, *, nseq, ls, chunk):
    tile = nseq * ls
    cps = ls // chunk
    cr = lax.broadcasted_iota(jnp.int32, (chunk, chunk), 0)
    cc = lax.broadcasted_iota(jnp.int32, (chunk, chunk), 1)
    incl = cr >= cc
    strict = cr > cc

    probs = [(hh, s, ci) for hh in range(DN_HEADS) for s in range(nseq) for ci in range(cps)]
    rows = lambda s, ci: slice(s * ls + ci * chunk, s * ls + (ci + 1) * chunk)
    head = lambda hh: slice(hh * HEAD_DIM, (hh + 1) * HEAD_DIM)
    f_col = lambda base, pr: st["ft"][slot, rows(pr[1], pr[2]), base + pr[0]:base + pr[0] + 1]
    gc_row = lambda pr: st["gce"][slot, pr[0]:pr[0] + 1, rows(pr[1], pr[2])]
    kb = [st["kb"][slot, rows(s, ci), head(hh)] for hh, s, ci in probs]
    qb = [st["qb"][slot, rows(s, ci), head(hh)] for hh, s, ci in probs]
    decay = [jnp.exp(jnp.where(incl, f_col(F_GC, pr) - gc_row(pr), -1e30)) for pr in probs]
    kk = [lax.dot_general(kb_p, kb_p, NT_DIMS, preferred_element_type=F32) for kb_p in kb]
    qk = [lax.dot_general(qb_p, kb_p, NT_DIMS, preferred_element_type=F32) for qb_p, kb_p in zip(qb, kb)]
    a_inv = _unit_lower_inverses(
        [jnp.where(strict, f_col(F_U, pr) * kk_p * dec_p, 0.0) for pr, kk_p, dec_p in zip(probs, kk, decay)], chunk)
    sol = [_mm(inv_p, st["rhs"][slot, rows(s, ci), 2 * hh * HEAD_DIM:(2 * hh + 2) * HEAD_DIM]).astype(BF16)
           for (hh, s, ci), inv_p in zip(probs, a_inv)]
    qw = [_mm(qk_p * dec_p, sol_p) for qk_p, dec_p, sol_p in zip(qk, decay, sol)]
    kw = [lax.dot_general(st["kdec"][slot, rows(s, ci), head(hh)], sol_p, TN_DIMS, preferred_element_type=F32)
          for (hh, s, ci), sol_p in zip(probs, sol)]

    lhs = {(hh, s, ci): jnp.concatenate([st["qdec"][slot, rows(s, ci), head(hh)] - qw_p[:, HEAD_DIM:],
                                         kw_p[:, HEAD_DIM:]], axis=0).astype(BF16)
           for (hh, s, ci), qw_p, kw_p in zip(probs, qw, kw)}
    o_add = {pr: qw_p[:, :HEAD_DIM] for pr, qw_p in zip(probs, qw)}
    s_add = {pr: kw_p[:, :HEAD_DIM] for pr, kw_p in zip(probs, kw)}
    states = {(hh, s): s_scr[s, hh] for hh in range(DN_HEADS) for s in range(nseq)}
    o_chunk = {}
    for ci in range(cps):
        for hh in range(DN_HEADS):
            for s in range(nseq):
                pr = (hh, s, ci)
                both = jnp.dot(lhs[pr], states[hh, s].astype(BF16), preferred_element_type=F32)
                o_chunk[pr] = o_add[pr] + both[:chunk]
                first = s * ls + ci * chunk
                e_chunk = st["gce"][slot, SUBLANES + hh:SUBLANES + hh + 1, first:first + 1]
                states[hh, s] = states[hh, s] * e_chunk - both[chunk:] + s_add[pr]
    o_heads = []
    for hh in range(DN_HEADS):
        for s in range(nseq):
            s_scr[s, hh] = states[hh, s]
        o_parts = [o_chunk[hh, s, ci] for s in range(nseq) for ci in range(cps)]
        o_h = jnp.concatenate(o_parts, axis=0) if len(o_parts) > 1 else o_parts[0]
        o_heads.append(_rms(o_h, on_ref[...]) * st["gact"][slot, :, head(hh)])

    mixed = jnp.concatenate(o_heads + [st["z"][slot]], axis=1).astype(BF16)
    x = x_ref[...].reshape(tile, x_ref.shape[-1])
    y = x + jnp.dot(mixed, wout_ref[...], preferred_element_type=F32)
    y_ref[...] = y.reshape(y_ref.shape)
    sn_ref[...] = s_scr[...]


STAGE_BUFFERS = (("qb", DN_WIDTH, BF16), ("kb", DN_WIDTH, BF16), ("kdec", DN_WIDTH, BF16), ("rhs", 2 * DN_WIDTH, BF16),
                 ("qdec", DN_WIDTH, F32), ("gact", DN_WIDTH, F32), ("z", POOL_WIDTH, F32), ("ft", LANES, F32))


def _mixer_kernel(x1_ref, x2_ref, nw_ref, win_ref, wab_ref, sel_ref, cum_ref, cw_ref, alog_ref, dtb_ref, on_ref,
                  wp_ref, ps_ref, wout_ref, s0_ref, c0_ref, p0_ref,
                  y_ref, sn_ref, cn_ref, pn_ref,
                  s_scr, xpad, upad, *stage_refs, nseq, ls, chunk, pos0, pipelined):
    step = pl.program_id(1)
    last_tile = pl.num_programs(1) - (2 if pipelined else 1)
    names = [name for name, _, _ in STAGE_BUFFERS] + ["gce"]
    sets = [dict(zip(names, stage_refs[k:k + len(names)])) for k in range(0, len(stage_refs), len(names))]
    stage1 = functools.partial(
        _mixer_stage1, 0, x_ref=x1_ref, nw_ref=nw_ref, win_ref=win_ref, wab_ref=wab_ref, sel_ref=sel_ref,
        cum_ref=cum_ref, cw_ref=cw_ref, alog_ref=alog_ref, dtb_ref=dtb_ref, wp_ref=wp_ref, ps_ref=ps_ref,
        cn_ref=cn_ref, pn_ref=pn_ref, xpad=xpad, upad=upad, nseq=nseq, ls=ls, chunk=chunk, pos0=pos0)
    stage2 = functools.partial(_mixer_stage2, 0, x_ref=x2_ref, on_ref=on_ref, wout_ref=wout_ref, y_ref=y_ref,
                               sn_ref=sn_ref, s_scr=s_scr, nseq=nseq, ls=ls, chunk=chunk)

    @pl.when(step == 0)
    def _():
        s_scr[...] = s0_ref[...]
        xpad[:, CONV_PAD - (CONV_WIDTH - 1):CONV_PAD, :] = c0_ref[...]
        upad[:, POOL_PAD - POOL_HIST:POOL_PAD, :] = p0_ref[...]
        if pipelined:
            for ref in sets[1].values():
                ref[...] = jnp.zeros(ref.shape, ref.dtype)

    if pipelined:
        @pl.when(step == 1)
        def _():
            s_scr[...] = s0_ref[...]

        for parity in range(2):
            @pl.when((step & 1) == parity)
            def _():
                stage1(jnp.minimum(step, last_tile), st=sets[parity])
                stage2(st=sets[1 - parity])
    else:
        stage1(step, st=sets[0])
        stage2(st=sets[0])


def _chunk_sum_matrix(tile, chunk):
    j = np.arange(tile)[:, None]
    i = np.arange(tile)[None, :]
    same = (j // chunk) == (i // chunk)
    return jnp.asarray(np.concatenate([same & (j <= i), same], axis=1), dtype=BF16)


def _head_select_matrix():
    sel = np.zeros((2 * SUBLANES, 2 * DN_WIDTH), np.float32)
    for hh in range(DN_HEADS):
        sel[hh, hh * HEAD_DIM:(hh + 1) * HEAD_DIM] = 1.0
        sel[SUBLANES + hh, DN_WIDTH + hh * HEAD_DIM:DN_WIDTH + (hh + 1) * HEAD_DIM] = 1.0
    return jnp.asarray(sel, dtype=BF16)


def _mixer(x, params, s0, c0, p0, *, layer, nseq, ls, chunk, pos0):
    nw, w_in, w_ab_t, conv_w, alog, dtb, o_norm, w_pool, pool_scale, w_out = params
    b, l, d = x.shape
    assert ls & (ls - 1) == 0 and chunk & (chunk - 1) == 0 and ls % chunk == 0
    assert b % nseq == 0 and l % ls == 0 and (nseq == 1 or l == ls)
    tile = nseq * ls
    n_tiles = l // ls
    pipelined = n_tiles > 1
    n_sets = 2 if pipelined else 1
    sel = _head_select_matrix()
    cum = _chunk_sum_matrix(tile, chunk)
    per_seq = lambda shape: pl.BlockSpec((nseq,) + shape, lambda i, j: (i,) + (0,) * len(shape))
    stacked = lambda a: _resident(a.shape[1:], layer)
    if pipelined:
        tile_stage1 = lambda i, j: (i, jnp.minimum(j, n_tiles - 1), 0)
        tile_stage2 = lambda i, j: (i, jnp.maximum(j - 1, 0), 0)
    else:
        tile_stage1 = tile_stage2 = lambda i, j: (i, j, 0)
    out_shapes = (jax.ShapeDtypeStruct(x.shape, x.dtype), jax.ShapeDtypeStruct(s0.shape, s0.dtype),
                  jax.ShapeDtypeStruct(c0.shape, c0.dtype), jax.ShapeDtypeStruct(p0.shape, p0.dtype))
    stage_set = [pltpu.VMEM((1, tile, width), dtype) for _, width, dtype in STAGE_BUFFERS]
    stage_set.append(pltpu.VMEM((1, 2 * SUBLANES, tile), F32))
    stage_scratch = stage_set * n_sets
    return pl.pallas_call(
        functools.partial(_mixer_kernel, nseq=nseq, ls=ls, chunk=chunk, pos0=pos0, pipelined=pipelined),
        grid=(b // nseq, n_tiles + (1 if pipelined else 0)),
        in_specs=[pl.BlockSpec((nseq, ls, d), tile_stage1), pl.BlockSpec((nseq, ls, d), tile_stage2),
                  stacked(nw), stacked(w_in), stacked(w_ab_t), _resident(sel.shape), _resident(cum.shape),
                  stacked(conv_w), stacked(alog), stacked(dtb), stacked(o_norm), stacked(w_pool),
                  stacked(pool_scale), stacked(w_out),
                  per_seq(s0.shape[1:]), per_seq(c0.shape[1:]), per_seq(p0.shape[1:])],
        out_specs=(pl.BlockSpec((nseq, ls, d), tile_stage2),
                   per_seq(s0.shape[1:]), per_seq(c0.shape[1:]), per_seq(p0.shape[1:])),
        out_shape=out_shapes,
        scratch_shapes=[pltpu.VMEM((nseq, DN_HEADS, HEAD_DIM, HEAD_DIM), F32),
                        pltpu.VMEM((nseq, CONV_PAD + ls, CONV_CH), F32),
                        pltpu.VMEM((nseq, POOL_PAD + ls, POOL_WIDTH), F32)] + stage_scratch,
        compiler_params=pltpu.CompilerParams(dimension_semantics=("arbitrary", "arbitrary"),
                                             vmem_limit_bytes=VMEM_LIMIT_BYTES),
        name="mixer",
    )(x, x, nw, w_in, w_ab_t, sel, cum, conv_w, alog, dtb, o_norm, w_pool, pool_scale, w_out, s0, c0, p0)


def _head_col(v):
    return jnp.pad(v.astype(F32), ((0, 0), (0, SUBLANES - DN_HEADS)))[:, :, None]


def kernel(x_prompt, x_sample, state_delta, state_conv, state_pool, norm_ffn1, w_ffn1_gate, w_ffn1_up, w_ffn1_down, norm_mix, w_in, conv_w, a_log, dt_bias, o_norm, w_pool, pool_scale, w_out, norm_ffn2, w_ffn2_gate, w_ffn2_up, w_ffn2_down, norm_final):
    depth = w_in.shape[0]
    batch, seq, d_model = x_prompt.shape
    dec_batch, dec_seq, _ = x_sample.shape

    c_ab = CONV_CH + DN_WIDTH
    c_u = c_ab + 2 * DN_HEADS
    w_in_main = jnp.concatenate([w_in[..., :c_ab], w_in[..., c_u:]], axis=-1).astype(BF16)
    w_a_t = jnp.swapaxes(w_in[..., c_ab:c_ab + DN_HEADS], 1, 2)
    w_b_t = jnp.swapaxes(w_in[..., c_ab + DN_HEADS:c_u], 1, 2)
    head_pad = jnp.zeros((depth, SUBLANES - DN_HEADS, d_model), w_in.dtype)
    w_ab_t = jnp.concatenate([w_a_t, head_pad, w_b_t, head_pad], axis=1).astype(BF16)
    row = lambda v: v[:, None, :]
    ffn1 = (row(norm_ffn1), w_ffn1_gate.astype(BF16), w_ffn1_up.astype(BF16), w_ffn1_down.astype(BF16))
    ffn2 = (row(norm_ffn2), w_ffn2_gate.astype(BF16), w_ffn2_up.astype(BF16), w_ffn2_down.astype(BF16))
    mix = (row(norm_mix), w_in_main, w_ab_t, conv_w, _head_col(a_log), _head_col(dt_bias), row(o_norm),
           w_pool.astype(BF16), row(pool_scale), w_out.astype(BF16))
    nf = norm_final[None, :]

    zeros_like_state = lambda s: jnp.zeros((batch,) + s.shape[2:], s.dtype)
    groups = [
        dict(x=x_prompt, tm=512, nseq=1, ls=256, chunk=PROMPT_CHUNK, pos0=0,
             state=[(zeros_like_state(state_delta), zeros_like_state(state_conv), zeros_like_state(state_pool))] * depth),
        dict(x=x_sample, tm=dec_batch * dec_seq, nseq=dec_batch, ls=dec_seq, chunk=dec_seq, pos0=PAST_LEN,
             state=[(state_delta[l], state_conv[l], state_pool[l]) for l in range(depth)]),
    ]
    results = []
    for grp in groups:
        x = grp["x"]
        b, l, _ = x.shape
        new_s, new_c, new_p = [], [], []
        for layer in range(depth):
            x2 = _ffn(x.reshape(b * l, d_model), *ffn1, layer=layer, tm=grp["tm"])
            s0, c0, p0 = grp["state"][layer]
            x, s_n, c_n, p_n = _mixer(x2.reshape(b, l, d_model), mix, s0, c0, p0, layer=layer,
                                      nseq=grp["nseq"], ls=grp["ls"], chunk=grp["chunk"], pos0=grp["pos0"])
            fw = nf if layer == depth - 1 else None
            x = _ffn(x.reshape(b * l, d_model), *ffn2, fw, layer=layer, tm=grp["tm"]).reshape(b, l, d_model)
            new_s.append(s_n); new_c.append(c_n); new_p.append(p_n)
        results.append((x, jnp.stack(new_s), jnp.stack(new_c), jnp.stack(new_p)))
    (y_p, s_p, c_p, q_p), (y_s, s_s, c_s, q_s) = results
    return (y_p, y_s, s_p, c_p, q_p, s_s, c_s, q_s)
```

```python
import functools

import numpy as np
import jax
import jax.numpy as jnp
from jax import lax
from jax.experimental import pallas as pl
from jax.experimental.pallas import tpu as pltpu

EPS = 1e-6
PAST_LEN = 4096
DN_HEADS = 4
HEAD_DIM = 128
DN_WIDTH = DN_HEADS * HEAD_DIM
CONV_WIDTH = 4
CONV_CH = 3 * DN_WIDTH
POOL_WINDOWS = (2, 4, 8, 16)
POOL_GROUP_DIM = 128
POOL_WIDTH = len(POOL_WINDOWS) * POOL_GROUP_DIM
POOL_HIST = max(POOL_WINDOWS) - 1
PROMPT_CHUNK = 64
LANES = 128
SUBLANES = 8
INV_BASE = 16
ROW_STRIDE = 4
COL_GATE = CONV_CH
COL_POOL = COL_GATE + DN_WIDTH
N_IN_MAIN = COL_POOL + POOL_WIDTH
CONV_PAD = SUBLANES
POOL_PAD = 2 * SUBLANES
F_QN, F_KN, F_QDEC, F_KDEC, F_W, F_U, F_GC = (SUBLANES * i for i in range(7))
VMEM_LIMIT_BYTES = 56 * 1024 * 1024

F32 = jnp.float32
BF16 = jnp.bfloat16
NT_DIMS = (((1,), (1,)), ((), ()))
TN_DIMS = (((0,), (0,)), ((), ()))


def _rms(x, w):
    return x * lax.rsqrt(jnp.mean(x * x, axis=-1, keepdims=True) + EPS) * w


def _mm(a, b):
    return jnp.dot(a.astype(BF16), b.astype(BF16), preferred_element_type=F32)


def _resident(shape, layer=None):
    zeros = (0,) * len(shape)
    if layer is None:
        return pl.BlockSpec(shape, lambda *_: zeros, pipeline_mode=pl.Buffered(1))
    return pl.BlockSpec((None,) + shape, lambda *_: (layer,) + zeros, pipeline_mode=pl.Buffered(1))


def _ffn_kernel(*refs, f_chunk, final_norm):
    if final_norm:
        x_ref, nw_ref, wg_ref, wu_ref, wd_ref, fw_ref, o_ref, h_ref = refs
    else:
        x_ref, nw_ref, wg_ref, wu_ref, wd_ref, o_ref, h_ref = refs
    x = x_ref[...]
    xn = _rms(x, nw_ref[...]).astype(BF16)
    d_ff = wg_ref.shape[1]
    for c in range(d_ff // f_chunk):
        sl = slice(c * f_chunk, (c + 1) * f_chunk)
        g = jnp.dot(xn, wg_ref[:, sl], preferred_element_type=F32)
        u = jnp.dot(xn, wu_ref[:, sl], preferred_element_type=F32)
        h_ref[:, sl] = (g * jax.nn.sigmoid(g) * u).astype(BF16)
    y = x + 0.5 * jnp.dot(h_ref[...], wd_ref[...], preferred_element_type=F32)
    if final_norm:
        y = _rms(y, fw_ref[...])
    o_ref[...] = y


def _ffn(x, nw, wg, wu, wd, fw=None, *, layer, tm):
    m, d = x.shape
    d_ff = wg.shape[-1]
    final_norm = fw is not None
    args = [x, nw, wg, wu, wd] + ([fw] if final_norm else [])
    in_specs = [pl.BlockSpec((tm, d), lambda i: (i, 0)), _resident((1, d), layer), _resident((d, d_ff), layer),
                _resident((d, d_ff), layer), _resident((d_ff, d), layer)]
    in_specs += [_resident((1, d))] if final_norm else []
    return pl.pallas_call(
        functools.partial(_ffn_kernel, f_chunk=256, final_norm=final_norm),
        grid=(m // tm,),
        in_specs=in_specs,
        out_specs=pl.BlockSpec((tm, d), lambda i: (i, 0)),
        out_shape=jax.ShapeDtypeStruct((m, d), x.dtype),
        scratch_shapes=[pltpu.VMEM((tm, d_ff), BF16)],
        compiler_params=pltpu.CompilerParams(dimension_semantics=("arbitrary",),
                                             vmem_limit_bytes=VMEM_LIMIT_BYTES),
        name="ffn",
    )(*args)


def _unit_lower_inverses(ns, chunk):
    r = lax.broadcasted_iota(jnp.int32, (chunk, chunk), 0)
    c = lax.broadcasted_iota(jnp.int32, (chunk, chunk), 1)
    base = min(chunk, INV_BASE)
    shift = base.bit_length() - 1
    if chunk > base:
        diag = (r >> shift) == (c >> shift)
        nds = [jnp.where(diag, n, 0.0) for n in ns]
    else:
        nds = ns
    eye = jnp.where(r == c, 1.0, 0.0)
    invs = [eye - nd for nd in nds]
    pws = [nd.astype(BF16) for nd in nds]
    pws = [jnp.dot(pw, pw, preferred_element_type=F32).astype(BF16) for pw in pws]
    order = 2
    while 2 * order < base:
        both = [jnp.dot(jnp.concatenate([inv.astype(BF16), pw], axis=0), pw, preferred_element_type=F32)
                for inv, pw in zip(invs, pws)]
        invs = [inv + bo[:chunk] for inv, bo in zip(invs, both)]
        pws = [bo[chunk:].astype(BF16) for bo in both]
        order *= 2
    invs = [inv + _mm(inv, pw) for inv, pw in zip(invs, pws)]
    blk = base
    while blk < chunk:
        s = blk.bit_length() - 1
        lower = ((r >> (s + 1)) == (c >> (s + 1))) & ((r >> s) > (c >> s))
        tmp = [_mm(inv, jnp.where(lower, n, 0.0)) for inv, n in zip(invs, ns)]
        invs = [inv - _mm(t, inv) for inv, t in zip(invs, tmp)]
        blk *= 2
    return invs


def _mixer_stage1(slot, step_tile, x_ref, nw_ref, win_ref, wab_ref, sel_ref, cum_ref, cw_ref, alog_ref, dtb_ref,
                  wp_ref, ps_ref, cn_ref, pn_ref, xpad, upad, qkv_s, st, *, nseq, ls, chunk, pos0):
    tile = nseq * ls
    x = x_ref[...].reshape(tile, x_ref.shape[-1])
    h = _rms(x, nw_ref[...]).astype(BF16)
    p = jnp.dot(h, win_ref[...], preferred_element_type=F32)
    pre = p[:, :CONV_CH]
    gate = p[:, COL_GATE:COL_POOL]
    u = p[:, COL_POOL:]
    st["gact"][slot] = gate * jax.nn.sigmoid(gate)

    for sl in range(CONV_CH // LANES):
        lanes = slice(sl * LANES, (sl + 1) * LANES)
        xpad[:, sl, CONV_PAD:CONV_PAD + ls, :] = pre[:, lanes].reshape(nseq, ls, LANES)
        cw = cw_ref[:, lanes]
        if nseq == 1:
            for r in range(ROW_STRIDE):
                conv = None
                for j in range(CONV_WIDTH):
                    rows_j = pl.ds(CONV_PAD - (CONV_WIDTH - 1) + j + r, ls // ROW_STRIDE, stride=ROW_STRIDE)
                    term = xpad[0, sl, rows_j, :] * cw[j:j + 1]
                    conv = term if conv is None else conv + term
                qkv_s[sl, pl.ds(r, ls // ROW_STRIDE, stride=ROW_STRIDE), :] = conv * jax.nn.sigmoid(conv)
        else:
            conv = None
            for j in range(CONV_WIDTH):
                start = CONV_PAD - (CONV_WIDTH - 1) + j
                term = xpad[:, sl, start:start + ls, :].reshape(tile, LANES) * cw[j:j + 1]
                conv = term if conv is None else conv + term
            qkv_s[sl] = conv * jax.nn.sigmoid(conv)
        tail = xpad[:, sl, CONV_PAD + ls - (CONV_WIDTH - 1):CONV_PAD + ls, :]
        xpad[:, sl, CONV_PAD - (CONV_WIDTH - 1):CONV_PAD, :] = tail
        cn_ref[:, :, lanes] = tail
    upad[:, POOL_PAD:POOL_PAD + ls, :] = u.reshape(nseq, ls, POOL_WIDTH)
    slab = lambda part, hh: qkv_s[part * DN_HEADS + hh]

    ab_t = lax.dot_general(wab_ref[...], h, NT_DIMS, preferred_element_type=F32)
    qk_part = jnp.concatenate([qkv_s[sl] for sl in range(2 * DN_HEADS)], axis=1)
    ss_t = lax.dot_general(sel_ref[...], (qk_part * qk_part).astype(BF16), NT_DIMS,
                           preferred_element_type=F32)
    a_pre = ab_t[:SUBLANES] + dtb_ref[...]
    softplus = jnp.maximum(a_pre, 0.0) + jnp.log1p(jnp.exp(-jnp.abs(a_pre)))
    g = -jnp.exp(alog_ref[...]) * softplus
    beta = jax.nn.sigmoid(ab_t[SUBLANES:])
    r_q = lax.rsqrt(ss_t[:SUBLANES] + EPS) * (HEAD_DIM ** -0.5)
    r_k = lax.rsqrt(ss_t[SUBLANES:] + EPS)
    g_hi = g.astype(BF16).astype(F32)
    g_mid = (g - g_hi).astype(BF16).astype(F32)
    g_lo = g - g_hi - g_mid
    pieces = jnp.concatenate([g_hi, g_mid, g_lo, jnp.zeros_like(g)], axis=0).astype(BF16)
    sums = jnp.dot(pieces, cum_ref[...], preferred_element_type=F32)
    part = lambda k, half: sums[k * SUBLANES:(k + 1) * SUBLANES, half * tile:(half + 1) * tile]
    gc = part(0, 0) + part(1, 0) + part(2, 0)
    gs = part(0, 1) + part(1, 1) + part(2, 1)
    e_gc = jnp.exp(gc)
    st["gce"][slot] = jnp.concatenate([gc, jnp.exp(gs)], axis=0)
    factors = [r_q, r_k, r_q * e_gc, r_k * jnp.exp(gs - gc), r_k * (beta * e_gc), beta, gc]
    f_t = jnp.concatenate(factors + [jnp.zeros((LANES - SUBLANES * len(factors), tile), F32)], axis=0).T
    st["ft"][slot] = f_t

    col = lambda base, hh: f_t[:, base + hh:base + hh + 1]
    for hh in range(DN_HEADS):
        lanes = slice(hh * HEAD_DIM, (hh + 1) * HEAD_DIM)
        qh, kh, vh = slab(0, hh), slab(1, hh), slab(2, hh)
        st["qdec"][slot, :, lanes] = qh * col(F_QDEC, hh)
        st["kdec"][slot, :, lanes] = (kh * col(F_KDEC, hh)).astype(BF16)
        st["rhs"][slot, :, 2 * hh * HEAD_DIM:(2 * hh + 1) * HEAD_DIM] = (vh * col(F_U, hh)).astype(BF16)
        st["rhs"][slot, :, (2 * hh + 1) * HEAD_DIM:(2 * hh + 2) * HEAD_DIM] = (kh * col(F_W, hh)).astype(BF16)
        st["qb"][slot, :, lanes] = (qh * col(F_QN, hh)).astype(BF16)
        st["kb"][slot, :, lanes] = (kh * col(F_KN, hh)).astype(BF16)

    row = lax.broadcasted_iota(jnp.int32, (tile, 1), 0)
    pos = pos0 + step_tile * ls + (row & (ls - 1))
    for gi, win in enumerate(POOL_WINDOWS):
        lanes = slice(gi * POOL_GROUP_DIM, (gi + 1) * POOL_GROUP_DIM)
        acc = u[:, lanes]
        for back in range(1, win):
            acc = acc + upad[:, POOL_PAD - back:POOL_PAD - back + ls, lanes].reshape(tile, POOL_GROUP_DIM)
        cnt = jnp.minimum(win, pos + 1).astype(F32)
        st["z"][slot, :, lanes] = _mm(acc / cnt - u[:, lanes], wp_ref[gi]) * ps_ref[:, lanes]

    pool_tail = upad[:, POOL_PAD + ls - POOL_HIST:POOL_PAD + ls, :]
    upad[:, POOL_PAD - POOL_HIST:POOL_PAD, :] = pool_tail
    pn_ref[...] = pool_tail


def _mixer_stage2(slot, x_ref, on_ref, wout_ref, y_ref, sn_ref, s_scr, st, *, nseq, ls, chunk):
    tile = nseq * ls
    cps = ls // chunk
    cr = lax.broadcasted_iota(jnp.int32, (chunk, chunk), 0)
    cc = lax.broadcasted_iota(jnp.int32, (chunk, chunk), 1)
    incl = cr >= cc
    strict = cr > cc

    probs = [(hh, s, ci) for hh in range(DN_HEADS) for s in range(nseq) for ci in range(cps)]
    rows = lambda s, ci: slice(s * ls + ci * chunk, s * ls + (ci + 1) * chunk)
    head = lambda hh: slice(hh * HEAD_DIM, (hh + 1) * HEAD_DIM)
    f_col = lambda base, pr: st["ft"][slot, rows(pr[1], pr[2]), base + pr[0]:base + pr[0] + 1]
    gc_row = lambda pr: st["gce"][slot, pr[0]:pr[0] + 1, rows(pr[1], pr[2])]
    kb = [st["kb"][slot, rows(s, ci), head(hh)] for hh, s, ci in probs]
    qb = [st["qb"][slot, rows(s, ci), head(hh)] for hh, s, ci in probs]
    decay = [jnp.exp(jnp.where(incl, f_col(F_GC, pr) - gc_row(pr), -1e30)) for pr in probs]
    kk = [lax.dot_general(kb_p, kb_p, NT_DIMS, preferred_element_type=F32) for kb_p in kb]
    qk = [lax.dot_general(qb_p, kb_p, NT_DIMS, preferred_element_type=F32) for qb_p, kb_p in zip(qb, kb)]
    a_inv = _unit_lower_inverses(
        [jnp.where(strict, f_col(F_U, pr) * kk_p * dec_p, 0.0) for pr, kk_p, dec_p in zip(probs, kk, decay)], chunk)
    sol = [_mm(inv_p, st["rhs"][slot, rows(s, ci), 2 * hh * HEAD_DIM:(2 * hh + 2) * HEAD_DIM]).astype(BF16)
           for (hh, s, ci), inv_p in zip(probs, a_inv)]
    qw = [_mm(qk_p * dec_p, sol_p) for qk_p, dec_p, sol_p in zip(qk, decay, sol)]
    kw = [lax.dot_general(st["kdec"][slot, rows(s, ci), head(hh)], sol_p, TN_DIMS, preferred_element_type=F32)
          for (hh, s, ci), sol_p in zip(probs, sol)]

    lhs = {(hh, s, ci): jnp.concatenate([st["qdec"][slot, rows(s, ci), head(hh)] - qw_p[:, HEAD_DIM:],
                                         kw_p[:, HEAD_DIM:]], axis=0).astype(BF16)
           for (hh, s, ci), qw_p, kw_p in zip(probs, qw, kw)}
    o_add = {pr: qw_p[:, :HEAD_DIM] for pr, qw_p in zip(probs, qw)}
    s_add = {pr: kw_p[:, :HEAD_DIM] for pr, kw_p in zip(probs, kw)}
    states = {(hh, s): s_scr[s, hh] for hh in range(DN_HEADS) for s in range(nseq)}
    o_chunk = {}
    for ci in range(cps):
        for hh in range(DN_HEADS):
            for s in range(nseq):
                pr = (hh, s, ci)
                both = jnp.dot(lhs[pr], states[hh, s].astype(BF16), preferred_element_type=F32)
                o_chunk[pr] = o_add[pr] + both[:chunk]
                first = s * ls + ci * chunk
                e_chunk = st["gce"][slot, SUBLANES + hh:SUBLANES + hh + 1, first:first + 1]
                states[hh, s] = states[hh, s] * e_chunk - both[chunk:] + s_add[pr]
    o_heads = []
    for hh in range(DN_HEADS):
        for s in range(nseq):
            s_scr[s, hh] = states[hh, s]
        o_parts = [o_chunk[hh, s, ci] for s in range(nseq) for ci in range(cps)]
        o_h = jnp.concatenate(o_parts, axis=0) if len(o_parts) > 1 else o_parts[0]
        o_heads.append(_rms(o_h, on_ref[...]) * st["gact"][slot, :, head(hh)])

    mixed = jnp.concatenate(o_heads + [st["z"][slot]], axis=1).astype(BF16)
    x = x_ref[...].reshape(tile, x_ref.shape[-1])
    y = x + jnp.dot(mixed, wout_ref[...], preferred_element_type=F32)
    y_ref[...] = y.reshape(y_ref.shape)
    sn_ref[...] = s_scr[...]


STAGE_BUFFERS = (("qb", DN_WIDTH, BF16), ("kb", DN_WIDTH, BF16), ("kdec", DN_WIDTH, BF16), ("rhs", 2 * DN_WIDTH, BF16),
                 ("qdec", DN_WIDTH, F32), ("gact", DN_WIDTH, F32), ("z", POOL_WIDTH, F32), ("ft", LANES, F32))


def _mixer_kernel(x1_ref, x2_ref, nw_ref, win_ref, wab_ref, sel_ref, cum_ref, cw_ref, alog_ref, dtb_ref, on_ref,
                  wp_ref, ps_ref, wout_ref, s0_ref, c0_ref, p0_ref,
                  y_ref, sn_ref, cn_ref, pn_ref,
                  s_scr, xpad, upad, qkv_s, *stage_refs, nseq, ls, chunk, pos0, pipelined):
    step = pl.program_id(1)
    last_tile = pl.num_programs(1) - (2 if pipelined else 1)
    names = [name for name, _, _ in STAGE_BUFFERS] + ["gce"]
    sets = [dict(zip(names, stage_refs[k:k + len(names)])) for k in range(0, len(stage_refs), len(names))]
    stage1 = functools.partial(
        _mixer_stage1, 0, x_ref=x1_ref, nw_ref=nw_ref, win_ref=win_ref, wab_ref=wab_ref, sel_ref=sel_ref,
        cum_ref=cum_ref, cw_ref=cw_ref, alog_ref=alog_ref, dtb_ref=dtb_ref, wp_ref=wp_ref, ps_ref=ps_ref,
        cn_ref=cn_ref, pn_ref=pn_ref, xpad=xpad, upad=upad, qkv_s=qkv_s, nseq=nseq, ls=ls, chunk=chunk, pos0=pos0)
    stage2 = functools.partial(_mixer_stage2, 0, x_ref=x2_ref, on_ref=on_ref, wout_ref=wout_ref, y_ref=y_ref,
                               sn_ref=sn_ref, s_scr=s_scr, nseq=nseq, ls=ls, chunk=chunk)

    @pl.when(step == 0)
    def _():
        s_scr[...] = s0_ref[...]
        for sl in range(CONV_CH // LANES):
            xpad[:, sl, CONV_PAD - (CONV_WIDTH - 1):CONV_PAD, :] = c0_ref[:, :, sl * LANES:(sl + 1) * LANES]
        upad[:, POOL_PAD - POOL_HIST:POOL_PAD, :] = p0_ref[...]
        if pipelined:
            for ref in sets[1].values():
                ref[...] = jnp.zeros(ref.shape, ref.dtype)

    if pipelined:
        @pl.when(step == 1)
        def _():
            s_scr[...] = s0_ref[...]

        for parity in range(2):
            @pl.when((step & 1) == parity)
            def _():
                stage1(jnp.minimum(step, last_tile), st=sets[parity])
                stage2(st=sets[1 - parity])
    else:
        stage1(step, st=sets[0])
        stage2(st=sets[0])


def _chunk_sum_matrix(tile, chunk):
    j = np.arange(tile)[:, None]
    i = np.arange(tile)[None, :]
    same = (j // chunk) == (i // chunk)
    return jnp.asarray(np.concatenate([same & (j <= i), same], axis=1), dtype=BF16)


def _head_select_matrix():
    sel = np.zeros((2 * SUBLANES, 2 * DN_WIDTH), np.float32)
    for hh in range(DN_HEADS):
        sel[hh, hh * HEAD_DIM:(hh + 1) * HEAD_DIM] = 1.0
        sel[SUBLANES + hh, DN_WIDTH + hh * HEAD_DIM:DN_WIDTH + (hh + 1) * HEAD_DIM] = 1.0
    return jnp.asarray(sel, dtype=BF16)


def _mixer(x, params, s0, c0, p0, *, layer, nseq, ls, chunk, pos0):
    nw, w_in, w_ab_t, conv_w, alog, dtb, o_norm, w_pool, pool_scale, w_out = params
    b, l, d = x.shape
    assert ls & (ls - 1) == 0 and chunk & (chunk - 1) == 0 and ls % chunk == 0
    assert b % nseq == 0 and l % ls == 0 and (nseq == 1 or l == ls)
    tile = nseq * ls
    n_tiles = l // ls
    pipelined = False
    n_sets = 2 if pipelined else 1
    sel = _head_select_matrix()
    cum = _chunk_sum_matrix(tile, chunk)
    per_seq = lambda shape: pl.BlockSpec((nseq,) + shape, lambda i, j: (i,) + (0,) * len(shape))
    stacked = lambda a: _resident(a.shape[1:], layer)
    if pipelined:
        tile_stage1 = lambda i, j: (i, jnp.minimum(j, n_tiles - 1), 0)
        tile_stage2 = lambda i, j: (i, jnp.maximum(j - 1, 0), 0)
    else:
        tile_stage1 = tile_stage2 = lambda i, j: (i, j, 0)
    out_shapes = (jax.ShapeDtypeStruct(x.shape, x.dtype), jax.ShapeDtypeStruct(s0.shape, s0.dtype),
                  jax.ShapeDtypeStruct(c0.shape, c0.dtype), jax.ShapeDtypeStruct(p0.shape, p0.dtype))
    stage_set = [pltpu.VMEM((1, tile, width), dtype) for _, width, dtype in STAGE_BUFFERS]
    stage_set.append(pltpu.VMEM((1, 2 * SUBLANES, tile), F32))
    stage_scratch = stage_set * n_sets
    return pl.pallas_call(
        functools.partial(_mixer_kernel, nseq=nseq, ls=ls, chunk=chunk, pos0=pos0, pipelined=pipelined),
        grid=(b // nseq, n_tiles + (1 if pipelined else 0)),
        in_specs=[pl.BlockSpec((nseq, ls, d), tile_stage1), pl.BlockSpec((nseq, ls, d), tile_stage2),
                  stacked(nw), stacked(w_in), stacked(w_ab_t), _resident(sel.shape), _resident(cum.shape),
                  stacked(conv_w), stacked(alog), stacked(dtb), stacked(o_norm), stacked(w_pool),
                  stacked(pool_scale), stacked(w_out),
                  per_seq(s0.shape[1:]), per_seq(c0.shape[1:]), per_seq(p0.shape[1:])],
        out_specs=(pl.BlockSpec((nseq, ls, d), tile_stage2),
                   per_seq(s0.shape[1:]), per_seq(c0.shape[1:]), per_seq(p0.shape[1:])),
        out_shape=out_shapes,
        scratch_shapes=[pltpu.VMEM((nseq, DN_HEADS, HEAD_DIM, HEAD_DIM), F32),
                        pltpu.VMEM((nseq, CONV_CH // LANES, CONV_PAD + ls, LANES), F32),
                        pltpu.VMEM((nseq, POOL_PAD + ls, POOL_WIDTH), F32),
                        pltpu.VMEM((CONV_CH // LANES, tile, LANES), F32)] + stage_scratch,
        compiler_params=pltpu.CompilerParams(dimension_semantics=("arbitrary", "arbitrary"),
                                             vmem_limit_bytes=VMEM_LIMIT_BYTES),
        name="mixer",
    )(x, x, nw, w_in, w_ab_t, sel, cum, conv_w, alog, dtb, o_norm, w_pool, pool_scale, w_out, s0, c0, p0)


def _head_col(v):
    return jnp.pad(v.astype(F32), ((0, 0), (0, SUBLANES - DN_HEADS)))[:, :, None]


def kernel(x_prompt, x_sample, state_delta, state_conv, state_pool, norm_ffn1, w_ffn1_gate, w_ffn1_up, w_ffn1_down, norm_mix, w_in, conv_w, a_log, dt_bias, o_norm, w_pool, pool_scale, w_out, norm_ffn2, w_ffn2_gate, w_ffn2_up, w_ffn2_down, norm_final):
    depth = w_in.shape[0]
    batch, seq, d_model = x_prompt.shape
    dec_batch, dec_seq, _ = x_sample.shape

    c_ab = CONV_CH + DN_WIDTH
    c_u = c_ab + 2 * DN_HEADS
    w_in_main = jnp.concatenate([w_in[..., :c_ab], w_in[..., c_u:]], axis=-1).astype(BF16)
    w_a_t = jnp.swapaxes(w_in[..., c_ab:c_ab + DN_HEADS], 1, 2)
    w_b_t = jnp.swapaxes(w_in[..., c_ab + DN_HEADS:c_u], 1, 2)
    head_pad = jnp.zeros((depth, SUBLANES - DN_HEADS, d_model), w_in.dtype)
    w_ab_t = jnp.concatenate([w_a_t, head_pad, w_b_t, head_pad], axis=1).astype(BF16)
    row = lambda v: v[:, None, :]
    ffn1 = (row(norm_ffn1), w_ffn1_gate.astype(BF16), w_ffn1_up.astype(BF16), w_ffn1_down.astype(BF16))
    ffn2 = (row(norm_ffn2), w_ffn2_gate.astype(BF16), w_ffn2_up.astype(BF16), w_ffn2_down.astype(BF16))
    mix = (row(norm_mix), w_in_main, w_ab_t, conv_w, _head_col(a_log), _head_col(dt_bias), row(o_norm),
           w_pool.astype(BF16), row(pool_scale), w_out.astype(BF16))
    nf = norm_final[None, :]

    zeros_like_state = lambda s: jnp.zeros((batch,) + s.shape[2:], s.dtype)
    groups = [
        dict(x=x_prompt, tm=512, nseq=1, ls=256, chunk=PROMPT_CHUNK, pos0=0,
             state=[(zeros_like_state(state_delta), zeros_like_state(state_conv), zeros_like_state(state_pool))] * depth),
        dict(x=x_sample, tm=dec_batch * dec_seq, nseq=dec_batch, ls=dec_seq, chunk=dec_seq, pos0=PAST_LEN,
             state=[(state_delta[l], state_conv[l], state_pool[l]) for l in range(depth)]),
    ]
    results = []
    for grp in groups:
        x = grp["x"]
        b, l, _ = x.shape
        new_s, new_c, new_p = [], [], []
        for layer in range(depth):
            x2 = _ffn(x.reshape(b * l, d_model), *ffn1, layer=layer, tm=grp["tm"])
            s0, c0, p0 = grp["state"][layer]
            x, s_n, c_n, p_n = _mixer(x2.reshape(b, l, d_model), mix, s0, c0, p0, layer=layer,
                                      nseq=grp["nseq"], ls=grp["ls"], chunk=grp["chunk"], pos0=grp["pos0"])
            fw = nf if layer == depth - 1 else None
            x = _ffn(x.reshape(b * l, d_model), *ffn2, fw, layer=layer, tm=grp["tm"]).reshape(b, l, d_model)
            new_s.append(s_n); new_c.append(c_n); new_p.append(p_n)
        results.append((x, jnp.stack(new_s), jnp.stack(new_c), jnp.stack(new_p)))
    (y_p, s_p, c_p, q_p), (y_s, s_s, c_s, q_s) = results
    return (y_p, y_s, s_p, c_p, q_p, s_s, c_s, q_s)
```

```python
import functools

import numpy as np
import jax
import jax.numpy as jnp
from jax import lax
from jax.experimental import pallas as pl
from jax.experimental.pallas import tpu as pltpu

EPS = 1e-6
PAST_LEN = 4096
DN_HEADS = 4
HEAD_DIM = 128
DN_WIDTH = DN_HEADS * HEAD_DIM
CONV_WIDTH = 4
CONV_CH = 3 * DN_WIDTH
POOL_WINDOWS = (2, 4, 8, 16)
POOL_GROUP_DIM = 128
POOL_WIDTH = len(POOL_WINDOWS) * POOL_GROUP_DIM
POOL_HIST = max(POOL_WINDOWS) - 1
PROMPT_CHUNK = 64
LANES = 128
SUBLANES = 8
INV_BASE = 16
ROW_STRIDE = 4
COL_GATE = CONV_CH
COL_POOL = COL_GATE + DN_WIDTH
N_IN_MAIN = COL_POOL + POOL_WIDTH
CONV_PAD = SUBLANES
POOL_PAD = 2 * SUBLANES
F_QN, F_KN, F_QDEC, F_KDEC, F_W, F_U, F_GC = (SUBLANES * i for i in range(7))
VMEM_LIMIT_BYTES = 56 * 1024 * 1024

F32 = jnp.float32
BF16 = jnp.bfloat16
NT_DIMS = (((1,), (1,)), ((), ()))
TN_DIMS = (((0,), (0,)), ((), ()))


def _rms(x, w):
    return x * lax.rsqrt(jnp.mean(x * x, axis=-1, keepdims=True) + EPS) * w


def _mm(a, b):
    return jnp.dot(a.astype(BF16), b.astype(BF16), preferred_element_type=F32)


def _resident(shape, layer=None):
    zeros = (0,) * len(shape)
    if layer is None:
        return pl.BlockSpec(shape, lambda *_: zeros, pipeline_mode=pl.Buffered(1))
    return pl.BlockSpec((None,) + shape, lambda *_: (layer,) + zeros, pipeline_mode=pl.Buffered(1))


def _ffn_kernel(*refs, f_chunk, final_norm):
    if final_norm:
        x_ref, nw_ref, wg_ref, wu_ref, wd_ref, fw_ref, o_ref, h_ref = refs
    else:
        x_ref, nw_ref, wg_ref, wu_ref, wd_ref, o_ref, h_ref = refs
    x = x_ref[...]
    xn = _rms(x, nw_ref[...]).astype(BF16)
    d_ff = wg_ref.shape[1]
    for c in range(d_ff // f_chunk):
        sl = slice(c * f_chunk, (c + 1) * f_chunk)
        g = jnp.dot(xn, wg_ref[:, sl], preferred_element_type=F32)
        u = jnp.dot(xn, wu_ref[:, sl], preferred_element_type=F32)
        h_ref[:, sl] = (g * jax.nn.sigmoid(g) * u).astype(BF16)
    y = x + 0.5 * jnp.dot(h_ref[...], wd_ref[...], preferred_element_type=F32)
    if final_norm:
        y = _rms(y, fw_ref[...])
    o_ref[...] = y


def _ffn(x, nw, wg, wu, wd, fw=None, *, layer, tm):
    m, d = x.shape
    d_ff = wg.shape[-1]
    final_norm = fw is not None
    args = [x, nw, wg, wu, wd] + ([fw] if final_norm else [])
    in_specs = [pl.BlockSpec((tm, d), lambda i: (i, 0)), _resident((1, d), layer), _resident((d, d_ff), layer),
                _resident((d, d_ff), layer), _resident((d_ff, d), layer)]
    in_specs += [_resident((1, d))] if final_norm else []
    return pl.pallas_call(
        functools.partial(_ffn_kernel, f_chunk=256, final_norm=final_norm),
        grid=(m // tm,),
        in_specs=in_specs,
        out_specs=pl.BlockSpec((tm, d), lambda i: (i, 0)),
        out_shape=jax.ShapeDtypeStruct((m, d), x.dtype),
        scratch_shapes=[pltpu.VMEM((tm, d_ff), BF16)],
        compiler_params=pltpu.CompilerParams(dimension_semantics=("arbitrary",),
                                             vmem_limit_bytes=VMEM_LIMIT_BYTES),
        name="ffn",
    )(*args)


def _unit_lower_inverses(ns, chunk):
    r = lax.broadcasted_iota(jnp.int32, (chunk, chunk), 0)
    c = lax.broadcasted_iota(jnp.int32, (chunk, chunk), 1)
    base = min(chunk, INV_BASE)
    shift = base.bit_length() - 1
    if chunk > base:
        diag = (r >> shift) == (c >> shift)
        nds = [jnp.where(diag, n, 0.0) for n in ns]
    else:
        nds = ns
    eye = jnp.where(r == c, 1.0, 0.0)
    invs = [eye - nd for nd in nds]
    pws = [nd.astype(BF16) for nd in nds]
    pws = [jnp.dot(pw, pw, preferred_element_type=F32).astype(BF16) for pw in pws]
    order = 2
    while 2 * order < base:
        both = [jnp.dot(jnp.concatenate([inv.astype(BF16), pw], axis=0), pw, preferred_element_type=F32)
                for inv, pw in zip(invs, pws)]
        invs = [inv + bo[:chunk] for inv, bo in zip(invs, both)]
        pws = [bo[chunk:].astype(BF16) for bo in both]
        order *= 2
    invs = [inv + _mm(inv, pw) for inv, pw in zip(invs, pws)]
    blk = base
    while blk < chunk:
        s = blk.bit_length() - 1
        lower = ((r >> (s + 1)) == (c >> (s + 1))) & ((r >> s) > (c >> s))
        tmp = [_mm(inv, jnp.where(lower, n, 0.0)) for inv, n in zip(invs, ns)]
        invs = [inv - _mm(t, inv) for inv, t in zip(invs, tmp)]
        blk *= 2
    return invs


def _mixer_stage1(slot, step_tile, x_ref, nw_ref, win_ref, wab_ref, sel_ref, cum_ref, cw_ref, alog_ref, dtb_ref,
                  wp_ref, ps_ref, cn_ref, pn_ref, xpad, upad, qkv_s, st, *, nseq, ls, chunk, pos0):
    tile = nseq * ls
    x = x_ref[...].reshape(tile, x_ref.shape[-1])
    h = _rms(x, nw_ref[...]).astype(BF16)
    p = jnp.dot(h, win_ref[...], preferred_element_type=F32)
    pre = p[:, :CONV_CH]
    gate = p[:, COL_GATE:COL_POOL]
    u = p[:, COL_POOL:]
    st["gact"][slot] = gate * jax.nn.sigmoid(gate)

    for sl in range(CONV_CH // LANES):
        lanes = slice(sl * LANES, (sl + 1) * LANES)
        xpad[:, sl, CONV_PAD:CONV_PAD + ls, :] = pre[:, lanes].reshape(nseq, ls, LANES)
        cw = cw_ref[:, lanes]
        if nseq == 1:
            for r in range(ROW_STRIDE):
                conv = None
                for j in range(CONV_WIDTH):
                    rows_j = pl.ds(CONV_PAD - (CONV_WIDTH - 1) + j + r, ls // ROW_STRIDE, stride=ROW_STRIDE)
                    term = xpad[0, sl, rows_j, :] * cw[j:j + 1]
                    conv = term if conv is None else conv + term
                qkv_s[sl, pl.ds(r, ls // ROW_STRIDE, stride=ROW_STRIDE), :] = conv * jax.nn.sigmoid(conv)
        else:
            conv = None
            for j in range(CONV_WIDTH):
                start = CONV_PAD - (CONV_WIDTH - 1) + j
                term = xpad[:, sl, start:start + ls, :].reshape(tile, LANES) * cw[j:j + 1]
                conv = term if conv is None else conv + term
            qkv_s[sl] = conv * jax.nn.sigmoid(conv)
        tail = xpad[:, sl, CONV_PAD + ls - (CONV_WIDTH - 1):CONV_PAD + ls, :]
        xpad[:, sl, CONV_PAD - (CONV_WIDTH - 1):CONV_PAD, :] = tail
        cn_ref[:, :, lanes] = tail
    upad[:, POOL_PAD:POOL_PAD + ls, :] = u.reshape(nseq, ls, POOL_WIDTH)
    slab = lambda part, hh: qkv_s[part * DN_HEADS + hh]

    ab_t = lax.dot_general(wab_ref[...], h, NT_DIMS, preferred_element_type=F32)
    qk_part = jnp.concatenate([qkv_s[sl] for sl in range(2 * DN_HEADS)], axis=1)
    ss_t = lax.dot_general(sel_ref[...], (qk_part * qk_part).astype(BF16), NT_DIMS,
                           preferred_element_type=F32)
    a_pre = ab_t[:SUBLANES] + dtb_ref[...]
    softplus = jnp.maximum(a_pre, 0.0) + jnp.log1p(jnp.exp(-jnp.abs(a_pre)))
    g = -jnp.exp(alog_ref[...]) * softplus
    beta = jax.nn.sigmoid(ab_t[SUBLANES:])
    r_q = lax.rsqrt(ss_t[:SUBLANES] + EPS) * (HEAD_DIM ** -0.5)
    r_k = lax.rsqrt(ss_t[SUBLANES:] + EPS)
    g_hi = g.astype(BF16).astype(F32)
    g_mid = (g - g_hi).astype(BF16).astype(F32)
    g_lo = g - g_hi - g_mid
    pieces = jnp.concatenate([g_hi, g_mid, g_lo, jnp.zeros_like(g)], axis=0).astype(BF16)
    sums = jnp.dot(pieces, cum_ref[...], preferred_element_type=F32)
    part = lambda k, half: sums[k * SUBLANES:(k + 1) * SUBLANES, half * tile:(half + 1) * tile]
    gc = part(0, 0) + part(1, 0) + part(2, 0)
    gs = part(0, 1) + part(1, 1) + part(2, 1)
    e_gc = jnp.exp(gc)
    st["gce"][slot] = jnp.concatenate([gc, jnp.exp(gs)], axis=0)
    factors = [r_q, r_k, r_q * e_gc, r_k * jnp.exp(gs - gc), r_k * (beta * e_gc), beta, gc]
    f_t = jnp.concatenate(factors + [jnp.zeros((LANES - SUBLANES * len(factors), tile), F32)], axis=0).T
    st["ft"][slot] = f_t

    col = lambda base, hh: f_t[:, base + hh:base + hh + 1]
    for hh in range(DN_HEADS):
        lanes = slice(hh * HEAD_DIM, (hh + 1) * HEAD_DIM)
        qh, kh, vh = slab(0, hh), slab(1, hh), slab(2, hh)
        st["qdec"][slot, :, lanes] = qh * col(F_QDEC, hh)
        st["kdec"][slot, :, lanes] = (kh * col(F_KDEC, hh)).astype(BF16)
        st["rhs"][slot, :, 2 * hh * HEAD_DIM:(2 * hh + 1) * HEAD_DIM] = (vh * col(F_U, hh)).astype(BF16)
        st["rhs"][slot, :, (2 * hh + 1) * HEAD_DIM:(2 * hh + 2) * HEAD_DIM] = (kh * col(F_W, hh)).astype(BF16)
        st["qb"][slot, :, lanes] = (qh * col(F_QN, hh)).astype(BF16)
        st["kb"][slot, :, lanes] = (kh * col(F_KN, hh)).astype(BF16)

    row = lax.broadcasted_iota(jnp.int32, (tile, 1), 0)
    pos = pos0 + step_tile * ls + (row & (ls - 1))
    for gi, win in enumerate(POOL_WINDOWS):
        lanes = slice(gi * POOL_GROUP_DIM, (gi + 1) * POOL_GROUP_DIM)
        acc = u[:, lanes]
        for back in range(1, win):
            acc = acc + upad[:, POOL_PAD - back:POOL_PAD - back + ls, lanes].reshape(tile, POOL_GROUP_DIM)
        cnt = jnp.minimum(win, pos + 1).astype(F32)
        st["z"][slot, :, lanes] = _mm(acc / cnt - u[:, lanes], wp_ref[gi]) * ps_ref[:, lanes]

    pool_tail = upad[:, POOL_PAD + ls - POOL_HIST:POOL_PAD + ls, :]
    upad[:, POOL_PAD - POOL_HIST:POOL_PAD, :] = pool_tail
    pn_ref[...] = pool_tail


def _mixer_stage2(slot, x_ref, on_ref, wout_ref, y_ref, sn_ref, s_scr, st, *, nseq, ls, chunk):
    tile = nseq * ls
    cps = ls // chunk
    cr = lax.broadcasted_iota(jnp.int32, (chunk, chunk), 0)
    cc = lax.broadcasted_iota(jnp.int32, (chunk, chunk), 1)
    incl = cr >= cc
    strict = cr > cc

    probs = [(hh, s, ci) for hh in range(DN_HEADS) for s in range(nseq) for ci in range(cps)]
    rows = lambda s, ci: slice(s * ls + ci * chunk, s * ls + (ci + 1) * chunk)
    head = lambda hh: slice(hh * HEAD_DIM, (hh + 1) * HEAD_DIM)
    f_col = lambda base, pr: st["ft"][slot, rows(pr[1], pr[2]), base + pr[0]:base + pr[0] + 1]
    gc_row = lambda pr: st["gce"][slot, pr[0]:pr[0] + 1, rows(pr[1], pr[2])]
    kb = [st["kb"][slot, rows(s, ci), head(hh)] for hh, s, ci in probs]
    qb = [st["qb"][slot, rows(s, ci), head(hh)] for hh, s, ci in probs]
    decay = [jnp.exp(jnp.where(incl, f_col(F_GC, pr) - gc_row(pr), -1e30)) for pr in probs]
    kk = [lax.dot_general(kb_p, kb_p, NT_DIMS, preferred_element_type=F32) for kb_p in kb]
    qk = [lax.dot_general(qb_p, kb_p, NT_DIMS, preferred_element_type=F32) for qb_p, kb_p in zip(qb, kb)]
    a_inv = _unit_lower_inverses(
        [jnp.where(strict, f_col(F_U, pr) * kk_p * dec_p, 0.0) for pr, kk_p, dec_p in zip(probs, kk, decay)], chunk)
    sol = [_mm(inv_p, st["rhs"][slot, rows(s, ci), 2 * hh * HEAD_DIM:(2 * hh + 2) * HEAD_DIM]).astype(BF16)
           for (hh, s, ci), inv_p in zip(probs, a_inv)]
    qw = [_mm(qk_p * dec_p, sol_p) for qk_p, dec_p, sol_p in zip(qk, decay, sol)]
    kw = [lax.dot_general(st["kdec"][slot, rows(s, ci), head(hh)], sol_p, TN_DIMS, preferred_element_type=F32)
          for (hh, s, ci), sol_p in zip(probs, sol)]

    lhs = {(hh, s, ci): jnp.concatenate([st["qdec"][slot, rows(s, ci), head(hh)] - qw_p[:, HEAD_DIM:],
                                         kw_p[:, HEAD_DIM:]], axis=0).astype(BF16)
           for (hh, s, ci), qw_p, kw_p in zip(probs, qw, kw)}
    o_add = {pr: qw_p[:, :HEAD_DIM] for pr, qw_p in zip(probs, qw)}
    s_add = {pr: kw_p[:, :HEAD_DIM] for pr, kw_p in zip(probs, kw)}
    states = {(hh, s): s_scr[s, hh] for hh in range(DN_HEADS) for s in range(nseq)}
    o_chunk = {}
    for ci in range(cps):
        for hh in range(DN_HEADS):
            for s in range(nseq):
                pr = (hh, s, ci)
                both = jnp.dot(lhs[pr], states[hh, s].astype(BF16), preferred_element_type=F32)
                o_chunk[pr] = o_add[pr] + both[:chunk]
                first = s * ls + ci * chunk
                e_chunk = st["gce"][slot, SUBLANES + hh:SUBLANES + hh + 1, first:first + 1]
                states[hh, s] = states[hh, s] * e_chunk - both[chunk:] + s_add[pr]
    o_heads = []
    for hh in range(DN_HEADS):
        for s in range(nseq):
            s_scr[s, hh] = states[hh, s]
        o_parts = [o_chunk[hh, s, ci] for s in range(nseq) for ci in range(cps)]
        o_h = jnp.concatenate(o_parts, axis=0) if len(o_parts) > 1 else o_parts[0]
        o_heads.append(_rms(o_h, on_ref[...]) * st["gact"][slot, :, head(hh)])

    mixed = jnp.concatenate(o_heads + [st["z"][slot]], axis=1).astype(BF16)
    x = x_ref[...].reshape(tile, x_ref.shape[-1])
    y = x + jnp.dot(mixed, wout_ref[...], preferred_element_type=F32)
    y_ref[...] = y.reshape(y_ref.shape)
    sn_ref[...] = s_scr[...]


STAGE_BUFFERS = (("qb", DN_WIDTH, BF16), ("kb", DN_WIDTH, BF16), ("kdec", DN_WIDTH, BF16), ("rhs", 2 * DN_WIDTH, BF16),
                 ("qdec", DN_WIDTH, F32), ("gact", DN_WIDTH, F32), ("z", POOL_WIDTH, F32), ("ft", LANES, F32))


def _mixer_kernel(x1_ref, x2_ref, nw_ref, win_ref, wab_ref, sel_ref, cum_ref, cw_ref, alog_ref, dtb_ref, on_ref,
                  wp_ref, ps_ref, wout_ref, s0_ref, c0_ref, p0_ref,
                  y_ref, sn_ref, cn_ref, pn_ref,
                  s_scr, xpad, upad, qkv_s, *stage_refs, nseq, ls, chunk, pos0, pipelined):
    step = pl.program_id(1)
    last_tile = pl.num_programs(1) - (2 if pipelined else 1)
    names = [name for name, _, _ in STAGE_BUFFERS] + ["gce"]
    sets = [dict(zip(names, stage_refs[k:k + len(names)])) for k in range(0, len(stage_refs), len(names))]
    stage1 = functools.partial(
        _mixer_stage1, 0, x_ref=x1_ref, nw_ref=nw_ref, win_ref=win_ref, wab_ref=wab_ref, sel_ref=sel_ref,
        cum_ref=cum_ref, cw_ref=cw_ref, alog_ref=alog_ref, dtb_ref=dtb_ref, wp_ref=wp_ref, ps_ref=ps_ref,
        cn_ref=cn_ref, pn_ref=pn_ref, xpad=xpad, upad=upad, qkv_s=qkv_s, nseq=nseq, ls=ls, chunk=chunk, pos0=pos0)
    stage2 = functools.partial(_mixer_stage2, 0, x_ref=x2_ref, on_ref=on_ref, wout_ref=wout_ref, y_ref=y_ref,
                               sn_ref=sn_ref, s_scr=s_scr, nseq=nseq, ls=ls, chunk=chunk)

    @pl.when(step == 0)
    def _():
        s_scr[...] = s0_ref[...]
        for sl in range(CONV_CH // LANES):
            xpad[:, sl, CONV_PAD - (CONV_WIDTH - 1):CONV_PAD, :] = c0_ref[:, :, sl * LANES:(sl + 1) * LANES]
        upad[:, POOL_PAD - POOL_HIST:POOL_PAD, :] = p0_ref[...]
        if pipelined:
            for ref in sets[1].values():
                ref[...] = jnp.zeros(ref.shape, ref.dtype)

    if pipelined:
        @pl.when(step == 1)
        def _():
            s_scr[...] = s0_ref[...]

        for parity in range(2):
            @pl.when((step & 1) == parity)
            def _():
                stage1(jnp.minimum(step, last_tile), st=sets[parity])
                stage2(st=sets[1 - parity])
    else:
        stage1(step, st=sets[0])
        stage2(st=sets[0])


def _chunk_sum_matrix(tile, chunk):
    j = np.arange(tile)[:, None]
    i = np.arange(tile)[None, :]
    same = (j // chunk) == (i // chunk)
    return jnp.asarray(np.concatenate([same & (j <= i), same], axis=1), dtype=BF16)


def _head_select_matrix():
    sel = np.zeros((2 * SUBLANES, 2 * DN_WIDTH), np.float32)
    for hh in range(DN_HEADS):
        sel[hh, hh * HEAD_DIM:(hh + 1) * HEAD_DIM] = 1.0
        sel[SUBLANES + hh, DN_WIDTH + hh * HEAD_DIM:DN_WIDTH + (hh + 1) * HEAD_DIM] = 1.0
    return jnp.asarray(sel, dtype=BF16)


def _mixer(x, params, states, *, layer, state_layer, nseq, ls, chunk, pos0):
    nw, w_in, w_ab_t, conv_w, alog, dtb, o_norm, w_pool, pool_scale, w_out = params
    state_shapes = [a.shape[2:] for a in states]
    b, l, d = x.shape
    assert ls & (ls - 1) == 0 and chunk & (chunk - 1) == 0 and ls % chunk == 0
    assert b % nseq == 0 and l % ls == 0 and (nseq == 1 or l == ls)
    tile = nseq * ls
    n_tiles = l // ls
    pipelined = False
    n_sets = 2 if pipelined else 1
    sel = _head_select_matrix()
    cum = _chunk_sum_matrix(tile, chunk)
    per_seq = lambda shape: pl.BlockSpec((nseq,) + shape, lambda i, j: (i,) + (0,) * len(shape))
    per_seq_in = lambda shape: pl.BlockSpec((None, nseq) + shape, lambda i, j: (state_layer, i) + (0,) * len(shape))
    stacked = lambda a: _resident(a.shape[1:], layer)
    if pipelined:
        tile_stage1 = lambda i, j: (i, jnp.minimum(j, n_tiles - 1), 0)
        tile_stage2 = lambda i, j: (i, jnp.maximum(j - 1, 0), 0)
    else:
        tile_stage1 = tile_stage2 = lambda i, j: (i, j, 0)
    out_shapes = (jax.ShapeDtypeStruct(x.shape, x.dtype),) + tuple(
        jax.ShapeDtypeStruct((b,) + shape, a.dtype) for shape, a in zip(state_shapes, states))
    stage_set = [pltpu.VMEM((1, tile, width), dtype) for _, width, dtype in STAGE_BUFFERS]
    stage_set.append(pltpu.VMEM((1, 2 * SUBLANES, tile), F32))
    stage_scratch = stage_set * n_sets
    return pl.pallas_call(
        functools.partial(_mixer_kernel, nseq=nseq, ls=ls, chunk=chunk, pos0=pos0, pipelined=pipelined),
        grid=(b // nseq, n_tiles + (1 if pipelined else 0)),
        in_specs=[pl.BlockSpec((nseq, ls, d), tile_stage1), pl.BlockSpec((nseq, ls, d), tile_stage2),
                  stacked(nw), stacked(w_in), stacked(w_ab_t), _resident(sel.shape), _resident(cum.shape),
                  stacked(conv_w), stacked(alog), stacked(dtb), stacked(o_norm), stacked(w_pool),
                  stacked(pool_scale), stacked(w_out),
                  *[per_seq_in(shape) for shape in state_shapes]],
        out_specs=(pl.BlockSpec((nseq, ls, d), tile_stage2), *[per_seq(shape) for shape in state_shapes]),
        out_shape=out_shapes,
        scratch_shapes=[pltpu.VMEM((nseq, DN_HEADS, HEAD_DIM, HEAD_DIM), F32),
                        pltpu.VMEM((nseq, CONV_CH // LANES, CONV_PAD + ls, LANES), F32),
                        pltpu.VMEM((nseq, POOL_PAD + ls, POOL_WIDTH), F32),
                        pltpu.VMEM((CONV_CH // LANES, tile, LANES), F32)] + stage_scratch,
        compiler_params=pltpu.CompilerParams(dimension_semantics=("arbitrary", "arbitrary"),
                                             vmem_limit_bytes=VMEM_LIMIT_BYTES),
        name="mixer",
    )(x, x, nw, w_in, w_ab_t, sel, cum, conv_w, alog, dtb, o_norm, w_pool, pool_scale, w_out, *states)


def _head_col(v):
    return jnp.pad(v.astype(F32), ((0, 0), (0, SUBLANES - DN_HEADS)))[:, :, None]


def kernel(x_prompt, x_sample, state_delta, state_conv, state_pool, norm_ffn1, w_ffn1_gate, w_ffn1_up, w_ffn1_down, norm_mix, w_in, conv_w, a_log, dt_bias, o_norm, w_pool, pool_scale, w_out, norm_ffn2, w_ffn2_gate, w_ffn2_up, w_ffn2_down, norm_final):
    depth = w_in.shape[0]
    batch, seq, d_model = x_prompt.shape
    dec_batch, dec_seq, _ = x_sample.shape

    c_ab = CONV_CH + DN_WIDTH
    c_u = c_ab + 2 * DN_HEADS
    w_in_main = jnp.concatenate([w_in[..., :c_ab], w_in[..., c_u:]], axis=-1).astype(BF16)
    w_a_t = jnp.swapaxes(w_in[..., c_ab:c_ab + DN_HEADS], 1, 2)
    w_b_t = jnp.swapaxes(w_in[..., c_ab + DN_HEADS:c_u], 1, 2)
    head_pad = jnp.zeros((depth, SUBLANES - DN_HEADS, d_model), w_in.dtype)
    w_ab_t = jnp.concatenate([w_a_t, head_pad, w_b_t, head_pad], axis=1).astype(BF16)
    row = lambda v: v[:, None, :]
    ffn1 = (row(norm_ffn1), w_ffn1_gate.astype(BF16), w_ffn1_up.astype(BF16), w_ffn1_down.astype(BF16))
    ffn2 = (row(norm_ffn2), w_ffn2_gate.astype(BF16), w_ffn2_up.astype(BF16), w_ffn2_down.astype(BF16))
    mix = (row(norm_mix), w_in_main, w_ab_t, conv_w, _head_col(a_log), _head_col(dt_bias), row(o_norm),
           w_pool.astype(BF16), row(pool_scale), w_out.astype(BF16))
    nf = norm_final[None, :]

    no_history = lambda s: jnp.zeros((1, batch) + s.shape[2:], s.dtype)
    groups = [
        dict(x=x_prompt, tm=1024, nseq=1, ls=512, chunk=PROMPT_CHUNK, pos0=0, state_layer=lambda layer: 0,
             states=(no_history(state_delta), no_history(state_conv), no_history(state_pool))),
        dict(x=x_sample, tm=dec_batch * dec_seq, nseq=dec_batch, ls=dec_seq, chunk=dec_seq, pos0=PAST_LEN,
             state_layer=lambda layer: layer, states=(state_delta, state_conv, state_pool)),
    ]
    results = []
    for grp in groups:
        x = grp["x"]
        b, l, _ = x.shape
        new_s, new_c, new_p = [], [], []
        for layer in range(depth):
            x2 = _ffn(x.reshape(b * l, d_model), *ffn1, layer=layer, tm=grp["tm"])
            x, s_n, c_n, p_n = _mixer(x2.reshape(b, l, d_model), mix, grp["states"], layer=layer,
                                      state_layer=grp["state_layer"](layer), nseq=grp["nseq"], ls=grp["ls"],
                                      chunk=grp["chunk"], pos0=grp["pos0"])
            fw = nf if layer == depth - 1 else None
            x = _ffn(x.reshape(b * l, d_model), *ffn2, fw, layer=layer, tm=grp["tm"]).reshape(b, l, d_model)
            new_s.append(s_n); new_c.append(c_n); new_p.append(p_n)
        results.append((x, jnp.stack(new_s), jnp.stack(new_c), jnp.stack(new_p)))
    (y_p, s_p, c_p, q_p), (y_s, s_s, c_s, q_s) = results
    return (y_p, y_s, s_p, c_p, q_p, s_s, c_s, q_s)
```

```python
import functools

import numpy as np
import jax
import jax.numpy as jnp
from jax import lax
from jax.experimental import pallas as pl
from jax.experimental.pallas import tpu as pltpu

EPS = 1e-6
PAST_LEN = 4096
DN_HEADS = 4
HEAD_DIM = 128
DN_WIDTH = DN_HEADS * HEAD_DIM
CONV_WIDTH = 4
CONV_CH = 3 * DN_WIDTH
POOL_WINDOWS = (2, 4, 8, 16)
POOL_GROUP_DIM = 128
POOL_WIDTH = len(POOL_WINDOWS) * POOL_GROUP_DIM
POOL_HIST = max(POOL_WINDOWS) - 1
PROMPT_CHUNK = 64
LANES = 128
SUBLANES = 8
INV_BASE = 16
ROW_STRIDE = 4
COL_GATE = CONV_CH
COL_POOL = COL_GATE + DN_WIDTH
N_IN_MAIN = COL_POOL + POOL_WIDTH
CONV_PAD = SUBLANES
POOL_PAD = 2 * SUBLANES
ROW_N, ROW_ETOT, ROW_BETA, ROW_W = (SUBLANES * i for i in range(4))
N_ROW_FACTORS = 4
COL_N, COL_Q, COL_QDEC = (SUBLANES * i for i in range(3))
VMEM_LIMIT_BYTES = 56 * 1024 * 1024

F32 = jnp.float32
BF16 = jnp.bfloat16
NT_DIMS = (((1,), (1,)), ((), ()))


def _rms(x, w):
    return x * lax.rsqrt(jnp.mean(x * x, axis=-1, keepdims=True) + EPS) * w


def _mm(a, b):
    return jnp.dot(a.astype(BF16), b.astype(BF16), preferred_element_type=F32)


def _resident(shape, layer=None):
    zeros = (0,) * len(shape)
    if layer is None:
        return pl.BlockSpec(shape, lambda *_: zeros, pipeline_mode=pl.Buffered(1))
    return pl.BlockSpec((None,) + shape, lambda *_: (layer,) + zeros, pipeline_mode=pl.Buffered(1))


def _ffn_kernel(*refs, f_chunk, n_main, final_norm):
    if final_norm:
        x_ref, xe_ref, nw_ref, wg_ref, wu_ref, wd_ref, fw_ref, o_ref, oe_ref, h_ref = refs
    else:
        x_ref, xe_ref, nw_ref, wg_ref, wu_ref, wd_ref, o_ref, oe_ref, h_ref = refs
    d_ff = wg_ref.shape[1]

    def tile(src_ref, dst_ref):
        rows = src_ref.shape[0]
        x = src_ref[...]
        xn = _rms(x, nw_ref[...]).astype(BF16)
        for c in range(d_ff // f_chunk):
            sl = slice(c * f_chunk, (c + 1) * f_chunk)
            g = jnp.dot(xn, wg_ref[:, sl], preferred_element_type=F32)
            u = jnp.dot(xn, wu_ref[:, sl], preferred_element_type=F32)
            h_ref[:rows, sl] = (g * jax.nn.sigmoid(g) * u).astype(BF16)
        y = x + 0.5 * jnp.dot(h_ref[:rows, :], wd_ref[...], preferred_element_type=F32)
        if final_norm:
            y = _rms(y, fw_ref[...])
        dst_ref[...] = y

    step = pl.program_id(0)
    pl.when(step < n_main)(lambda: tile(x_ref, o_ref))
    pl.when(step == n_main)(lambda: tile(xe_ref, oe_ref))


def _ffn(x, x_extra, nw, wg, wu, wd, fw=None, *, layer, tm):
    m, d = x.shape
    me = x_extra.shape[0]
    d_ff = wg.shape[-1]
    n_main = m // tm
    assert m % tm == 0 and me <= tm
    final_norm = fw is not None
    args = [x, x_extra, nw, wg, wu, wd] + ([fw] if final_norm else [])
    main_tile = lambda i: (jnp.minimum(i, n_main - 1), 0)
    in_specs = [pl.BlockSpec((tm, d), main_tile), pl.BlockSpec((me, d), lambda i: (0, 0)),
                _resident((1, d), layer), _resident((d, d_ff), layer), _resident((d, d_ff), layer),
                _resident((d_ff, d), layer)]
    in_specs += [_resident((1, d))] if final_norm else []
    return pl.pallas_call(
        functools.partial(_ffn_kernel, f_chunk=256, n_main=n_main, final_norm=final_norm),
        grid=(n_main + 1,),
        in_specs=in_specs,
        out_specs=(pl.BlockSpec((tm, d), main_tile), pl.BlockSpec((me, d), lambda i: (0, 0))),
        out_shape=(jax.ShapeDtypeStruct((m, d), x.dtype), jax.ShapeDtypeStruct((me, d), x.dtype)),
        scratch_shapes=[pltpu.VMEM((tm, d_ff), BF16)],
        compiler_params=pltpu.CompilerParams(dimension_semantics=("arbitrary",),
                                             vmem_limit_bytes=VMEM_LIMIT_BYTES),
        name="ffn",
    )(*args)


def _unit_lower_inverses(ns, chunk):
    r = lax.broadcasted_iota(jnp.int32, (chunk, chunk), 0)
    c = lax.broadcasted_iota(jnp.int32, (chunk, chunk), 1)
    base = min(chunk, INV_BASE)
    shift = base.bit_length() - 1
    if chunk > base:
        diag = (r >> shift) == (c >> shift)
        nds = [jnp.where(diag, n, 0.0) for n in ns]
    else:
        nds = ns
    eye = jnp.where(r == c, 1.0, 0.0)
    invs = [eye - nd for nd in nds]
    pws = [nd.astype(BF16) for nd in nds]
    pws = [jnp.dot(pw, pw, preferred_element_type=F32).astype(BF16) for pw in pws]
    order = 2
    while 2 * order < base:
        both = [jnp.dot(jnp.concatenate([inv.astype(BF16), pw], axis=0), pw, preferred_element_type=F32)
                for inv, pw in zip(invs, pws)]
        invs = [inv + bo[:chunk] for inv, bo in zip(invs, both)]
        pws = [bo[chunk:].astype(BF16) for bo in both]
        order *= 2
    invs = [inv + _mm(inv, pw) for inv, pw in zip(invs, pws)]
    blk = base
    while blk < chunk:
        s = blk.bit_length() - 1
        lower = ((r >> (s + 1)) == (c >> (s + 1))) & ((r >> s) > (c >> s))
        tmp = [_mm(inv, jnp.where(lower, n, 0.0)) for inv, n in zip(invs, ns)]
        invs = [inv - _mm(t, inv) for inv, t in zip(invs, tmp)]
        blk *= 2
    return invs


def _mixer_stage1(step, x_ref, nw_ref, win_ref, wab_ref, sel_ref, cum_ref, cw_ref, alog_ref, dtb_ref,
                  wp_ref, ps_ref, cn_ref, pn_ref, xpad, upad, qkv_s, st, *, nseq, ls, chunk, pos0):
    tile = nseq * ls
    x = x_ref[...].reshape(tile, x_ref.shape[-1])
    h = _rms(x, nw_ref[...]).astype(BF16)
    p = jnp.dot(h, win_ref[...], preferred_element_type=F32)
    pre = p[:, :CONV_CH]
    gate = p[:, COL_GATE:COL_POOL]
    u = p[:, COL_POOL:]
    st["gact"][...] = gate * jax.nn.sigmoid(gate)

    for sl in range(CONV_CH // LANES):
        lanes = slice(sl * LANES, (sl + 1) * LANES)
        xpad[:, sl, CONV_PAD:CONV_PAD + ls, :] = pre[:, lanes].reshape(nseq, ls, LANES)
        cw = cw_ref[:, lanes]
        if nseq == 1:
            for r in range(ROW_STRIDE):
                conv = None
                for j in range(CONV_WIDTH):
                    rows_j = pl.ds(CONV_PAD - (CONV_WIDTH - 1) + j + r, ls // ROW_STRIDE, stride=ROW_STRIDE)
                    term = xpad[0, sl, rows_j, :] * cw[j:j + 1]
                    conv = term if conv is None else conv + term
                qkv_s[sl, pl.ds(r, ls // ROW_STRIDE, stride=ROW_STRIDE), :] = conv * jax.nn.sigmoid(conv)
        else:
            conv = None
            for j in range(CONV_WIDTH):
                start = CONV_PAD - (CONV_WIDTH - 1) + j
                term = xpad[:, sl, start:start + ls, :].reshape(tile, LANES) * cw[j:j + 1]
                conv = term if conv is None else conv + term
            qkv_s[sl] = conv * jax.nn.sigmoid(conv)
        tail = xpad[:, sl, CONV_PAD + ls - (CONV_WIDTH - 1):CONV_PAD + ls, :]
        xpad[:, sl, CONV_PAD - (CONV_WIDTH - 1):CONV_PAD, :] = tail
        cn_ref[:, :, lanes] = tail
    upad[:, POOL_PAD:POOL_PAD + ls, :] = u.reshape(nseq, ls, POOL_WIDTH)
    for name, part in (("qb", 0), ("kb", 1), ("vb", 2)):
        for hh in range(DN_HEADS):
            st[name][:, hh * HEAD_DIM:(hh + 1) * HEAD_DIM] = qkv_s[part * DN_HEADS + hh].astype(BF16)

    ab_t = lax.dot_general(wab_ref[...], h, NT_DIMS, preferred_element_type=F32)
    qk_part = jnp.concatenate([qkv_s[sl] for sl in range(2 * DN_HEADS)], axis=1)
    ss_t = lax.dot_general(sel_ref[...], (qk_part * qk_part).astype(BF16), NT_DIMS,
                           preferred_element_type=F32)
    a_pre = ab_t[:SUBLANES] + dtb_ref[...]
    softplus = jnp.maximum(a_pre, 0.0) + jnp.log1p(jnp.exp(-jnp.abs(a_pre)))
    g = -jnp.exp(alog_ref[...]) * softplus
    beta = jax.nn.sigmoid(ab_t[SUBLANES:])
    r_q = lax.rsqrt(ss_t[:SUBLANES] + EPS) * (HEAD_DIM ** -0.5)
    r_k = lax.rsqrt(ss_t[SUBLANES:] + EPS)
    g_hi = g.astype(BF16).astype(F32)
    g_mid = (g - g_hi).astype(BF16).astype(F32)
    g_lo = g - g_hi - g_mid
    pieces = jnp.concatenate([g_hi, g_mid, g_lo, jnp.zeros_like(g)], axis=0).astype(BF16)
    sums = jnp.dot(pieces, cum_ref[...], preferred_element_type=F32)
    part = lambda k, half: sums[k * SUBLANES:(k + 1) * SUBLANES, half * tile:(half + 1) * tile]
    gc = part(0, 0) + part(1, 0) + part(2, 0)
    gs = part(0, 1) + part(1, 1) + part(2, 1)
    e_gc = jnp.exp(gc)
    row_n = gc - jnp.log(r_k)
    st["rowf"][...] = jnp.concatenate([row_n, jnp.exp(gs), beta, r_k * (beta * e_gc)], axis=0)
    cols = [gc + jnp.log(beta * r_k), gc + jnp.log(r_q), r_q * e_gc]
    st["colf"][...] = jnp.concatenate(cols + [jnp.zeros((LANES - SUBLANES * len(cols), tile), F32)], axis=0).T
    k_dec_row = r_k * jnp.exp(gs - gc)
    for hh in range(DN_HEADS):
        st["kdt"][hh] = (qkv_s[DN_HEADS + hh].T * k_dec_row[hh:hh + 1, :]).astype(BF16)

    row = lax.broadcasted_iota(jnp.int32, (tile, 1), 0)
    pos = pos0 + step * ls + (row & (ls - 1))
    for gi, win in enumerate(POOL_WINDOWS):
        lanes = slice(gi * POOL_GROUP_DIM, (gi + 1) * POOL_GROUP_DIM)
        acc = u[:, lanes]
        for back in range(1, win):
            acc = acc + upad[:, POOL_PAD - back:POOL_PAD - back + ls, lanes].reshape(tile, POOL_GROUP_DIM)
        cnt = jnp.minimum(win, pos + 1).astype(F32)
        st["z"][:, lanes] = _mm(acc / cnt - u[:, lanes], wp_ref[gi]) * ps_ref[:, lanes]

    pool_tail = upad[:, POOL_PAD + ls - POOL_HIST:POOL_PAD + ls, :]
    upad[:, POOL_PAD - POOL_HIST:POOL_PAD, :] = pool_tail
    pn_ref[...] = pool_tail


def _mixer_stage2(x_ref, on_ref, wout_ref, y_ref, sn_ref, s_scr, qkv_s, st, *, nseq, ls, chunk):
    tile = nseq * ls
    cps = ls // chunk
    cr = lax.broadcasted_iota(jnp.int32, (chunk, chunk), 0)
    cc = lax.broadcasted_iota(jnp.int32, (chunk, chunk), 1)
    incl = cr >= cc
    strict = cr > cc

    probs = [(hh, s, ci) for hh in range(DN_HEADS) for s in range(nseq) for ci in range(cps)]
    rows = lambda pr: slice(pr[1] * ls + pr[2] * chunk, pr[1] * ls + (pr[2] + 1) * chunk)
    head = lambda pr: slice(pr[0] * HEAD_DIM, (pr[0] + 1) * HEAD_DIM)
    col = lambda base, pr: st["colf"][rows(pr), base + pr[0]:base + pr[0] + 1]
    row = lambda base, pr: st["rowf"][base + pr[0]:base + pr[0] + 1, rows(pr)]
    kq = [lax.dot_general(jnp.concatenate([st["kb"][rows(pr), head(pr)], st["qb"][rows(pr), head(pr)]], axis=0),
                          st["kb"][rows(pr), head(pr)], NT_DIMS, preferred_element_type=F32) for pr in probs]
    ns = [jnp.where(strict, kq_p[:chunk] * jnp.exp(jnp.where(strict, col(COL_N, pr) - row(ROW_N, pr), -1e30)), 0.0)
          for pr, kq_p in zip(probs, kq)]
    qkd = [kq_p[chunk:] * jnp.exp(jnp.where(incl, col(COL_Q, pr) - row(ROW_N, pr), -1e30))
           for pr, kq_p in zip(probs, kq)]
    a_inv = _unit_lower_inverses(ns, chunk)
    sol = [jnp.concatenate([_mm(inv_p * row(ROW_BETA, pr), st["vb"][rows(pr), head(pr)]),
                            _mm(inv_p * row(ROW_W, pr), st["kb"][rows(pr), head(pr)])], axis=1).astype(BF16)
           for pr, inv_p in zip(probs, a_inv)]
    qw = [_mm(qkd_p, sol_p) for qkd_p, sol_p in zip(qkd, sol)]
    kw = [jnp.dot(st["kdt"][pr[0], :, rows(pr)], sol_p, preferred_element_type=F32) for pr, sol_p in zip(probs, sol)]

    lhs = {pr: jnp.concatenate([qkv_s[pr[0], rows(pr), :] * col(COL_QDEC, pr) - qw_p[:, HEAD_DIM:],
                                kw_p[:, HEAD_DIM:]], axis=0).astype(BF16)
           for pr, qw_p, kw_p in zip(probs, qw, kw)}
    o_add = {pr: qw_p[:, :HEAD_DIM] for pr, qw_p in zip(probs, qw)}
    s_add = {pr: kw_p[:, :HEAD_DIM] for pr, kw_p in zip(probs, kw)}
    states = {(hh, s): s_scr[s, hh] for hh in range(DN_HEADS) for s in range(nseq)}
    o_chunk = {}
    for ci in range(cps):
        for hh in range(DN_HEADS):
            for s in range(nseq):
                pr = (hh, s, ci)
                both = jnp.dot(lhs[pr], states[hh, s].astype(BF16), preferred_element_type=F32)
                o_chunk[pr] = o_add[pr] + both[:chunk]
                first = s * ls + ci * chunk
                e_chunk = st["rowf"][ROW_ETOT + hh:ROW_ETOT + hh + 1, first:first + 1]
                states[hh, s] = states[hh, s] * e_chunk - both[chunk:] + s_add[pr]
    o_heads = []
    for hh in range(DN_HEADS):
        for s in range(nseq):
            s_scr[s, hh] = states[hh, s]
        o_parts = [o_chunk[hh, s, ci] for s in range(nseq) for ci in range(cps)]
        o_h = jnp.concatenate(o_parts, axis=0) if len(o_parts) > 1 else o_parts[0]
        o_heads.append(_rms(o_h, on_ref[...]) * st["gact"][:, hh * HEAD_DIM:(hh + 1) * HEAD_DIM])

    mixed = jnp.concatenate(o_heads + [st["z"][...]], axis=1).astype(BF16)
    x = x_ref[...].reshape(tile, x_ref.shape[-1])
    y = x + jnp.dot(mixed, wout_ref[...], preferred_element_type=F32)
    y_ref[...] = y.reshape(y_ref.shape)
    sn_ref[...] = s_scr[...]


def _stage_buffers(tile):
    return {
        "qb": pltpu.VMEM((tile, DN_WIDTH), BF16), "kb": pltpu.VMEM((tile, DN_WIDTH), BF16),
        "vb": pltpu.VMEM((tile, DN_WIDTH), BF16), "kdt": pltpu.VMEM((DN_HEADS, HEAD_DIM, tile), BF16),
        "gact": pltpu.VMEM((tile, DN_WIDTH), F32), "z": pltpu.VMEM((tile, POOL_WIDTH), F32),
        "colf": pltpu.VMEM((tile, LANES), F32), "rowf": pltpu.VMEM((N_ROW_FACTORS * SUBLANES, tile), F32),
    }


def _mixer_kernel(x_ref, nw_ref, win_ref, wab_ref, sel_ref, cum_ref, cw_ref, alog_ref, dtb_ref, on_ref,
                  wp_ref, ps_ref, wout_ref, s0_ref, c0_ref, p0_ref, sprev_ref, cprev_ref, pprev_ref,
                  y_ref, sn_ref, cn_ref, pn_ref,
                  s_scr, xpad, upad, qkv_s, *stage_refs, nseq, ls, chunk, pos0):
    del sprev_ref, cprev_ref, pprev_ref
    step = pl.program_id(1)
    st = dict(zip(_stage_buffers(nseq * ls), stage_refs))

    @pl.when(step == 0)
    def _():
        s_scr[...] = s0_ref[...]
        for sl in range(CONV_CH // LANES):
            xpad[:, sl, CONV_PAD - (CONV_WIDTH - 1):CONV_PAD, :] = c0_ref[:, :, sl * LANES:(sl + 1) * LANES]
        upad[:, POOL_PAD - POOL_HIST:POOL_PAD, :] = p0_ref[...]

    _mixer_stage1(step, x_ref, nw_ref, win_ref, wab_ref, sel_ref, cum_ref, cw_ref, alog_ref, dtb_ref, wp_ref, ps_ref,
                  cn_ref, pn_ref, xpad, upad, qkv_s, st, nseq=nseq, ls=ls, chunk=chunk, pos0=pos0)
    _mixer_stage2(x_ref, on_ref, wout_ref, y_ref, sn_ref, s_scr, qkv_s, st, nseq=nseq, ls=ls, chunk=chunk)


def _chunk_sum_matrix(tile, chunk):
    j = np.arange(tile)[:, None]
    i = np.arange(tile)[None, :]
    same = (j // chunk) == (i // chunk)
    return jnp.asarray(np.concatenate([same & (j <= i), same], axis=1), dtype=BF16)


def _head_select_matrix():
    sel = np.zeros((2 * SUBLANES, 2 * DN_WIDTH), np.float32)
    for hh in range(DN_HEADS):
        sel[hh, hh * HEAD_DIM:(hh + 1) * HEAD_DIM] = 1.0
        sel[SUBLANES + hh, DN_WIDTH + hh * HEAD_DIM:DN_WIDTH + (hh + 1) * HEAD_DIM] = 1.0
    return jnp.asarray(sel, dtype=BF16)


def _mixer(x, params, states, new_states, *, layer, state_layer, nseq, ls, chunk, pos0):
    nw, w_in, w_ab_t, conv_w, alog, dtb, o_norm, w_pool, pool_scale, w_out = params
    b, l, d = x.shape
    assert ls & (ls - 1) == 0 and chunk & (chunk - 1) == 0 and ls % chunk == 0
    assert b % nseq == 0 and l % ls == 0 and (nseq == 1 or l == ls)
    tile = nseq * ls
    sel = _head_select_matrix()
    cum = _chunk_sum_matrix(tile, chunk)
    stacked = lambda a: _resident(a.shape[1:], layer)
    seq_block = lambda a, which: pl.BlockSpec((None, nseq) + a.shape[2:],
                                              lambda i, j: (which, i) + (0,) * (a.ndim - 2))
    x_spec = pl.BlockSpec((nseq, ls, d), lambda i, j: (i, j, 0))
    untouched = pl.BlockSpec(memory_space=pl.ANY)
    n_in = 16
    return pl.pallas_call(
        functools.partial(_mixer_kernel, nseq=nseq, ls=ls, chunk=chunk, pos0=pos0),
        grid=(b // nseq, l // ls),
        in_specs=[x_spec, stacked(nw), stacked(w_in), stacked(w_ab_t), _resident(sel.shape), _resident(cum.shape),
                  stacked(conv_w), stacked(alog), stacked(dtb), stacked(o_norm), stacked(w_pool),
                  stacked(pool_scale), stacked(w_out),
                  *[seq_block(a, state_layer) for a in states], untouched, untouched, untouched],
        out_specs=(x_spec, *[seq_block(a, layer) for a in new_states]),
        out_shape=(jax.ShapeDtypeStruct(x.shape, x.dtype),
                   *[jax.ShapeDtypeStruct(a.shape, a.dtype) for a in new_states]),
        input_output_aliases={n_in + k: 1 + k for k in range(len(new_states))},
        scratch_shapes=[pltpu.VMEM((nseq, DN_HEADS, HEAD_DIM, HEAD_DIM), F32),
                        pltpu.VMEM((nseq, CONV_CH // LANES, CONV_PAD + ls, LANES), F32),
                        pltpu.VMEM((nseq, POOL_PAD + ls, POOL_WIDTH), F32),
                        pltpu.VMEM((CONV_CH // LANES, tile, LANES), F32)] + list(_stage_buffers(tile).values()),
        compiler_params=pltpu.CompilerParams(dimension_semantics=("arbitrary", "arbitrary"),
                                             vmem_limit_bytes=VMEM_LIMIT_BYTES),
        name="mixer",
    )(x, nw, w_in, w_ab_t, sel, cum, conv_w, alog, dtb, o_norm, w_pool, pool_scale, w_out, *states, *new_states)


def _head_col(v):
    return jnp.pad(v.astype(F32), ((0, 0), (0, SUBLANES - DN_HEADS)))[:, :, None]


def kernel(x_prompt, x_sample, state_delta, state_conv, state_pool, norm_ffn1, w_ffn1_gate, w_ffn1_up, w_ffn1_down, norm_mix, w_in, conv_w, a_log, dt_bias, o_norm, w_pool, pool_scale, w_out, norm_ffn2, w_ffn2_gate, w_ffn2_up, w_ffn2_down, norm_final):
    depth = w_in.shape[0]
    batch, seq, d_model = x_prompt.shape
    dec_batch, dec_seq, _ = x_sample.shape

    c_ab = CONV_CH + DN_WIDTH
    c_u = c_ab + 2 * DN_HEADS
    w_in_main = jnp.concatenate([w_in[..., :c_ab], w_in[..., c_u:]], axis=-1).astype(BF16)
    w_a_t = jnp.swapaxes(w_in[..., c_ab:c_ab + DN_HEADS], 1, 2)
    w_b_t = jnp.swapaxes(w_in[..., c_ab + DN_HEADS:c_u], 1, 2)
    head_pad = jnp.zeros((depth, SUBLANES - DN_HEADS, d_model), w_in.dtype)
    w_ab_t = jnp.concatenate([w_a_t, head_pad, w_b_t, head_pad], axis=1).astype(BF16)
    row = lambda v: v[:, None, :]
    ffn1 = (row(norm_ffn1), w_ffn1_gate.astype(BF16), w_ffn1_up.astype(BF16), w_ffn1_down.astype(BF16))
    ffn2 = (row(norm_ffn2), w_ffn2_gate.astype(BF16), w_ffn2_up.astype(BF16), w_ffn2_down.astype(BF16))
    mix = (row(norm_mix), w_in_main, w_ab_t, conv_w, _head_col(a_log), _head_col(dt_bias), row(o_norm),
           w_pool.astype(BF16), row(pool_scale), w_out.astype(BF16))
    nf = norm_final[None, :]

    state_in = (state_delta, state_conv, state_pool)
    prompt = dict(nseq=1, ls=512, chunk=PROMPT_CHUNK, pos0=0, state_layer=lambda layer: 0,
                  states=tuple(jnp.zeros((1, batch) + s.shape[2:], s.dtype) for s in state_in))
    sample = dict(nseq=dec_batch, ls=dec_seq, chunk=dec_seq, pos0=PAST_LEN, state_layer=lambda layer: layer,
                  states=state_in)
    xp, xs = x_prompt.reshape(batch * seq, d_model), x_sample.reshape(dec_batch * dec_seq, d_model)
    new_p = tuple(jnp.zeros((depth, batch) + s.shape[2:], s.dtype) for s in state_in)
    new_s = tuple(jnp.zeros_like(s) for s in state_in)
    for layer in range(depth):
        xp, xs = _ffn(xp, xs, *ffn1, layer=layer, tm=1024)
        outs = []
        for grp, x, new in ((prompt, xp.reshape(x_prompt.shape), new_p), (sample, xs.reshape(x_sample.shape), new_s)):
            outs.append(_mixer(x, mix, grp["states"], new, layer=layer, state_layer=grp["state_layer"](layer),
                               nseq=grp["nseq"], ls=grp["ls"], chunk=grp["chunk"], pos0=grp["pos0"]))
        (xp, *new_p), (xs, *new_s) = outs
        fw = nf if layer == depth - 1 else None
        xp, xs = _ffn(xp.reshape(batch * seq, d_model), xs.reshape(dec_batch * dec_seq, d_model), *ffn2, fw,
                      layer=layer, tm=1024)
    return (xp.reshape(x_prompt.shape), xs.reshape(x_sample.shape), *new_p, *new_s)
```

```python
import functools

import numpy as np
import jax
import jax.numpy as jnp
from jax import lax
from jax.experimental import pallas as pl
from jax.experimental.pallas import tpu as pltpu

EPS = 1e-6
PAST_LEN = 4096
DN_HEADS = 4
HEAD_DIM = 128
DN_WIDTH = DN_HEADS * HEAD_DIM
CONV_WIDTH = 4
CONV_CH = 3 * DN_WIDTH
POOL_WINDOWS = (2, 4, 8, 16)
POOL_GROUP_DIM = 128
POOL_WIDTH = len(POOL_WINDOWS) * POOL_GROUP_DIM
POOL_HIST = max(POOL_WINDOWS) - 1
PROMPT_CHUNK = 64
LANES = 128
SUBLANES = 8
INV_BASE = 16
ROW_STRIDE = 4
COL_GATE = CONV_CH
COL_POOL = COL_GATE + DN_WIDTH
COL_POOL_RAW = COL_POOL + 2 * DN_HEADS
CONV_PAD = SUBLANES
POOL_PAD = 2 * SUBLANES
ROW_N, ROW_ETOT, ROW_BETA, ROW_W = (SUBLANES * i for i in range(4))
N_ROW_FACTORS = 4
COL_N, COL_Q, COL_QDEC = (SUBLANES * i for i in range(3))
VMEM_LIMIT_BYTES = 56 * 1024 * 1024

F32 = jnp.float32
BF16 = jnp.bfloat16
NT_DIMS = (((1,), (1,)), ((), ()))


def _rms(x, w):
    return x * lax.rsqrt(jnp.mean(x * x, axis=-1, keepdims=True) + EPS) * w


def _mm(a, b):
    return jnp.dot(a.astype(BF16), b.astype(BF16), preferred_element_type=F32)


def _resident(shape, layer=None):
    zeros = (0,) * len(shape)
    if layer is None:
        return pl.BlockSpec(shape, lambda *_: zeros, pipeline_mode=pl.Buffered(1))
    return pl.BlockSpec((None,) + shape, lambda *_: (layer,) + zeros, pipeline_mode=pl.Buffered(1))


def _ffn_kernel(*refs, f_chunk, n_main, final_norm):
    if final_norm:
        x_ref, xe_ref, nw_ref, wg_ref, wu_ref, wd_ref, fw_ref, o_ref, oe_ref, h_ref = refs
    else:
        x_ref, xe_ref, nw_ref, wg_ref, wu_ref, wd_ref, o_ref, oe_ref, h_ref = refs
    d_ff = wg_ref.shape[1]

    def tile(src_ref, dst_ref):
        rows = src_ref.shape[0]
        x = src_ref[...]
        xn = _rms(x, nw_ref[...]).astype(BF16)
        for c in range(d_ff // f_chunk):
            sl = slice(c * f_chunk, (c + 1) * f_chunk)
            g = jnp.dot(xn, wg_ref[:, sl], preferred_element_type=F32)
            u = jnp.dot(xn, wu_ref[:, sl], preferred_element_type=F32)
            h_ref[:rows, sl] = (g * jax.nn.sigmoid(g) * u).astype(BF16)
        y = x + 0.5 * jnp.dot(h_ref[:rows, :], wd_ref[...], preferred_element_type=F32)
        if final_norm:
            y = _rms(y, fw_ref[...])
        dst_ref[...] = y

    step = pl.program_id(0)
    pl.when(step < n_main)(lambda: tile(x_ref, o_ref))
    pl.when(step == n_main)(lambda: tile(xe_ref, oe_ref))


def _ffn(x, x_extra, nw, wg, wu, wd, fw=None, *, layer, tm):
    m, d = x.shape
    me = x_extra.shape[0]
    d_ff = wg.shape[-1]
    n_main = m // tm
    assert m % tm == 0 and me <= tm
    final_norm = fw is not None
    args = [x, x_extra, nw, wg, wu, wd] + ([fw] if final_norm else [])
    main_tile = lambda i: (jnp.minimum(i, n_main - 1), 0)
    in_specs = [pl.BlockSpec((tm, d), main_tile), pl.BlockSpec((me, d), lambda i: (0, 0)),
                _resident((1, d), layer), _resident((d, d_ff), layer), _resident((d, d_ff), layer),
                _resident((d_ff, d), layer)]
    in_specs += [_resident((1, d))] if final_norm else []
    return pl.pallas_call(
        functools.partial(_ffn_kernel, f_chunk=256, n_main=n_main, final_norm=final_norm),
        grid=(n_main + 1,),
        in_specs=in_specs,
        out_specs=(pl.BlockSpec((tm, d), main_tile), pl.BlockSpec((me, d), lambda i: (0, 0))),
        out_shape=(jax.ShapeDtypeStruct((m, d), x.dtype), jax.ShapeDtypeStruct((me, d), x.dtype)),
        scratch_shapes=[pltpu.VMEM((tm, d_ff), BF16)],
        compiler_params=pltpu.CompilerParams(dimension_semantics=("arbitrary",),
                                             vmem_limit_bytes=VMEM_LIMIT_BYTES),
        name="ffn",
    )(*args)


def _unit_lower_inverses(ns, chunk):
    r = lax.broadcasted_iota(jnp.int32, (chunk, chunk), 0)
    c = lax.broadcasted_iota(jnp.int32, (chunk, chunk), 1)
    base = min(chunk, INV_BASE)
    shift = base.bit_length() - 1
    if chunk > base:
        diag = (r >> shift) == (c >> shift)
        nds = [jnp.where(diag, n, 0.0) for n in ns]
    else:
        nds = ns
    eye = jnp.where(r == c, 1.0, 0.0)
    invs = [eye - nd for nd in nds]
    pws = [nd.astype(BF16) for nd in nds]
    pws = [jnp.dot(pw, pw, preferred_element_type=F32).astype(BF16) for pw in pws]
    yield
    order = 2
    while 2 * order < base:
        both = [jnp.dot(jnp.concatenate([inv.astype(BF16), pw], axis=0), pw, preferred_element_type=F32)
                for inv, pw in zip(invs, pws)]
        invs = [inv + bo[:chunk] for inv, bo in zip(invs, both)]
        pws = [bo[chunk:].astype(BF16) for bo in both]
        order *= 2
        yield
    invs = [inv + _mm(inv, pw) for inv, pw in zip(invs, pws)]
    yield
    blk = base
    while blk < chunk:
        s = blk.bit_length() - 1
        lower = ((r >> (s + 1)) == (c >> (s + 1))) & ((r >> s) > (c >> s))
        tmp = [_mm(inv, jnp.where(lower, n, 0.0)) for inv, n in zip(invs, ns)]
        yield
        invs = [inv - _mm(t, inv) for inv, t in zip(invs, tmp)]
        yield
        blk *= 2
    return invs


def _mixer_stage1(step, r0, hl, x_ref, nw_ref, win_ref, wu_s, wab_ref, sel_ref, cum_ref, cw_ref, alog_ref, dtb_ref,
                  wp_ref, ps_ref, xpad, upad, qkv_s, st, *, nseq, ls, pos0):
    tile = nseq * hl
    tr = slice(nseq * r0, nseq * r0 + tile)
    x = x_ref[:, r0:r0 + hl, :].reshape(tile, x_ref.shape[-1])
    h = _rms(x, nw_ref[...]).astype(BF16)
    project = lambda c0, c1: jnp.dot(h, win_ref[:, c0:c1], preferred_element_type=F32)
    yield

    for sl in range(CONV_CH // LANES):
        lanes = slice(sl * LANES, (sl + 1) * LANES)
        if sl % DN_HEADS == 0:
            if sl:
                yield
            pre = project(sl * LANES, (sl + DN_HEADS) * LANES)
        xpad[:, sl, CONV_PAD + r0:CONV_PAD + r0 + hl, :] = (
            pre[:, (sl % DN_HEADS) * LANES:(sl % DN_HEADS + 1) * LANES].reshape(nseq, hl, LANES))
        cw = cw_ref[:, lanes]
        if nseq == 1:
            for r in range(ROW_STRIDE):
                conv = None
                for j in range(CONV_WIDTH):
                    rows_j = pl.ds(CONV_PAD + r0 - (CONV_WIDTH - 1) + j + r, hl // ROW_STRIDE, stride=ROW_STRIDE)
                    term = xpad[0, sl, rows_j, :] * cw[j:j + 1]
                    conv = term if conv is None else conv + term
                qkv_s[sl, pl.ds(r0 + r, hl // ROW_STRIDE, stride=ROW_STRIDE), :] = conv * jax.nn.sigmoid(conv)
        else:
            conv = None
            for j in range(CONV_WIDTH):
                start = CONV_PAD + r0 - (CONV_WIDTH - 1) + j
                term = xpad[:, sl, start:start + hl, :].reshape(tile, LANES) * cw[j:j + 1]
                conv = term if conv is None else conv + term
            qkv_s[sl, tr, :] = conv * jax.nn.sigmoid(conv)
    yield
    gate = project(COL_GATE, COL_POOL)
    st["gact"][tr, :] = gate * jax.nn.sigmoid(gate)
    yield
    u = jnp.dot(h, wu_s[...], preferred_element_type=F32)
    upad[:, POOL_PAD + r0:POOL_PAD + r0 + hl, :] = u.reshape(nseq, hl, POOL_WIDTH)
    yield
    for name, part in (("qb", 0), ("kb", 1), ("vb", 2)):
        for hh in range(DN_HEADS):
            st[name][tr, hh * HEAD_DIM:(hh + 1) * HEAD_DIM] = qkv_s[part * DN_HEADS + hh, tr, :].astype(BF16)

    ab_t = lax.dot_general(wab_ref[...], h, NT_DIMS, preferred_element_type=F32)
    qk_part = jnp.concatenate([qkv_s[sl, tr, :] for sl in range(2 * DN_HEADS)], axis=1)
    ss_t = lax.dot_general(sel_ref[...], (qk_part * qk_part).astype(BF16), NT_DIMS,
                           preferred_element_type=F32)
    yield
    a_pre = ab_t[:SUBLANES] + dtb_ref[...]
    softplus = jnp.maximum(a_pre, 0.0) + jnp.log1p(jnp.exp(-jnp.abs(a_pre)))
    g = -jnp.exp(alog_ref[...]) * softplus
    beta = jax.nn.sigmoid(ab_t[SUBLANES:])
    r_q = lax.rsqrt(ss_t[:SUBLANES] + EPS) * (HEAD_DIM ** -0.5)
    r_k = lax.rsqrt(ss_t[SUBLANES:] + EPS)
    g_hi = g.astype(BF16).astype(F32)
    g_mid = (g - g_hi).astype(BF16).astype(F32)
    g_lo = g - g_hi - g_mid
    pieces = jnp.concatenate([g_hi, g_mid, g_lo, jnp.zeros_like(g)], axis=0).astype(BF16)
    sums = jnp.dot(pieces, cum_ref[...], preferred_element_type=F32)
    part = lambda k, half: sums[k * SUBLANES:(k + 1) * SUBLANES, half * tile:(half + 1) * tile]
    gc = part(0, 0) + part(1, 0) + part(2, 0)
    gs = part(0, 1) + part(1, 1) + part(2, 1)
    e_gc = jnp.exp(gc)
    row_n = gc - jnp.log(r_k)
    st["rowf"][:, tr] = jnp.concatenate([row_n, jnp.exp(gs), beta, r_k * (beta * e_gc)], axis=0)
    cols = [gc + jnp.log(beta * r_k), gc + jnp.log(r_q), r_q * e_gc]
    st["colf"][tr, :] = jnp.concatenate(cols + [jnp.zeros((LANES - SUBLANES * len(cols), tile), F32)], axis=0).T
    yield
    k_dec_row = r_k * jnp.exp(gs - gc)
    for hh in range(DN_HEADS):
        st["kdt"][hh, :, tr] = (qkv_s[DN_HEADS + hh, tr, :].T * k_dec_row[hh:hh + 1, :]).astype(BF16)
    yield

    row = lax.broadcasted_iota(jnp.int32, (tile, 1), 0)
    pos = pos0 + step * ls + r0 + (row & (hl - 1))
    for gi, win in enumerate(POOL_WINDOWS):
        lanes = slice(gi * POOL_GROUP_DIM, (gi + 1) * POOL_GROUP_DIM)
        acc = u[:, lanes]
        for back in range(1, win):
            start = POOL_PAD + r0 - back
            acc = acc + upad[:, start:start + hl, lanes].reshape(tile, POOL_GROUP_DIM)
        cnt = jnp.minimum(win, pos + 1).astype(F32)
        st["z"][tr, lanes] = _mm(acc / cnt - u[:, lanes], wp_ref[gi]) * ps_ref[:, lanes]
        yield


def _mixer_stage2(r0, hl, states, x_ref, on_ref, wout_ref, y_ref, qkv_s, st, *, nseq, ls, chunk):
    tile = nseq * hl
    tr = slice(nseq * r0, nseq * r0 + tile)
    chunks = range(r0 // chunk, (r0 + hl) // chunk)
    cr = lax.broadcasted_iota(jnp.int32, (chunk, chunk), 0)
    cc = lax.broadcasted_iota(jnp.int32, (chunk, chunk), 1)
    incl = cr >= cc
    strict = cr > cc

    probs = [(hh, s, ci) for hh in range(DN_HEADS) for s in range(nseq) for ci in chunks]
    rows = lambda pr: slice(pr[1] * ls + pr[2] * chunk, pr[1] * ls + (pr[2] + 1) * chunk)
    head = lambda pr: slice(pr[0] * HEAD_DIM, (pr[0] + 1) * HEAD_DIM)
    col = lambda base, pr: st["colf"][rows(pr), base + pr[0]:base + pr[0] + 1]
    row = lambda base, pr: st["rowf"][base + pr[0]:base + pr[0] + 1, rows(pr)]
    kq = [lax.dot_general(jnp.concatenate([st["kb"][rows(pr), head(pr)], st["qb"][rows(pr), head(pr)]], axis=0),
                          st["kb"][rows(pr), head(pr)], NT_DIMS, preferred_element_type=F32) for pr in probs]
    yield
    ns = [jnp.where(strict, kq_p[:chunk] * jnp.exp(jnp.where(strict, col(COL_N, pr) - row(ROW_N, pr), -1e30)), 0.0)
          for pr, kq_p in zip(probs, kq)]
    qkd = [kq_p[chunk:] * jnp.exp(jnp.where(incl, col(COL_Q, pr) - row(ROW_N, pr), -1e30))
           for pr, kq_p in zip(probs, kq)]
    yield
    a_inv = yield from _unit_lower_inverses(ns, chunk)
    sol = [jnp.concatenate([_mm(inv_p * row(ROW_BETA, pr), st["vb"][rows(pr), head(pr)]),
                            _mm(inv_p * row(ROW_W, pr), st["kb"][rows(pr), head(pr)])], axis=1).astype(BF16)
           for pr, inv_p in zip(probs, a_inv)]
    yield
    qw = [_mm(qkd_p, sol_p) for qkd_p, sol_p in zip(qkd, sol)]
    yield
    kw = [jnp.dot(st["kdt"][pr[0], :, rows(pr)], sol_p, preferred_element_type=F32) for pr, sol_p in zip(probs, sol)]
    yield

    lhs = {pr: jnp.concatenate([qkv_s[pr[0], rows(pr), :] * col(COL_QDEC, pr) - qw_p[:, HEAD_DIM:],
                                kw_p[:, HEAD_DIM:]], axis=0).astype(BF16)
           for pr, qw_p, kw_p in zip(probs, qw, kw)}
    o_add = {pr: qw_p[:, :HEAD_DIM] for pr, qw_p in zip(probs, qw)}
    s_add = {pr: kw_p[:, :HEAD_DIM] for pr, kw_p in zip(probs, kw)}
    o_chunk = {}
    yield
    for ci in chunks:
        for hh in range(DN_HEADS):
            for s in range(nseq):
                pr = (hh, s, ci)
                both = jnp.dot(lhs[pr], states[hh, s].astype(BF16), preferred_element_type=F32)
                o_chunk[pr] = o_add[pr] + both[:chunk]
                first = s * ls + ci * chunk
                e_chunk = st["rowf"][ROW_ETOT + hh:ROW_ETOT + hh + 1, first:first + 1]
                states[hh, s] = states[hh, s] * e_chunk - both[chunk:] + s_add[pr]
        yield
    o_heads = []
    for hh in range(DN_HEADS):
        o_parts = [o_chunk[hh, s, ci] for s in range(nseq) for ci in chunks]
        o_h = jnp.concatenate(o_parts, axis=0) if len(o_parts) > 1 else o_parts[0]
        o_heads.append(_rms(o_h, on_ref[...]) * st["gact"][tr, hh * HEAD_DIM:(hh + 1) * HEAD_DIM])

    mixed = jnp.concatenate(o_heads + [st["z"][tr, :]], axis=1).astype(BF16)
    x = x_ref[:, r0:r0 + hl, :].reshape(tile, x_ref.shape[-1])
    y = x + jnp.dot(mixed, wout_ref[...], preferred_element_type=F32)
    y_ref[:, r0:r0 + hl, :] = y.reshape(nseq, hl, y.shape[-1])


def _interleave(*gens):
    active = list(gens)
    while active:
        for gen in list(active):
            try:
                next(gen)
            except StopIteration:
                active.remove(gen)


def _stage_buffers(tile):
    return {
        "qb": pltpu.VMEM((tile, DN_WIDTH), BF16), "kb": pltpu.VMEM((tile, DN_WIDTH), BF16),
        "vb": pltpu.VMEM((tile, DN_WIDTH), BF16), "kdt": pltpu.VMEM((DN_HEADS, HEAD_DIM, tile), BF16),
        "gact": pltpu.VMEM((tile, DN_WIDTH), F32), "z": pltpu.VMEM((tile, POOL_WIDTH), F32),
        "colf": pltpu.VMEM((tile, LANES), F32), "rowf": pltpu.VMEM((N_ROW_FACTORS * SUBLANES, tile), F32),
    }


def _mixer_kernel(x_ref, nw_ref, win_ref, wab_ref, sel_ref, cum_ref, cw_ref, alog_ref, dtb_ref, on_ref,
                  wp_ref, ps_ref, wout_ref, s0_ref, c0_ref, p0_ref, sprev_ref, cprev_ref, pprev_ref,
                  y_ref, sn_ref, cn_ref, pn_ref,
                  s_scr, xpad, upad, qkv_s, wu_s, *stage_refs, nseq, ls, sub, chunk, pos0):
    del sprev_ref, cprev_ref, pprev_ref
    step = pl.program_id(1)
    st = dict(zip(_stage_buffers(nseq * ls), stage_refs))

    @pl.when(step == 0)
    def _():
        s_scr[...] = s0_ref[...]
        for sl in range(CONV_CH // LANES):
            xpad[:, sl, CONV_PAD - (CONV_WIDTH - 1):CONV_PAD, :] = c0_ref[:, :, sl * LANES:(sl + 1) * LANES]
        upad[:, POOL_PAD - POOL_HIST:POOL_PAD, :] = p0_ref[...]
        wu_s[...] = win_ref[:, COL_POOL_RAW:COL_POOL_RAW + POOL_WIDTH]

    stage1 = lambda r0: _mixer_stage1(step, r0, sub, x_ref, nw_ref, win_ref, wu_s, wab_ref, sel_ref, cum_ref, cw_ref,
                                      alog_ref, dtb_ref, wp_ref, ps_ref, xpad, upad, qkv_s, st,
                                      nseq=nseq, ls=ls, pos0=pos0)
    stage2 = lambda r0: _mixer_stage2(r0, sub, states, x_ref, on_ref, wout_ref, y_ref, qkv_s, st,
                                      nseq=nseq, ls=ls, chunk=chunk)
    states = {(hh, s): s_scr[s, hh] for hh in range(DN_HEADS) for s in range(nseq)}
    subs = list(range(0, ls, sub))
    _interleave(stage1(subs[0]))
    for prev, nxt in zip(subs, subs[1:]):
        _interleave(stage2(prev), stage1(nxt))
    _interleave(stage2(subs[-1]))
    for (hh, s), state in states.items():
        s_scr[s, hh] = state
    sn_ref[...] = s_scr[...]

    for sl in range(CONV_CH // LANES):
        tail = xpad[:, sl, CONV_PAD + ls - (CONV_WIDTH - 1):CONV_PAD + ls, :]
        xpad[:, sl, CONV_PAD - (CONV_WIDTH - 1):CONV_PAD, :] = tail
        cn_ref[:, :, sl * LANES:(sl + 1) * LANES] = tail
    pool_tail = upad[:, POOL_PAD + ls - POOL_HIST:POOL_PAD + ls, :]
    upad[:, POOL_PAD - POOL_HIST:POOL_PAD, :] = pool_tail
    pn_ref[...] = pool_tail


def _chunk_sum_matrix(tile, chunk):
    j = np.arange(tile)[:, None]
    i = np.arange(tile)[None, :]
    same = (j // chunk) == (i // chunk)
    return jnp.asarray(np.concatenate([same & (j <= i), same], axis=1), dtype=BF16)


def _head_select_matrix():
    sel = np.zeros((2 * SUBLANES, 2 * DN_WIDTH), np.float32)
    for hh in range(DN_HEADS):
        sel[hh, hh * HEAD_DIM:(hh + 1) * HEAD_DIM] = 1.0
        sel[SUBLANES + hh, DN_WIDTH + hh * HEAD_DIM:DN_WIDTH + (hh + 1) * HEAD_DIM] = 1.0
    return jnp.asarray(sel, dtype=BF16)


def _mixer(x, params, states, new_states, *, layer, state_layer, nseq, ls, sub, chunk, pos0):
    nw, w_in, w_ab_t, conv_w, alog, dtb, o_norm, w_pool, pool_scale, w_out = params
    b, l, d = x.shape
    assert sub & (sub - 1) == 0 and chunk & (chunk - 1) == 0 and ls % sub == 0 and sub % chunk == 0
    assert b % nseq == 0 and l % ls == 0 and (nseq == 1 or l == ls == sub)
    tile = nseq * ls
    sel = _head_select_matrix()
    cum = _chunk_sum_matrix(nseq * sub, chunk)
    stacked = lambda a: _resident(a.shape[1:], layer)
    seq_block = lambda a, which: pl.BlockSpec((None, nseq) + a.shape[2:],
                                              lambda i, j: (which, i) + (0,) * (a.ndim - 2))
    x_spec = pl.BlockSpec((nseq, ls, d), lambda i, j: (i, j, 0))
    untouched = pl.BlockSpec(memory_space=pl.ANY)
    n_in = 16
    return pl.pallas_call(
        functools.partial(_mixer_kernel, nseq=nseq, ls=ls, sub=sub, chunk=chunk, pos0=pos0),
        grid=(b // nseq, l // ls),
        in_specs=[x_spec, stacked(nw), stacked(w_in), stacked(w_ab_t), _resident(sel.shape), _resident(cum.shape),
                  stacked(conv_w), stacked(alog), stacked(dtb), stacked(o_norm), stacked(w_pool),
                  stacked(pool_scale), stacked(w_out),
                  *[seq_block(a, state_layer) for a in states], untouched, untouched, untouched],
        out_specs=(x_spec, *[seq_block(a, layer) for a in new_states]),
        out_shape=(jax.ShapeDtypeStruct(x.shape, x.dtype),
                   *[jax.ShapeDtypeStruct(a.shape, a.dtype) for a in new_states]),
        input_output_aliases={n_in + k: 1 + k for k in range(len(new_states))},
        scratch_shapes=[pltpu.VMEM((nseq, DN_HEADS, HEAD_DIM, HEAD_DIM), F32),
                        pltpu.VMEM((nseq, CONV_CH // LANES, CONV_PAD + ls, LANES), F32),
                        pltpu.VMEM((nseq, POOL_PAD + ls, POOL_WIDTH), F32),
                        pltpu.VMEM((CONV_CH // LANES, tile, LANES), F32),
                        pltpu.VMEM((d, POOL_WIDTH), BF16)] + list(_stage_buffers(tile).values()),
        compiler_params=pltpu.CompilerParams(dimension_semantics=("arbitrary", "arbitrary"),
                                             vmem_limit_bytes=VMEM_LIMIT_BYTES),
        name="mixer",
    )(x, nw, w_in, w_ab_t, sel, cum, conv_w, alog, dtb, o_norm, w_pool, pool_scale, w_out, *states, *new_states)


def _head_col(v):
    return jnp.pad(v.astype(F32), ((0, 0), (0, SUBLANES - DN_HEADS)))[:, :, None]


def kernel(x_prompt, x_sample, state_delta, state_conv, state_pool, norm_ffn1, w_ffn1_gate, w_ffn1_up, w_ffn1_down, norm_mix, w_in, conv_w, a_log, dt_bias, o_norm, w_pool, pool_scale, w_out, norm_ffn2, w_ffn2_gate, w_ffn2_up, w_ffn2_down, norm_final):
    depth = w_in.shape[0]
    batch, seq, d_model = x_prompt.shape
    dec_batch, dec_seq, _ = x_sample.shape

    w_in_b = w_in.astype(BF16)
    w_a_t = jnp.swapaxes(w_in[..., COL_POOL:COL_POOL + DN_HEADS], 1, 2)
    w_b_t = jnp.swapaxes(w_in[..., COL_POOL + DN_HEADS:COL_POOL_RAW], 1, 2)
    head_pad = jnp.zeros((depth, SUBLANES - DN_HEADS, d_model), w_in.dtype)
    w_ab_t = jnp.concatenate([w_a_t, head_pad, w_b_t, head_pad], axis=1).astype(BF16)
    row = lambda v: v[:, None, :]
    ffn1 = (row(norm_ffn1), w_ffn1_gate.astype(BF16), w_ffn1_up.astype(BF16), w_ffn1_down.astype(BF16))
    ffn2 = (row(norm_ffn2), w_ffn2_gate.astype(BF16), w_ffn2_up.astype(BF16), w_ffn2_down.astype(BF16))
    mix = (row(norm_mix), w_in_b, w_ab_t, conv_w, _head_col(a_log), _head_col(dt_bias), row(o_norm),
           w_pool.astype(BF16), row(pool_scale), w_out.astype(BF16))
    nf = norm_final[None, :]

    state_in = (state_delta, state_conv, state_pool)
    prompt = dict(nseq=1, ls=512, sub=512, chunk=PROMPT_CHUNK, pos0=0, state_layer=lambda layer: 0,
                  states=tuple(jnp.zeros((1, batch) + s.shape[2:], s.dtype) for s in state_in))
    sample = dict(nseq=dec_batch, ls=dec_seq, sub=dec_seq, chunk=dec_seq, pos0=PAST_LEN, state_layer=lambda layer: layer,
                  states=state_in)
    xp, xs = x_prompt.reshape(batch * seq, d_model), x_sample.reshape(dec_batch * dec_seq, d_model)
    new_p = tuple(jnp.zeros((depth, batch) + s.shape[2:], s.dtype) for s in state_in)
    new_s = tuple(jnp.zeros_like(s) for s in state_in)
    for layer in range(depth):
        xp, xs = _ffn(xp, xs, *ffn1, layer=layer, tm=1024)
        outs = []
        for grp, x, new in ((prompt, xp.reshape(x_prompt.shape), new_p), (sample, xs.reshape(x_sample.shape), new_s)):
            outs.append(_mixer(x, mix, grp["states"], new, layer=layer, state_layer=grp["state_layer"](layer),
                               nseq=grp["nseq"], ls=grp["ls"], sub=grp["sub"], chunk=grp["chunk"], pos0=grp["pos0"]))
        (xp, *new_p), (xs, *new_s) = outs
        fw = nf if layer == depth - 1 else None
        xp, xs = _ffn(xp.reshape(batch * seq, d_model), xs.reshape(dec_batch * dec_seq, d_model), *ffn2, fw,
                      layer=layer, tm=1024)
    return (xp.reshape(x_prompt.shape), xs.reshape(x_sample.shape), *new_p, *new_s)
```

```python
import functools

import numpy as np
import jax
import jax.numpy as jnp
from jax import lax
from jax.experimental import pallas as pl
from jax.experimental.pallas import tpu as pltpu

EPS = 1e-6
PAST_LEN = 4096
DN_HEADS = 4
HEAD_DIM = 128
DN_WIDTH = DN_HEADS * HEAD_DIM
CONV_WIDTH = 4
CONV_CH = 3 * DN_WIDTH
POOL_WINDOWS = (2, 4, 8, 16)
POOL_GROUP_DIM = 128
POOL_WIDTH = len(POOL_WINDOWS) * POOL_GROUP_DIM
POOL_HIST = max(POOL_WINDOWS) - 1
PROMPT_CHUNK = 64
LANES = 128
SUBLANES = 8
INV_BASE = 16
ROW_STRIDE = 4
MIN_STRIDED_ROWS = ROW_STRIDE * SUBLANES
COL_GATE = CONV_CH
COL_POOL = COL_GATE + DN_WIDTH
COL_POOL_RAW = COL_POOL + 2 * DN_HEADS
CONV_PAD = SUBLANES
POOL_PAD = 2 * SUBLANES
ROW_N, ROW_ETOT, ROW_BETA, ROW_W = (SUBLANES * i for i in range(4))
N_ROW_FACTORS = 4
COL_N, COL_Q, COL_QDEC = (SUBLANES * i for i in range(3))
VMEM_LIMIT_BYTES = 56 * 1024 * 1024

F32 = jnp.float32
BF16 = jnp.bfloat16
NT_DIMS = (((1,), (1,)), ((), ()))


def _rms(x, w):
    return x * lax.rsqrt(jnp.mean(x * x, axis=-1, keepdims=True) + EPS) * w


def _mm(a, b):
    return jnp.dot(a.astype(BF16), b.astype(BF16), preferred_element_type=F32)


def _resident(shape, layer=None):
    zeros = (0,) * len(shape)
    if layer is None:
        return pl.BlockSpec(shape, lambda *_: zeros, pipeline_mode=pl.Buffered(1))
    return pl.BlockSpec((None,) + shape, lambda *_: (layer,) + zeros, pipeline_mode=pl.Buffered(1))


def _ffn_kernel(*refs, f_chunk, n_main, final_norm):
    if final_norm:
        x_ref, xe_ref, nw_ref, wg_ref, wu_ref, wd_ref, fw_ref, o_ref, oe_ref, h_ref = refs
    else:
        x_ref, xe_ref, nw_ref, wg_ref, wu_ref, wd_ref, o_ref, oe_ref, h_ref = refs
    d_ff = wg_ref.shape[1]

    def tile(src_ref, dst_ref):
        rows = src_ref.shape[0]
        x = src_ref[...]
        xn = _rms(x, nw_ref[...]).astype(BF16)
        for c in range(d_ff // f_chunk):
            sl = slice(c * f_chunk, (c + 1) * f_chunk)
            g = jnp.dot(xn, wg_ref[:, sl], preferred_element_type=F32)
            u = jnp.dot(xn, wu_ref[:, sl], preferred_element_type=F32)
            h_ref[:rows, sl] = (g * jax.nn.sigmoid(g) * u).astype(BF16)
        y = x + 0.5 * jnp.dot(h_ref[:rows, :], wd_ref[...], preferred_element_type=F32)
        if final_norm:
            y = _rms(y, fw_ref[...])
        dst_ref[...] = y

    step = pl.program_id(0)
    pl.when(step < n_main)(lambda: tile(x_ref, o_ref))
    pl.when(step == n_main)(lambda: tile(xe_ref, oe_ref))


def _ffn(x, x_extra, nw, wg, wu, wd, fw=None, *, layer, tm):
    m, d = x.shape
    me = x_extra.shape[0]
    d_ff = wg.shape[-1]
    n_main = m // tm
    assert m % tm == 0 and me <= tm
    final_norm = fw is not None
    args = [x, x_extra, nw, wg, wu, wd] + ([fw] if final_norm else [])
    main_tile = lambda i: (jnp.minimum(i, n_main - 1), 0)
    in_specs = [pl.BlockSpec((tm, d), main_tile), pl.BlockSpec((me, d), lambda i: (0, 0)),
                _resident((1, d), layer), _resident((d, d_ff), layer), _resident((d, d_ff), layer),
                _resident((d_ff, d), layer)]
    in_specs += [_resident((1, d))] if final_norm else []
    return pl.pallas_call(
        functools.partial(_ffn_kernel, f_chunk=256, n_main=n_main, final_norm=final_norm),
        grid=(n_main + 1,),
        in_specs=in_specs,
        out_specs=(pl.BlockSpec((tm, d), main_tile), pl.BlockSpec((me, d), lambda i: (0, 0))),
        out_shape=(jax.ShapeDtypeStruct((m, d), x.dtype), jax.ShapeDtypeStruct((me, d), x.dtype)),
        scratch_shapes=[pltpu.VMEM((tm, d_ff), BF16)],
        compiler_params=pltpu.CompilerParams(dimension_semantics=("arbitrary",),
                                             vmem_limit_bytes=VMEM_LIMIT_BYTES),
        name="ffn",
    )(*args)


def _unit_lower_inverses(ns, chunk):
    r = lax.broadcasted_iota(jnp.int32, (chunk, chunk), 0)
    c = lax.broadcasted_iota(jnp.int32, (chunk, chunk), 1)
    base = min(chunk, INV_BASE)
    shift = base.bit_length() - 1
    if chunk > base:
        diag = (r >> shift) == (c >> shift)
        nds = [jnp.where(diag, n, 0.0) for n in ns]
    else:
        nds = ns
    eye = jnp.where(r == c, 1.0, 0.0)
    invs = [eye - nd for nd in nds]
    pws = [nd.astype(BF16) for nd in nds]
    pws = [jnp.dot(pw, pw, preferred_element_type=F32).astype(BF16) for pw in pws]
    yield
    order = 2
    while 2 * order < base:
        both = [jnp.dot(jnp.concatenate([inv.astype(BF16), pw], axis=0), pw, preferred_element_type=F32)
                for inv, pw in zip(invs, pws)]
        invs = [inv + bo[:chunk] for inv, bo in zip(invs, both)]
        pws = [bo[chunk:].astype(BF16) for bo in both]
        order *= 2
        yield
    invs = [inv + _mm(inv, pw) for inv, pw in zip(invs, pws)]
    yield
    blk = base
    while blk < chunk:
        s = blk.bit_length() - 1
        lower = ((r >> (s + 1)) == (c >> (s + 1))) & ((r >> s) > (c >> s))
        tmp = [_mm(inv, jnp.where(lower, n, 0.0)) for inv, n in zip(invs, ns)]
        yield
        invs = [inv - _mm(t, inv) for inv, t in zip(invs, tmp)]
        yield
        blk *= 2
    return invs


def _mixer_stage1(step, r0, hl, x_ref, nw_ref, win_ref, wu_s, wab_ref, sel_ref, cum_ref, cw_ref, alog_ref, dtb_ref,
                  wp_ref, ps_ref, xpad, upad, qkv_s, st, *, nseq, ls, pos0):
    tile = nseq * hl
    tr = slice(nseq * r0, nseq * r0 + tile)
    x = x_ref[:, r0:r0 + hl, :].reshape(tile, x_ref.shape[-1])
    h = _rms(x, nw_ref[...]).astype(BF16)
    project = lambda c0, c1: jnp.dot(h, win_ref[:, c0:c1], preferred_element_type=F32)
    yield

    for sl in range(CONV_CH // LANES):
        lanes = slice(sl * LANES, (sl + 1) * LANES)
        if sl % DN_HEADS == 0:
            if sl:
                yield
            pre = project(sl * LANES, (sl + DN_HEADS) * LANES)
        xpad[:, sl, CONV_PAD + r0:CONV_PAD + r0 + hl, :] = (
            pre[:, (sl % DN_HEADS) * LANES:(sl % DN_HEADS + 1) * LANES].reshape(nseq, hl, LANES))
        cw = cw_ref[:, lanes]
        if hl >= MIN_STRIDED_ROWS:
            for s in range(nseq):
                for r in range(ROW_STRIDE):
                    conv = None
                    for j in range(CONV_WIDTH):
                        rows_j = pl.ds(CONV_PAD + r0 - (CONV_WIDTH - 1) + j + r, hl // ROW_STRIDE, stride=ROW_STRIDE)
                        term = xpad[s, sl, rows_j, :] * cw[j:j + 1]
                        conv = term if conv is None else conv + term
                    rows_out = pl.ds(nseq * r0 + s * hl + r, hl // ROW_STRIDE, stride=ROW_STRIDE)
                    qkv_s[sl, rows_out, :] = conv * jax.nn.sigmoid(conv)
        else:
            conv = None
            for j in range(CONV_WIDTH):
                start = CONV_PAD + r0 - (CONV_WIDTH - 1) + j
                term = xpad[:, sl, start:start + hl, :].reshape(tile, LANES) * cw[j:j + 1]
                conv = term if conv is None else conv + term
            qkv_s[sl, tr, :] = conv * jax.nn.sigmoid(conv)
    yield
    gate = project(COL_GATE, COL_POOL)
    st["gact"][tr, :] = gate * jax.nn.sigmoid(gate)
    yield
    u = jnp.dot(h, wu_s[...], preferred_element_type=F32)
    upad[:, POOL_PAD + r0:POOL_PAD + r0 + hl, :] = u.reshape(nseq, hl, POOL_WIDTH)
    yield
    for name, part in (("qb", 0), ("kb", 1), ("vb", 2)):
        for hh in range(DN_HEADS):
            st[name][tr, hh * HEAD_DIM:(hh + 1) * HEAD_DIM] = qkv_s[part * DN_HEADS + hh, tr, :].astype(BF16)

    ab_t = lax.dot_general(wab_ref[...], h, NT_DIMS, preferred_element_type=F32)
    qk_part = jnp.concatenate([qkv_s[sl, tr, :] for sl in range(2 * DN_HEADS)], axis=1)
    ss_t = lax.dot_general(sel_ref[...], (qk_part * qk_part).astype(BF16), NT_DIMS,
                           preferred_element_type=F32)
    yield
    a_pre = ab_t[:SUBLANES] + dtb_ref[...]
    softplus = jnp.maximum(a_pre, 0.0) + jnp.log1p(jnp.exp(-jnp.abs(a_pre)))
    g = -jnp.exp(alog_ref[...]) * softplus
    beta = jax.nn.sigmoid(ab_t[SUBLANES:])
    r_q = lax.rsqrt(ss_t[:SUBLANES] + EPS) * (HEAD_DIM ** -0.5)
    r_k = lax.rsqrt(ss_t[SUBLANES:] + EPS)
    g_hi = g.astype(BF16).astype(F32)
    g_mid = (g - g_hi).astype(BF16).astype(F32)
    g_lo = g - g_hi - g_mid
    pieces = jnp.concatenate([g_hi, g_mid, g_lo, jnp.zeros_like(g)], axis=0).astype(BF16)
    sums = jnp.dot(pieces, cum_ref[...], preferred_element_type=F32)
    part = lambda k, half: sums[k * SUBLANES:(k + 1) * SUBLANES, half * tile:(half + 1) * tile]
    gc = part(0, 0) + part(1, 0) + part(2, 0)
    gs = part(0, 1) + part(1, 1) + part(2, 1)
    e_gc = jnp.exp(gc)
    row_n = gc - jnp.log(r_k)
    st["rowf"][:, tr] = jnp.concatenate([row_n, jnp.exp(gs), beta, r_k * (beta * e_gc)], axis=0)
    cols = [gc + jnp.log(beta * r_k), gc + jnp.log(r_q), r_q * e_gc]
    st["colf"][tr, :] = jnp.concatenate(cols + [jnp.zeros((LANES - SUBLANES * len(cols), tile), F32)], axis=0).T
    yield
    k_dec_row = r_k * jnp.exp(gs - gc)
    for hh in range(DN_HEADS):
        st["kdt"][hh, :, tr] = (qkv_s[DN_HEADS + hh, tr, :].T * k_dec_row[hh:hh + 1, :]).astype(BF16)
    yield

    row = lax.broadcasted_iota(jnp.int32, (tile, 1), 0)
    pos = pos0 + step * ls + r0 + (row & (hl - 1))
    for gi, win in enumerate(POOL_WINDOWS):
        lanes = slice(gi * POOL_GROUP_DIM, (gi + 1) * POOL_GROUP_DIM)
        acc = u[:, lanes]
        for back in range(1, win):
            start = POOL_PAD + r0 - back
            acc = acc + upad[:, start:start + hl, lanes].reshape(tile, POOL_GROUP_DIM)
        cnt = jnp.minimum(win, pos + 1).astype(F32)
        st["z"][tr, lanes] = _mm(acc / cnt - u[:, lanes], wp_ref[gi]) * ps_ref[:, lanes]
        yield


def _mixer_stage2(r0, hl, states, x_ref, on_ref, wout_ref, y_ref, qkv_s, st, *, nseq, ls, chunk):
    tile = nseq * hl
    tr = slice(nseq * r0, nseq * r0 + tile)
    chunks = range(r0 // chunk, (r0 + hl) // chunk)
    cr = lax.broadcasted_iota(jnp.int32, (chunk, chunk), 0)
    cc = lax.broadcasted_iota(jnp.int32, (chunk, chunk), 1)
    incl = cr >= cc
    strict = cr > cc

    probs = [(hh, s, ci) for hh in range(DN_HEADS) for s in range(nseq) for ci in chunks]
    rows = lambda pr: slice(pr[1] * ls + pr[2] * chunk, pr[1] * ls + (pr[2] + 1) * chunk)
    head = lambda pr: slice(pr[0] * HEAD_DIM, (pr[0] + 1) * HEAD_DIM)
    col = lambda base, pr: st["colf"][rows(pr), base + pr[0]:base + pr[0] + 1]
    row = lambda base, pr: st["rowf"][base + pr[0]:base + pr[0] + 1, rows(pr)]
    kq = [lax.dot_general(jnp.concatenate([st["kb"][rows(pr), head(pr)], st["qb"][rows(pr), head(pr)]], axis=0),
                          st["kb"][rows(pr), head(pr)], NT_DIMS, preferred_element_type=F32) for pr in probs]
    yield
    ns = [jnp.where(strict, kq_p[:chunk] * jnp.exp(jnp.where(strict, col(COL_N, pr) - row(ROW_N, pr), -1e30)), 0.0)
          for pr, kq_p in zip(probs, kq)]
    qkd = [kq_p[chunk:] * jnp.exp(jnp.where(incl, col(COL_Q, pr) - row(ROW_N, pr), -1e30))
           for pr, kq_p in zip(probs, kq)]
    yield
    a_inv = yield from _unit_lower_inverses(ns, chunk)
    sol = [jnp.concatenate([_mm(inv_p * row(ROW_BETA, pr), st["vb"][rows(pr), head(pr)]),
                            _mm(inv_p * row(ROW_W, pr), st["kb"][rows(pr), head(pr)])], axis=1).astype(BF16)
           for pr, inv_p in zip(probs, a_inv)]
    yield
    qw = [_mm(qkd_p, sol_p) for qkd_p, sol_p in zip(qkd, sol)]
    yield
    kw = [jnp.dot(st["kdt"][pr[0], :, rows(pr)], sol_p, preferred_element_type=F32) for pr, sol_p in zip(probs, sol)]
    yield

    lhs = {pr: jnp.concatenate([qkv_s[pr[0], rows(pr), :] * col(COL_QDEC, pr) - qw_p[:, HEAD_DIM:],
                                kw_p[:, HEAD_DIM:]], axis=0).astype(BF16)
           for pr, qw_p, kw_p in zip(probs, qw, kw)}
    o_add = {pr: qw_p[:, :HEAD_DIM] for pr, qw_p in zip(probs, qw)}
    s_add = {pr: kw_p[:, :HEAD_DIM] for pr, kw_p in zip(probs, kw)}
    o_chunk = {}
    yield
    for ci in chunks:
        for hh in range(DN_HEADS):
            for s in range(nseq):
                pr = (hh, s, ci)
                both = jnp.dot(lhs[pr], states[hh, s].astype(BF16), preferred_element_type=F32)
                o_chunk[pr] = o_add[pr] + both[:chunk]
                first = s * ls + ci * chunk
                e_chunk = st["rowf"][ROW_ETOT + hh:ROW_ETOT + hh + 1, first:first + 1]
                states[hh, s] = states[hh, s] * e_chunk - both[chunk:] + s_add[pr]
        yield
    o_heads = []
    for hh in range(DN_HEADS):
        o_parts = [o_chunk[hh, s, ci] for s in range(nseq) for ci in chunks]
        o_h = jnp.concatenate(o_parts, axis=0) if len(o_parts) > 1 else o_parts[0]
        o_heads.append(_rms(o_h, on_ref[...]) * st["gact"][tr, hh * HEAD_DIM:(hh + 1) * HEAD_DIM])

    mixed = jnp.concatenate(o_heads + [st["z"][tr, :]], axis=1).astype(BF16)
    x = x_ref[:, r0:r0 + hl, :].reshape(tile, x_ref.shape[-1])
    y = x + jnp.dot(mixed, wout_ref[...], preferred_element_type=F32)
    y_ref[:, r0:r0 + hl, :] = y.reshape(nseq, hl, y.shape[-1])


def _interleave(*gens):
    active = list(gens)
    while active:
        for gen in list(active):
            try:
                next(gen)
            except StopIteration:
                active.remove(gen)


def _stage_buffers(tile):
    return {
        "qb": pltpu.VMEM((tile, DN_WIDTH), BF16), "kb": pltpu.VMEM((tile, DN_WIDTH), BF16),
        "vb": pltpu.VMEM((tile, DN_WIDTH), BF16), "kdt": pltpu.VMEM((DN_HEADS, HEAD_DIM, tile), BF16),
        "gact": pltpu.VMEM((tile, DN_WIDTH), F32), "z": pltpu.VMEM((tile, POOL_WIDTH), F32),
        "colf": pltpu.VMEM((tile, LANES), F32), "rowf": pltpu.VMEM((N_ROW_FACTORS * SUBLANES, tile), F32),
    }


def _mixer_kernel(x_ref, nw_ref, win_ref, wab_ref, sel_ref, cum_ref, cw_ref, alog_ref, dtb_ref, on_ref,
                  wp_ref, ps_ref, wout_ref, s0_ref, c0_ref, p0_ref, sprev_ref, cprev_ref, pprev_ref,
                  y_ref, sn_ref, cn_ref, pn_ref,
                  s_scr, xpad, upad, qkv_s, wu_s, *stage_refs, nseq, ls, sub, chunk, pos0):
    del sprev_ref, cprev_ref, pprev_ref
    step = pl.program_id(1)
    st = dict(zip(_stage_buffers(nseq * ls), stage_refs))

    @pl.when(step == 0)
    def _():
        s_scr[...] = s0_ref[...]
        for sl in range(CONV_CH // LANES):
            xpad[:, sl, CONV_PAD - (CONV_WIDTH - 1):CONV_PAD, :] = c0_ref[:, :, sl * LANES:(sl + 1) * LANES]
        upad[:, POOL_PAD - POOL_HIST:POOL_PAD, :] = p0_ref[...]
        wu_s[...] = win_ref[:, COL_POOL_RAW:COL_POOL_RAW + POOL_WIDTH]

    stage1 = lambda r0: _mixer_stage1(step, r0, sub, x_ref, nw_ref, win_ref, wu_s, wab_ref, sel_ref, cum_ref, cw_ref,
                                      alog_ref, dtb_ref, wp_ref, ps_ref, xpad, upad, qkv_s, st,
                                      nseq=nseq, ls=ls, pos0=pos0)
    stage2 = lambda r0: _mixer_stage2(r0, sub, states, x_ref, on_ref, wout_ref, y_ref, qkv_s, st,
                                      nseq=nseq, ls=ls, chunk=chunk)
    states = {(hh, s): s_scr[s, hh] for hh in range(DN_HEADS) for s in range(nseq)}
    subs = list(range(0, ls, sub))
    _interleave(stage1(subs[0]))
    for prev, nxt in zip(subs, subs[1:]):
        _interleave(stage2(prev), stage1(nxt))
    _interleave(stage2(subs[-1]))
    for (hh, s), state in states.items():
        s_scr[s, hh] = state
    sn_ref[...] = s_scr[...]

    for sl in range(CONV_CH // LANES):
        tail = xpad[:, sl, CONV_PAD + ls - (CONV_WIDTH - 1):CONV_PAD + ls, :]
        xpad[:, sl, CONV_PAD - (CONV_WIDTH - 1):CONV_PAD, :] = tail
        cn_ref[:, :, sl * LANES:(sl + 1) * LANES] = tail
    pool_tail = upad[:, POOL_PAD + ls - POOL_HIST:POOL_PAD + ls, :]
    upad[:, POOL_PAD - POOL_HIST:POOL_PAD, :] = pool_tail
    pn_ref[...] = pool_tail


def _chunk_sum_matrix(tile, chunk):
    j = np.arange(tile)[:, None]
    i = np.arange(tile)[None, :]
    same = (j // chunk) == (i // chunk)
    return jnp.asarray(np.concatenate([same & (j <= i), same], axis=1), dtype=BF16)


def _head_select_matrix():
    sel = np.zeros((2 * SUBLANES, 2 * DN_WIDTH), np.float32)
    for hh in range(DN_HEADS):
        sel[hh, hh * HEAD_DIM:(hh + 1) * HEAD_DIM] = 1.0
        sel[SUBLANES + hh, DN_WIDTH + hh * HEAD_DIM:DN_WIDTH + (hh + 1) * HEAD_DIM] = 1.0
    return jnp.asarray(sel, dtype=BF16)


def _mixer(x, params, states, new_states, *, layer, state_layer, nseq, ls, sub, chunk, pos0):
    nw, w_in, w_ab_t, conv_w, alog, dtb, o_norm, w_pool, pool_scale, w_out = params
    b, l, d = x.shape
    assert sub & (sub - 1) == 0 and chunk & (chunk - 1) == 0 and ls % sub == 0 and sub % chunk == 0
    assert b % nseq == 0 and l % ls == 0 and (nseq == 1 or ls == sub)
    tile = nseq * ls
    sel = _head_select_matrix()
    cum = _chunk_sum_matrix(nseq * sub, chunk)
    stacked = lambda a: _resident(a.shape[1:], layer)
    seq_block = lambda a, which: pl.BlockSpec((None, nseq) + a.shape[2:],
                                              lambda i, j: (which, i) + (0,) * (a.ndim - 2))
    x_spec = pl.BlockSpec((nseq, ls, d), lambda i, j: (i, j, 0))
    untouched = pl.BlockSpec(memory_space=pl.ANY)
    n_in = 16
    return pl.pallas_call(
        functools.partial(_mixer_kernel, nseq=nseq, ls=ls, sub=sub, chunk=chunk, pos0=pos0),
        grid=(b // nseq, l // ls),
        in_specs=[x_spec, stacked(nw), stacked(w_in), stacked(w_ab_t), _resident(sel.shape), _resident(cum.shape),
                  stacked(conv_w), stacked(alog), stacked(dtb), stacked(o_norm), stacked(w_pool),
                  stacked(pool_scale), stacked(w_out),
                  *[seq_block(a, state_layer) for a in states], untouched, untouched, untouched],
        out_specs=(x_spec, *[seq_block(a, layer) for a in new_states]),
        out_shape=(jax.ShapeDtypeStruct(x.shape, x.dtype),
                   *[jax.ShapeDtypeStruct(a.shape, a.dtype) for a in new_states]),
        input_output_aliases={n_in + k: 1 + k for k in range(len(new_states))},
        scratch_shapes=[pltpu.VMEM((nseq, DN_HEADS, HEAD_DIM, HEAD_DIM), F32),
                        pltpu.VMEM((nseq, CONV_CH // LANES, CONV_PAD + ls, LANES), F32),
                        pltpu.VMEM((nseq, POOL_PAD + ls, POOL_WIDTH), F32),
                        pltpu.VMEM((CONV_CH // LANES, tile, LANES), F32),
                        pltpu.VMEM((d, POOL_WIDTH), BF16)] + list(_stage_buffers(tile).values()),
        compiler_params=pltpu.CompilerParams(dimension_semantics=("arbitrary", "arbitrary"),
                                             vmem_limit_bytes=VMEM_LIMIT_BYTES),
        name="mixer",
    )(x, nw, w_in, w_ab_t, sel, cum, conv_w, alog, dtb, o_norm, w_pool, pool_scale, w_out, *states, *new_states)


def _head_col(v):
    return jnp.pad(v.astype(F32), ((0, 0), (0, SUBLANES - DN_HEADS)))[:, :, None]


def kernel(x_prompt, x_sample, state_delta, state_conv, state_pool, norm_ffn1, w_ffn1_gate, w_ffn1_up, w_ffn1_down, norm_mix, w_in, conv_w, a_log, dt_bias, o_norm, w_pool, pool_scale, w_out, norm_ffn2, w_ffn2_gate, w_ffn2_up, w_ffn2_down, norm_final):
    depth = w_in.shape[0]
    batch, seq, d_model = x_prompt.shape
    dec_batch, dec_seq, _ = x_sample.shape

    w_in_b = w_in.astype(BF16)
    w_a_t = jnp.swapaxes(w_in[..., COL_POOL:COL_POOL + DN_HEADS], 1, 2)
    w_b_t = jnp.swapaxes(w_in[..., COL_POOL + DN_HEADS:COL_POOL_RAW], 1, 2)
    head_pad = jnp.zeros((depth, SUBLANES - DN_HEADS, d_model), w_in.dtype)
    w_ab_t = jnp.concatenate([w_a_t, head_pad, w_b_t, head_pad], axis=1).astype(BF16)
    row = lambda v: v[:, None, :]
    ffn1 = (row(norm_ffn1), w_ffn1_gate.astype(BF16), w_ffn1_up.astype(BF16), w_ffn1_down.astype(BF16))
    ffn2 = (row(norm_ffn2), w_ffn2_gate.astype(BF16), w_ffn2_up.astype(BF16), w_ffn2_down.astype(BF16))
    mix = (row(norm_mix), w_in_b, w_ab_t, conv_w, _head_col(a_log), _head_col(dt_bias), row(o_norm),
           w_pool.astype(BF16), row(pool_scale), w_out.astype(BF16))
    nf = norm_final[None, :]

    state_in = (state_delta, state_conv, state_pool)
    prompt = dict(nseq=batch, ls=256, sub=256, chunk=PROMPT_CHUNK, pos0=0, state_layer=lambda layer: 0,
                  states=tuple(jnp.zeros((1, batch) + s.shape[2:], s.dtype) for s in state_in))
    sample = dict(nseq=dec_batch, ls=dec_seq, sub=dec_seq, chunk=dec_seq, pos0=PAST_LEN, state_layer=lambda layer: layer,
                  states=state_in)
    xp, xs = x_prompt.reshape(batch * seq, d_model), x_sample.reshape(dec_batch * dec_seq, d_model)
    new_p = tuple(jnp.zeros((depth, batch) + s.shape[2:], s.dtype) for s in state_in)
    new_s = tuple(jnp.zeros_like(s) for s in state_in)
    for layer in range(depth):
        xp, xs = _ffn(xp, xs, *ffn1, layer=layer, tm=1024)
        outs = []
        for grp, x, new in ((prompt, xp.reshape(x_prompt.shape), new_p), (sample, xs.reshape(x_sample.shape), new_s)):
            outs.append(_mixer(x, mix, grp["states"], new, layer=layer, state_layer=grp["state_layer"](layer),
                               nseq=grp["nseq"], ls=grp["ls"], sub=grp["sub"], chunk=grp["chunk"], pos0=grp["pos0"]))
        (xp, *new_p), (xs, *new_s) = outs
        fw = nf if layer == depth - 1 else None
        xp, xs = _ffn(xp.reshape(batch * seq, d_model), xs.reshape(dec_batch * dec_seq, d_model), *ffn2, fw,
                      layer=layer, tm=1024)
    return (xp.reshape(x_prompt.shape), xs.reshape(x_sample.shape), *new_p, *new_s)
```

```python
import functools

import numpy as np
import jax
import jax.numpy as jnp
from jax import lax
from jax.experimental import pallas as pl
from jax.experimental.pallas import tpu as pltpu

EPS = 1e-6
PAST_LEN = 4096
DN_HEADS = 4
HEAD_DIM = 128
DN_WIDTH = DN_HEADS * HEAD_DIM
CONV_WIDTH = 4
CONV_CH = 3 * DN_WIDTH
POOL_WINDOWS = (2, 4, 8, 16)
POOL_GROUP_DIM = 128
POOL_WIDTH = len(POOL_WINDOWS) * POOL_GROUP_DIM
POOL_HIST = max(POOL_WINDOWS) - 1
PROMPT_CHUNK = 64
LANES = 128
SUBLANES = 8
INV_BASE = 16
ROW_STRIDE = 4
MIN_STRIDED_ROWS = ROW_STRIDE * SUBLANES
COL_GATE = CONV_CH
COL_POOL = COL_GATE + DN_WIDTH
COL_POOL_RAW = COL_POOL + 2 * DN_HEADS
CONV_PAD = SUBLANES
POOL_PAD = 2 * SUBLANES
ROW_N, ROW_ETOT, ROW_BETA, ROW_W = (SUBLANES * i for i in range(4))
N_ROW_FACTORS = 4
COL_N, COL_Q, COL_QDEC = (SUBLANES * i for i in range(3))
VMEM_LIMIT_BYTES = 56 * 1024 * 1024

F32 = jnp.float32
BF16 = jnp.bfloat16
NT_DIMS = (((1,), (1,)), ((), ()))


def _rms(x, w):
    return x * lax.rsqrt(jnp.mean(x * x, axis=-1, keepdims=True) + EPS) * w


def _mm(a, b):
    return jnp.dot(a.astype(BF16), b.astype(BF16), preferred_element_type=F32)


def _resident(shape, layer=None):
    zeros = (0,) * len(shape)
    if layer is None:
        return pl.BlockSpec(shape, lambda *_: zeros, pipeline_mode=pl.Buffered(1))
    return pl.BlockSpec((None,) + shape, lambda *_: (layer,) + zeros, pipeline_mode=pl.Buffered(1))


def _ffn_kernel(*refs, f_chunk, n_main, final_norm):
    if final_norm:
        x_ref, xe_ref, nw_ref, wg_ref, wu_ref, wd_ref, fw_ref, o_ref, oe_ref, h_ref = refs
    else:
        x_ref, xe_ref, nw_ref, wg_ref, wu_ref, wd_ref, o_ref, oe_ref, h_ref = refs
    d_ff = wg_ref.shape[1]

    def tile(src_ref, dst_ref):
        rows = src_ref.shape[0]
        x = src_ref[...]
        xn = _rms(x, nw_ref[...]).astype(BF16)
        for c in range(d_ff // f_chunk):
            sl = slice(c * f_chunk, (c + 1) * f_chunk)
            g = jnp.dot(xn, wg_ref[:, sl], preferred_element_type=F32)
            u = jnp.dot(xn, wu_ref[:, sl], preferred_element_type=F32)
            h_ref[:rows, sl] = (g * jax.nn.sigmoid(g) * u).astype(BF16)
        y = x + 0.5 * jnp.dot(h_ref[:rows, :], wd_ref[...], preferred_element_type=F32)
        if final_norm:
            y = _rms(y, fw_ref[...])
        dst_ref[...] = y

    step = pl.program_id(0)
    pl.when(step < n_main)(lambda: tile(x_ref, o_ref))
    pl.when(step == n_main)(lambda: tile(xe_ref, oe_ref))


def _ffn(x, x_extra, nw, wg, wu, wd, fw=None, *, layer, tm):
    m, d = x.shape
    me = x_extra.shape[0]
    d_ff = wg.shape[-1]
    n_main = m // tm
    assert m % tm == 0 and me <= tm
    final_norm = fw is not None
    args = [x, x_extra, nw, wg, wu, wd] + ([fw] if final_norm else [])
    main_tile = lambda i: (jnp.minimum(i, n_main - 1), 0)
    in_specs = [pl.BlockSpec((tm, d), main_tile), pl.BlockSpec((me, d), lambda i: (0, 0)),
                _resident((1, d), layer), _resident((d, d_ff), layer), _resident((d, d_ff), layer),
                _resident((d_ff, d), layer)]
    in_specs += [_resident((1, d))] if final_norm else []
    return pl.pallas_call(
        functools.partial(_ffn_kernel, f_chunk=256, n_main=n_main, final_norm=final_norm),
        grid=(n_main + 1,),
        in_specs=in_specs,
        out_specs=(pl.BlockSpec((tm, d), main_tile), pl.BlockSpec((me, d), lambda i: (0, 0))),
        out_shape=(jax.ShapeDtypeStruct((m, d), x.dtype), jax.ShapeDtypeStruct((me, d), x.dtype)),
        scratch_shapes=[pltpu.VMEM((tm, d_ff), BF16)],
        compiler_params=pltpu.CompilerParams(dimension_semantics=("arbitrary",),
                                             vmem_limit_bytes=VMEM_LIMIT_BYTES),
        name="ffn",
    )(*args)


def _unit_lower_inverses(ns, chunk):
    r = lax.broadcasted_iota(jnp.int32, (chunk, chunk), 0)
    c = lax.broadcasted_iota(jnp.int32, (chunk, chunk), 1)
    base = min(chunk, INV_BASE)
    shift = base.bit_length() - 1
    if chunk > base:
        diag = (r >> shift) == (c >> shift)
        nds = [jnp.where(diag, n, 0.0) for n in ns]
    else:
        nds = ns
    eye = jnp.where(r == c, 1.0, 0.0)
    invs = [eye - nd for nd in nds]
    pws = [nd.astype(BF16) for nd in nds]
    pws = [jnp.dot(pw, pw, preferred_element_type=F32).astype(BF16) for pw in pws]
    yield
    order = 2
    while 2 * order < base:
        both = [jnp.dot(jnp.concatenate([inv.astype(BF16), pw], axis=0), pw, preferred_element_type=F32)
                for inv, pw in zip(invs, pws)]
        invs = [inv + bo[:chunk] for inv, bo in zip(invs, both)]
        pws = [bo[chunk:].astype(BF16) for bo in both]
        order *= 2
        yield
    invs = [inv + _mm(inv, pw) for inv, pw in zip(invs, pws)]
    yield
    blk = base
    while blk < chunk:
        s = blk.bit_length() - 1
        lower = ((r >> (s + 1)) == (c >> (s + 1))) & ((r >> s) > (c >> s))
        tmp = [_mm(inv, jnp.where(lower, n, 0.0)) for inv, n in zip(invs, ns)]
        yield
        invs = [inv - _mm(t, inv) for inv, t in zip(invs, tmp)]
        yield
        blk *= 2
    return invs


def _mixer_stage1(step, r0, hl, x_ref, nw_ref, win_ref, wu_s, wab_ref, sel_ref, cum_ref, cw_ref, alog_ref, dtb_ref,
                  wp_ref, ps_ref, xpad, upad, qkv_s, d_s, st, *, nseq, ls, pos0):
    tile = nseq * hl
    tr = slice(nseq * r0, nseq * r0 + tile)
    x = x_ref[:, r0:r0 + hl, :].reshape(tile, x_ref.shape[-1])
    h = _rms(x, nw_ref[...]).astype(BF16)
    project = lambda c0, c1: jnp.dot(h, win_ref[:, c0:c1], preferred_element_type=F32)
    yield

    for sl in range(CONV_CH // LANES):
        lanes = slice(sl * LANES, (sl + 1) * LANES)
        if sl % DN_HEADS == 0:
            if sl:
                yield
            pre = project(sl * LANES, (sl + DN_HEADS) * LANES)
        xpad[:, sl, CONV_PAD + r0:CONV_PAD + r0 + hl, :] = (
            pre[:, (sl % DN_HEADS) * LANES:(sl % DN_HEADS + 1) * LANES].reshape(nseq, hl, LANES))
        cw = cw_ref[:, lanes]
        if hl >= MIN_STRIDED_ROWS:
            for s in range(nseq):
                for r in range(ROW_STRIDE):
                    conv = None
                    for j in range(CONV_WIDTH):
                        rows_j = pl.ds(CONV_PAD + r0 - (CONV_WIDTH - 1) + j + r, hl // ROW_STRIDE, stride=ROW_STRIDE)
                        term = xpad[s, sl, rows_j, :] * cw[j:j + 1]
                        conv = term if conv is None else conv + term
                    rows_out = pl.ds(nseq * r0 + s * hl + r, hl // ROW_STRIDE, stride=ROW_STRIDE)
                    qkv_s[sl, rows_out, :] = conv * jax.nn.sigmoid(conv)
        else:
            conv = None
            for j in range(CONV_WIDTH):
                start = CONV_PAD + r0 - (CONV_WIDTH - 1) + j
                term = xpad[:, sl, start:start + hl, :].reshape(tile, LANES) * cw[j:j + 1]
                conv = term if conv is None else conv + term
            qkv_s[sl, tr, :] = conv * jax.nn.sigmoid(conv)
    yield
    gate = project(COL_GATE, COL_POOL)
    st["gact"][tr, :] = gate * jax.nn.sigmoid(gate)
    yield
    u = jnp.dot(h, wu_s[...], preferred_element_type=F32)
    for gi in range(len(POOL_WINDOWS)):
        upad[:, gi, POOL_PAD + r0:POOL_PAD + r0 + hl, :] = (
            u[:, gi * POOL_GROUP_DIM:(gi + 1) * POOL_GROUP_DIM].reshape(nseq, hl, POOL_GROUP_DIM))
    yield
    for name, part in (("qb", 0), ("kb", 1), ("vb", 2)):
        for hh in range(DN_HEADS):
            st[name][tr, hh * HEAD_DIM:(hh + 1) * HEAD_DIM] = qkv_s[part * DN_HEADS + hh, tr, :].astype(BF16)

    ab_t = lax.dot_general(wab_ref[...], h, NT_DIMS, preferred_element_type=F32)
    qk_part = jnp.concatenate([qkv_s[sl, tr, :] for sl in range(2 * DN_HEADS)], axis=1)
    ss_t = lax.dot_general(sel_ref[...], (qk_part * qk_part).astype(BF16), NT_DIMS,
                           preferred_element_type=F32)
    yield
    a_pre = ab_t[:SUBLANES] + dtb_ref[...]
    softplus = jnp.maximum(a_pre, 0.0) + jnp.log1p(jnp.exp(-jnp.abs(a_pre)))
    g = -jnp.exp(alog_ref[...]) * softplus
    beta = jax.nn.sigmoid(ab_t[SUBLANES:])
    r_q = lax.rsqrt(ss_t[:SUBLANES] + EPS) * (HEAD_DIM ** -0.5)
    r_k = lax.rsqrt(ss_t[SUBLANES:] + EPS)
    g_hi = g.astype(BF16).astype(F32)
    g_mid = (g - g_hi).astype(BF16).astype(F32)
    g_lo = g - g_hi - g_mid
    pieces = jnp.concatenate([g_hi, g_mid, g_lo, jnp.zeros_like(g)], axis=0).astype(BF16)
    sums = jnp.dot(pieces, cum_ref[...], preferred_element_type=F32)
    part = lambda k, half: sums[k * SUBLANES:(k + 1) * SUBLANES, half * tile:(half + 1) * tile]
    gc = part(0, 0) + part(1, 0) + part(2, 0)
    gs = part(0, 1) + part(1, 1) + part(2, 1)
    e_gc = jnp.exp(gc)
    row_n = gc - jnp.log(r_k)
    st["rowf"][:, tr] = jnp.concatenate([row_n, jnp.exp(gs), beta, r_k * (beta * e_gc)], axis=0)
    cols = [gc + jnp.log(beta * r_k), gc + jnp.log(r_q), r_q * e_gc]
    st["colf"][tr, :] = jnp.concatenate(cols + [jnp.zeros((LANES - SUBLANES * len(cols), tile), F32)], axis=0).T
    yield
    k_dec_row = r_k * jnp.exp(gs - gc)
    for hh in range(DN_HEADS):
        st["kdt"][hh, :, tr] = (qkv_s[DN_HEADS + hh, tr, :].T * k_dec_row[hh:hh + 1, :]).astype(BF16)
    yield

    first_pos = pos0 + step * ls + r0
    for gi, win in enumerate(POOL_WINDOWS):
        lanes = slice(gi * POOL_GROUP_DIM, (gi + 1) * POOL_GROUP_DIM)
        if hl >= MIN_STRIDED_ROWS:
            n_rows = hl // ROW_STRIDE
            for s in range(nseq):
                for r in range(ROW_STRIDE):
                    cur = upad[s, gi, pl.ds(POOL_PAD + r0 + r, n_rows, stride=ROW_STRIDE), :]
                    acc = cur
                    for back in range(1, win):
                        acc = acc + upad[s, gi, pl.ds(POOL_PAD + r0 + r - back, n_rows, stride=ROW_STRIDE), :]
                    pos = first_pos + r + ROW_STRIDE * lax.broadcasted_iota(jnp.int32, (n_rows, 1), 0)
                    cnt = jnp.minimum(win, pos + 1).astype(F32)
                    d_s[gi, pl.ds(nseq * r0 + s * hl + r, n_rows, stride=ROW_STRIDE), :] = acc / cnt - cur
            diff = d_s[gi, tr, :]
        else:
            acc = u[:, lanes]
            for back in range(1, win):
                start = POOL_PAD + r0 - back
                acc = acc + upad[:, gi, start:start + hl, :].reshape(tile, POOL_GROUP_DIM)
            pos = first_pos + (lax.broadcasted_iota(jnp.int32, (tile, 1), 0) & (hl - 1))
            cnt = jnp.minimum(win, pos + 1).astype(F32)
            diff = acc / cnt - u[:, lanes]
        st["z"][tr, lanes] = _mm(diff, wp_ref[gi]) * ps_ref[:, lanes]
        yield


def _mixer_stage2(r0, hl, states, x_ref, on_ref, wout_ref, y_ref, qkv_s, st, *, nseq, ls, chunk):
    tile = nseq * hl
    tr = slice(nseq * r0, nseq * r0 + tile)
    chunks = range(r0 // chunk, (r0 + hl) // chunk)
    cr = lax.broadcasted_iota(jnp.int32, (chunk, chunk), 0)
    cc = lax.broadcasted_iota(jnp.int32, (chunk, chunk), 1)
    incl = cr >= cc
    strict = cr > cc

    probs = [(hh, s, ci) for hh in range(DN_HEADS) for s in range(nseq) for ci in chunks]
    rows = lambda pr: slice(pr[1] * ls + pr[2] * chunk, pr[1] * ls + (pr[2] + 1) * chunk)
    head = lambda pr: slice(pr[0] * HEAD_DIM, (pr[0] + 1) * HEAD_DIM)
    col = lambda base, pr: st["colf"][rows(pr), base + pr[0]:base + pr[0] + 1]
    row = lambda base, pr: st["rowf"][base + pr[0]:base + pr[0] + 1, rows(pr)]
    kq = [lax.dot_general(jnp.concatenate([st["kb"][rows(pr), head(pr)], st["qb"][rows(pr), head(pr)]], axis=0),
                          st["kb"][rows(pr), head(pr)], NT_DIMS, preferred_element_type=F32) for pr in probs]
    yield
    ns = [jnp.where(strict, kq_p[:chunk] * jnp.exp(jnp.where(strict, col(COL_N, pr) - row(ROW_N, pr), -1e30)), 0.0)
          for pr, kq_p in zip(probs, kq)]
    qkd = [kq_p[chunk:] * jnp.exp(jnp.where(incl, col(COL_Q, pr) - row(ROW_N, pr), -1e30))
           for pr, kq_p in zip(probs, kq)]
    yield
    a_inv = yield from _unit_lower_inverses(ns, chunk)
    sol = [jnp.concatenate([_mm(inv_p * row(ROW_BETA, pr), st["vb"][rows(pr), head(pr)]),
                            _mm(inv_p * row(ROW_W, pr), st["kb"][rows(pr), head(pr)])], axis=1).astype(BF16)
           for pr, inv_p in zip(probs, a_inv)]
    yield
    qw = [_mm(qkd_p, sol_p) for qkd_p, sol_p in zip(qkd, sol)]
    yield
    kw = [jnp.dot(st["kdt"][pr[0], :, rows(pr)], sol_p, preferred_element_type=F32) for pr, sol_p in zip(probs, sol)]
    yield

    lhs = {pr: jnp.concatenate([qkv_s[pr[0], rows(pr), :] * col(COL_QDEC, pr) - qw_p[:, HEAD_DIM:],
                                kw_p[:, HEAD_DIM:]], axis=0).astype(BF16)
           for pr, qw_p, kw_p in zip(probs, qw, kw)}
    o_add = {pr: qw_p[:, :HEAD_DIM] for pr, qw_p in zip(probs, qw)}
    s_add = {pr: kw_p[:, :HEAD_DIM] for pr, kw_p in zip(probs, kw)}
    o_chunk = {}
    yield
    for ci in chunks:
        for hh in range(DN_HEADS):
            for s in range(nseq):
                pr = (hh, s, ci)
                both = jnp.dot(lhs[pr], states[hh, s].astype(BF16), preferred_element_type=F32)
                o_chunk[pr] = o_add[pr] + both[:chunk]
                first = s * ls + ci * chunk
                e_chunk = st["rowf"][ROW_ETOT + hh:ROW_ETOT + hh + 1, first:first + 1]
                states[hh, s] = states[hh, s] * e_chunk - both[chunk:] + s_add[pr]
        yield
    o_heads = []
    for hh in range(DN_HEADS):
        o_parts = [o_chunk[hh, s, ci] for s in range(nseq) for ci in chunks]
        o_h = jnp.concatenate(o_parts, axis=0) if len(o_parts) > 1 else o_parts[0]
        o_heads.append(_rms(o_h, on_ref[...]) * st["gact"][tr, hh * HEAD_DIM:(hh + 1) * HEAD_DIM])

    mixed = jnp.concatenate(o_heads + [st["z"][tr, :]], axis=1).astype(BF16)
    x = x_ref[:, r0:r0 + hl, :].reshape(tile, x_ref.shape[-1])
    y = x + jnp.dot(mixed, wout_ref[...], preferred_element_type=F32)
    y_ref[:, r0:r0 + hl, :] = y.reshape(nseq, hl, y.shape[-1])


def _interleave(*gens):
    active = list(gens)
    while active:
        for gen in list(active):
            try:
                next(gen)
            except StopIteration:
                active.remove(gen)


def _stage_buffers(tile):
    return {
        "qb": pltpu.VMEM((tile, DN_WIDTH), BF16), "kb": pltpu.VMEM((tile, DN_WIDTH), BF16),
        "vb": pltpu.VMEM((tile, DN_WIDTH), BF16), "kdt": pltpu.VMEM((DN_HEADS, HEAD_DIM, tile), BF16),
        "gact": pltpu.VMEM((tile, DN_WIDTH), F32), "z": pltpu.VMEM((tile, POOL_WIDTH), F32),
        "colf": pltpu.VMEM((tile, LANES), F32), "rowf": pltpu.VMEM((N_ROW_FACTORS * SUBLANES, tile), F32),
    }


def _mixer_kernel(x_ref, nw_ref, win_ref, wab_ref, sel_ref, cum_ref, cw_ref, alog_ref, dtb_ref, on_ref,
                  wp_ref, ps_ref, wout_ref, s0_ref, c0_ref, p0_ref, sprev_ref, cprev_ref, pprev_ref,
                  y_ref, sn_ref, cn_ref, pn_ref,
                  s_scr, xpad, upad, qkv_s, d_s, wu_s, *stage_refs, nseq, ls, sub, chunk, pos0):
    del sprev_ref, cprev_ref, pprev_ref
    step = pl.program_id(1)
    st = dict(zip(_stage_buffers(nseq * ls), stage_refs))

    @pl.when(step == 0)
    def _():
        s_scr[...] = s0_ref[...]
        for sl in range(CONV_CH // LANES):
            xpad[:, sl, CONV_PAD - (CONV_WIDTH - 1):CONV_PAD, :] = c0_ref[:, :, sl * LANES:(sl + 1) * LANES]
        for gi in range(len(POOL_WINDOWS)):
            upad[:, gi, POOL_PAD - POOL_HIST:POOL_PAD, :] = p0_ref[:, :, gi * POOL_GROUP_DIM:(gi + 1) * POOL_GROUP_DIM]
        wu_s[...] = win_ref[:, COL_POOL_RAW:COL_POOL_RAW + POOL_WIDTH]

    stage1 = lambda r0: _mixer_stage1(step, r0, sub, x_ref, nw_ref, win_ref, wu_s, wab_ref, sel_ref, cum_ref, cw_ref,
                                      alog_ref, dtb_ref, wp_ref, ps_ref, xpad, upad, qkv_s, d_s, st,
                                      nseq=nseq, ls=ls, pos0=pos0)
    stage2 = lambda r0: _mixer_stage2(r0, sub, states, x_ref, on_ref, wout_ref, y_ref, qkv_s, st,
                                      nseq=nseq, ls=ls, chunk=chunk)
    states = {(hh, s): s_scr[s, hh] for hh in range(DN_HEADS) for s in range(nseq)}
    subs = list(range(0, ls, sub))
    _interleave(stage1(subs[0]))
    for prev, nxt in zip(subs, subs[1:]):
        _interleave(stage2(prev), stage1(nxt))
    _interleave(stage2(subs[-1]))
    for (hh, s), state in states.items():
        s_scr[s, hh] = state
    sn_ref[...] = s_scr[...]

    for sl in range(CONV_CH // LANES):
        tail = xpad[:, sl, CONV_PAD + ls - (CONV_WIDTH - 1):CONV_PAD + ls, :]
        xpad[:, sl, CONV_PAD - (CONV_WIDTH - 1):CONV_PAD, :] = tail
        cn_ref[:, :, sl * LANES:(sl + 1) * LANES] = tail
    for gi in range(len(POOL_WINDOWS)):
        tail = upad[:, gi, POOL_PAD + ls - POOL_HIST:POOL_PAD + ls, :]
        upad[:, gi, POOL_PAD - POOL_HIST:POOL_PAD, :] = tail
        pn_ref[:, :, gi * POOL_GROUP_DIM:(gi + 1) * POOL_GROUP_DIM] = tail


def _chunk_sum_matrix(tile, chunk):
    j = np.arange(tile)[:, None]
    i = np.arange(tile)[None, :]
    same = (j // chunk) == (i // chunk)
    return jnp.asarray(np.concatenate([same & (j <= i), same], axis=1), dtype=BF16)


def _head_select_matrix():
    sel = np.zeros((2 * SUBLANES, 2 * DN_WIDTH), np.float32)
    for hh in range(DN_HEADS):
        sel[hh, hh * HEAD_DIM:(hh + 1) * HEAD_DIM] = 1.0
        sel[SUBLANES + hh, DN_WIDTH + hh * HEAD_DIM:DN_WIDTH + (hh + 1) * HEAD_DIM] = 1.0
    return jnp.asarray(sel, dtype=BF16)


def _mixer(x, params, states, new_states, *, layer, state_layer, nseq, ls, sub, chunk, pos0):
    nw, w_in, w_ab_t, conv_w, alog, dtb, o_norm, w_pool, pool_scale, w_out = params
    b, l, d = x.shape
    assert sub & (sub - 1) == 0 and chunk & (chunk - 1) == 0 and ls % sub == 0 and sub % chunk == 0
    assert b % nseq == 0 and l % ls == 0 and (nseq == 1 or ls == sub)
    tile = nseq * ls
    sel = _head_select_matrix()
    cum = _chunk_sum_matrix(nseq * sub, chunk)
    stacked = lambda a: _resident(a.shape[1:], layer)
    seq_block = lambda a, which: pl.BlockSpec((None, nseq) + a.shape[2:],
                                              lambda i, j: (which, i) + (0,) * (a.ndim - 2))
    x_spec = pl.BlockSpec((nseq, ls, d), lambda i, j: (i, j, 0))
    untouched = pl.BlockSpec(memory_space=pl.ANY)
    n_in = 16
    return pl.pallas_call(
        functools.partial(_mixer_kernel, nseq=nseq, ls=ls, sub=sub, chunk=chunk, pos0=pos0),
        grid=(b // nseq, l // ls),
        in_specs=[x_spec, stacked(nw), stacked(w_in), stacked(w_ab_t), _resident(sel.shape), _resident(cum.shape),
                  stacked(conv_w), stacked(alog), stacked(dtb), stacked(o_norm), stacked(w_pool),
                  stacked(pool_scale), stacked(w_out),
                  *[seq_block(a, state_layer) for a in states], untouched, untouched, untouched],
        out_specs=(x_spec, *[seq_block(a, layer) for a in new_states]),
        out_shape=(jax.ShapeDtypeStruct(x.shape, x.dtype),
                   *[jax.ShapeDtypeStruct(a.shape, a.dtype) for a in new_states]),
        input_output_aliases={n_in + k: 1 + k for k in range(len(new_states))},
        scratch_shapes=[pltpu.VMEM((nseq, DN_HEADS, HEAD_DIM, HEAD_DIM), F32),
                        pltpu.VMEM((nseq, CONV_CH // LANES, CONV_PAD + ls, LANES), F32),
                        pltpu.VMEM((nseq, len(POOL_WINDOWS), POOL_PAD + ls, POOL_GROUP_DIM), F32),
                        pltpu.VMEM((CONV_CH // LANES, tile, LANES), F32),
                        pltpu.VMEM((len(POOL_WINDOWS), tile, POOL_GROUP_DIM), F32),
                        pltpu.VMEM((d, POOL_WIDTH), BF16)] + list(_stage_buffers(tile).values()),
        compiler_params=pltpu.CompilerParams(dimension_semantics=("arbitrary", "arbitrary"),
                                             vmem_limit_bytes=VMEM_LIMIT_BYTES),
        name="mixer",
    )(x, nw, w_in, w_ab_t, sel, cum, conv_w, alog, dtb, o_norm, w_pool, pool_scale, w_out, *states, *new_states)


def _head_col(v):
    return jnp.pad(v.astype(F32), ((0, 0), (0, SUBLANES - DN_HEADS)))[:, :, None]


def kernel(x_prompt, x_sample, state_delta, state_conv, state_pool, norm_ffn1, w_ffn1_gate, w_ffn1_up, w_ffn1_down, norm_mix, w_in, conv_w, a_log, dt_bias, o_norm, w_pool, pool_scale, w_out, norm_ffn2, w_ffn2_gate, w_ffn2_up, w_ffn2_down, norm_final):
    depth = w_in.shape[0]
    batch, seq, d_model = x_prompt.shape
    dec_batch, dec_seq, _ = x_sample.shape

    w_in_b = w_in.astype(BF16)
    w_a_t = jnp.swapaxes(w_in[..., COL_POOL:COL_POOL + DN_HEADS], 1, 2)
    w_b_t = jnp.swapaxes(w_in[..., COL_POOL + DN_HEADS:COL_POOL_RAW], 1, 2)
    head_pad = jnp.zeros((depth, SUBLANES - DN_HEADS, d_model), w_in.dtype)
    w_ab_t = jnp.concatenate([w_a_t, head_pad, w_b_t, head_pad], axis=1).astype(BF16)
    row = lambda v: v[:, None, :]
    ffn1 = (row(norm_ffn1), w_ffn1_gate.astype(BF16), w_ffn1_up.astype(BF16), w_ffn1_down.astype(BF16))
    ffn2 = (row(norm_ffn2), w_ffn2_gate.astype(BF16), w_ffn2_up.astype(BF16), w_ffn2_down.astype(BF16))
    mix = (row(norm_mix), w_in_b, w_ab_t, conv_w, _head_col(a_log), _head_col(dt_bias), row(o_norm),
           w_pool.astype(BF16), row(pool_scale), w_out.astype(BF16))
    nf = norm_final[None, :]

    state_in = (state_delta, state_conv, state_pool)
    prompt = dict(nseq=batch, ls=256, sub=256, chunk=PROMPT_CHUNK, pos0=0, state_layer=lambda layer: 0,
                  states=tuple(jnp.zeros((1, batch) + s.shape[2:], s.dtype) for s in state_in))
    sample = dict(nseq=dec_batch, ls=dec_seq, sub=dec_seq, chunk=dec_seq, pos0=PAST_LEN, state_layer=lambda layer: layer,
                  states=state_in)
    xp, xs = x_prompt.reshape(batch * seq, d_model), x_sample.reshape(dec_batch * dec_seq, d_model)
    new_p = tuple(jnp.zeros((depth, batch) + s.shape[2:], s.dtype) for s in state_in)
    new_s = tuple(jnp.zeros_like(s) for s in state_in)
    for layer in range(depth):
        xp, xs = _ffn(xp, xs, *ffn1, layer=layer, tm=1024)
        outs = []
        for grp, x, new in ((prompt, xp.reshape(x_prompt.shape), new_p), (sample, xs.reshape(x_sample.shape), new_s)):
            outs.append(_mixer(x, mix, grp["states"], new, layer=layer, state_layer=grp["state_layer"](layer),
                               nseq=grp["nseq"], ls=grp["ls"], sub=grp["sub"], chunk=grp["chunk"], pos0=grp["pos0"]))
        (xp, *new_p), (xs, *new_s) = outs
        fw = nf if layer == depth - 1 else None
        xp, xs = _ffn(xp.reshape(batch * seq, d_model), xs.reshape(dec_batch * dec_seq, d_model), *ffn2, fw,
                      layer=layer, tm=1024)
    return (xp.reshape(x_prompt.shape), xs.reshape(x_sample.shape), *new_p, *new_s)
```

```python
import functools

import numpy as np
import jax
import jax.numpy as jnp
from jax import lax
from jax.experimental import pallas as pl
from jax.experimental.pallas import tpu as pltpu

EPS = 1e-6
PAST_LEN = 4096
DN_HEADS = 4
HEAD_DIM = 128
DN_WIDTH = DN_HEADS * HEAD_DIM
CONV_WIDTH = 4
CONV_CH = 3 * DN_WIDTH
POOL_WINDOWS = (2, 4, 8, 16)
POOL_GROUP_DIM = 128
POOL_WIDTH = len(POOL_WINDOWS) * POOL_GROUP_DIM
POOL_HIST = max(POOL_WINDOWS) - 1
PROMPT_CHUNK = 64
LANES = 128
SUBLANES = 8
INV_BASE = 16
ROW_STRIDE = 4
MIN_STRIDED_ROWS = ROW_STRIDE * SUBLANES
COL_GATE = CONV_CH
COL_POOL = COL_GATE + DN_WIDTH
COL_POOL_RAW = COL_POOL + 2 * DN_HEADS
CONV_PAD = SUBLANES
POOL_PAD = 2 * SUBLANES
ROW_N, ROW_ETOT, ROW_BETA, ROW_W = (SUBLANES * i for i in range(4))
N_ROW_FACTORS = 4
COL_N, COL_Q, COL_QDEC = (SUBLANES * i for i in range(3))
VMEM_LIMIT_BYTES = 56 * 1024 * 1024
FFN_TILE = 1024
PROMPT_TILE = 256

F32 = jnp.float32
BF16 = jnp.bfloat16
NT_DIMS = (((1,), (1,)), ((), ()))


def _rms(x, w):
    return x * lax.rsqrt(jnp.mean(x * x, axis=-1, keepdims=True) + EPS) * w


def _mm(a, b):
    return jnp.dot(a.astype(BF16), b.astype(BF16), preferred_element_type=F32)


def _resident(shape, layer=None):
    zeros = (0,) * len(shape)
    if layer is None:
        return pl.BlockSpec(shape, lambda *_: zeros, pipeline_mode=pl.Buffered(1))
    return pl.BlockSpec((None,) + shape, lambda *_: (layer,) + zeros, pipeline_mode=pl.Buffered(1))


def _ffn_kernel(*refs, f_chunk, n_main, final_norm):
    if final_norm:
        x_ref, xe_ref, nw_ref, wg_ref, wu_ref, wd_ref, fw_ref, o_ref, oe_ref, h_ref = refs
    else:
        x_ref, xe_ref, nw_ref, wg_ref, wu_ref, wd_ref, o_ref, oe_ref, h_ref = refs
    d_ff = wg_ref.shape[1]

    def tile(src_ref, dst_ref):
        rows = src_ref.shape[0]
        x = src_ref[...]
        xn = _rms(x, nw_ref[...]).astype(BF16)
        for c in range(d_ff // f_chunk):
            sl = slice(c * f_chunk, (c + 1) * f_chunk)
            g = jnp.dot(xn, wg_ref[:, sl], preferred_element_type=F32)
            u = jnp.dot(xn, wu_ref[:, sl], preferred_element_type=F32)
            h_ref[:rows, sl] = (g * jax.nn.sigmoid(g) * u).astype(BF16)
        y = x + 0.5 * jnp.dot(h_ref[:rows, :], wd_ref[...], preferred_element_type=F32)
        if final_norm:
            y = _rms(y, fw_ref[...])
        dst_ref[...] = y

    step = pl.program_id(0)
    pl.when(step < n_main)(lambda: tile(x_ref, o_ref))
    pl.when(step == n_main)(lambda: tile(xe_ref, oe_ref))


def _ffn(x, x_extra, nw, wg, wu, wd, fw=None, *, layer, tm):
    m, d = x.shape
    me = x_extra.shape[0]
    d_ff = wg.shape[-1]
    n_main = m // tm
    assert m % tm == 0 and me <= tm
    final_norm = fw is not None
    args = [x, x_extra, nw, wg, wu, wd] + ([fw] if final_norm else [])
    main_tile = lambda i: (jnp.minimum(i, n_main - 1), 0)
    in_specs = [pl.BlockSpec((tm, d), main_tile), pl.BlockSpec((me, d), lambda i: (0, 0)),
                _resident((1, d), layer), _resident((d, d_ff), layer), _resident((d, d_ff), layer),
                _resident((d_ff, d), layer)]
    in_specs += [_resident((1, d))] if final_norm else []
    return pl.pallas_call(
        functools.partial(_ffn_kernel, f_chunk=256, n_main=n_main, final_norm=final_norm),
        grid=(n_main + 1,),
        in_specs=in_specs,
        out_specs=(pl.BlockSpec((tm, d), main_tile), pl.BlockSpec((me, d), lambda i: (0, 0))),
        out_shape=(jax.ShapeDtypeStruct((m, d), x.dtype), jax.ShapeDtypeStruct((me, d), x.dtype)),
        scratch_shapes=[pltpu.VMEM((tm, d_ff), BF16)],
        compiler_params=pltpu.CompilerParams(dimension_semantics=("arbitrary",),
                                             vmem_limit_bytes=VMEM_LIMIT_BYTES),
        name="ffn",
    )(*args)


def _unit_lower_inverses(ns, chunk):
    r = lax.broadcasted_iota(jnp.int32, (chunk, chunk), 0)
    c = lax.broadcasted_iota(jnp.int32, (chunk, chunk), 1)
    base = min(chunk, INV_BASE)
    shift = base.bit_length() - 1
    if chunk > base:
        diag = (r >> shift) == (c >> shift)
        nds = [jnp.where(diag, n, 0.0) for n in ns]
    else:
        nds = ns
    eye = jnp.where(r == c, 1.0, 0.0)
    invs = [eye - nd for nd in nds]
    pws = [nd.astype(BF16) for nd in nds]
    pws = [jnp.dot(pw, pw, preferred_element_type=F32).astype(BF16) for pw in pws]
    order = 2
    while 2 * order < base:
        both = [jnp.dot(jnp.concatenate([inv.astype(BF16), pw], axis=0), pw, preferred_element_type=F32)
                for inv, pw in zip(invs, pws)]
        invs = [inv + bo[:chunk] for inv, bo in zip(invs, both)]
        pws = [bo[chunk:].astype(BF16) for bo in both]
        order *= 2
    invs = [inv + _mm(inv, pw) for inv, pw in zip(invs, pws)]
    blk = base
    while blk < chunk:
        s = blk.bit_length() - 1
        lower = ((r >> (s + 1)) == (c >> (s + 1))) & ((r >> s) > (c >> s))
        tmp = [_mm(inv, jnp.where(lower, n, 0.0)) for inv, n in zip(invs, ns)]
        invs = [inv - _mm(t, inv) for inv, t in zip(invs, tmp)]
        blk *= 2
    return invs


def _mixer_stage1(step, r0, hl, x_ref, nw_ref, win_ref, wu_s, wab_ref, sel_ref, cum_ref, cw_ref, alog_ref, dtb_ref,
                  wp_ref, ps_ref, xpad, upad, qkv_s, d_s, st, *, nseq, ls, pos0):
    tile = nseq * hl
    tr = slice(nseq * r0, nseq * r0 + tile)
    x = x_ref[:, r0:r0 + hl, :].reshape(tile, x_ref.shape[-1])
    h = _rms(x, nw_ref[...]).astype(BF16)
    project = lambda c0, c1: jnp.dot(h, win_ref[:, c0:c1], preferred_element_type=F32)

    def conv_group(part, name):
        pre = project(part * DN_WIDTH, (part + 1) * DN_WIDTH)
        for hh in range(DN_HEADS):
            sl = part * DN_HEADS + hh
            xpad[:, sl, CONV_PAD + r0:CONV_PAD + r0 + hl, :] = (
                pre[:, hh * LANES:(hh + 1) * LANES].reshape(nseq, hl, LANES))
            cw = cw_ref[:, sl * LANES:(sl + 1) * LANES]
            if hl >= MIN_STRIDED_ROWS:
                for s in range(nseq):
                    for r in range(ROW_STRIDE):
                        conv = None
                        for j in range(CONV_WIDTH):
                            rows_j = pl.ds(CONV_PAD + r0 - (CONV_WIDTH - 1) + j + r, hl // ROW_STRIDE,
                                           stride=ROW_STRIDE)
                            term = xpad[s, sl, rows_j, :] * cw[j:j + 1]
                            conv = term if conv is None else conv + term
                        rows_out = pl.ds(nseq * r0 + s * hl + r, hl // ROW_STRIDE, stride=ROW_STRIDE)
                        qkv_s[sl, rows_out, :] = conv * jax.nn.sigmoid(conv)
            else:
                conv = None
                for j in range(CONV_WIDTH):
                    start = CONV_PAD + r0 - (CONV_WIDTH - 1) + j
                    term = xpad[:, sl, start:start + hl, :].reshape(tile, LANES) * cw[j:j + 1]
                    conv = term if conv is None else conv + term
                qkv_s[sl, tr, :] = conv * jax.nn.sigmoid(conv)
            st[name][tr, hh * HEAD_DIM:(hh + 1) * HEAD_DIM] = qkv_s[sl, tr, :].astype(BF16)

    conv_group(0, "qb")
    conv_group(1, "kb")
    conv_group(2, "vb")
    gate = project(COL_GATE, COL_POOL)
    st["gact"][tr, :] = gate * jax.nn.sigmoid(gate)
    u = jnp.dot(h, wu_s[...], preferred_element_type=F32)
    for gi in range(len(POOL_WINDOWS)):
        upad[:, gi, POOL_PAD + r0:POOL_PAD + r0 + hl, :] = (
            u[:, gi * POOL_GROUP_DIM:(gi + 1) * POOL_GROUP_DIM].reshape(nseq, hl, POOL_GROUP_DIM))

    ab_t = lax.dot_general(wab_ref[...], h, NT_DIMS, preferred_element_type=F32)
    qk_part = jnp.concatenate([qkv_s[sl, tr, :] for sl in range(2 * DN_HEADS)], axis=1)
    ss_t = lax.dot_general(sel_ref[...], (qk_part * qk_part).astype(BF16), NT_DIMS,
                           preferred_element_type=F32)
    a_pre = ab_t[:SUBLANES] + dtb_ref[...]
    softplus = jnp.maximum(a_pre, 0.0) + jnp.log1p(jnp.exp(-jnp.abs(a_pre)))
    g = -jnp.exp(alog_ref[...]) * softplus
    beta = jax.nn.sigmoid(ab_t[SUBLANES:])
    r_q = lax.rsqrt(ss_t[:SUBLANES] + EPS) * (HEAD_DIM ** -0.5)
    r_k = lax.rsqrt(ss_t[SUBLANES:] + EPS)
    g_hi = g.astype(BF16).astype(F32)
    g_mid = (g - g_hi).astype(BF16).astype(F32)
    g_lo = g - g_hi - g_mid
    pieces = jnp.concatenate([g_hi, g_mid, g_lo, jnp.zeros_like(g)], axis=0).astype(BF16)
    sums = jnp.dot(pieces, cum_ref[...], preferred_element_type=F32)
    part = lambda k, half: sums[k * SUBLANES:(k + 1) * SUBLANES, half * tile:(half + 1) * tile]
    gc = part(0, 0) + part(1, 0) + part(2, 0)
    gs = part(0, 1) + part(1, 1) + part(2, 1)
    e_gc = jnp.exp(gc)
    row_n = gc - jnp.log(r_k)
    st["rowf"][:, tr] = jnp.concatenate([row_n, jnp.exp(gs), beta, r_k * (beta * e_gc)], axis=0)
    cols = [gc + jnp.log(beta * r_k), gc + jnp.log(r_q), r_q * e_gc]
    st["colf"][tr, :] = jnp.concatenate(cols + [jnp.zeros((LANES - SUBLANES * len(cols), tile), F32)], axis=0).T
    k_dec_row = r_k * jnp.exp(gs - gc)
    for hh in range(DN_HEADS):
        st["kdt"][hh, :, tr] = (qkv_s[DN_HEADS + hh, tr, :].T * k_dec_row[hh:hh + 1, :]).astype(BF16)

    first_pos = pos0 + step * ls + r0
    for gi, win in enumerate(POOL_WINDOWS):
        lanes = slice(gi * POOL_GROUP_DIM, (gi + 1) * POOL_GROUP_DIM)
        if hl >= MIN_STRIDED_ROWS:
            n_rows = hl // ROW_STRIDE
            for s in range(nseq):
                for r in range(ROW_STRIDE):
                    cur = upad[s, gi, pl.ds(POOL_PAD + r0 + r, n_rows, stride=ROW_STRIDE), :]
                    acc = cur
                    for back in range(1, win):
                        acc = acc + upad[s, gi, pl.ds(POOL_PAD + r0 + r - back, n_rows, stride=ROW_STRIDE), :]
                    pos = first_pos + r + ROW_STRIDE * lax.broadcasted_iota(jnp.int32, (n_rows, 1), 0)
                    cnt = jnp.minimum(win, pos + 1).astype(F32)
                    d_s[gi, pl.ds(nseq * r0 + s * hl + r, n_rows, stride=ROW_STRIDE), :] = acc / cnt - cur
            diff = d_s[gi, tr, :]
        else:
            acc = u[:, lanes]
            for back in range(1, win):
                start = POOL_PAD + r0 - back
                acc = acc + upad[:, gi, start:start + hl, :].reshape(tile, POOL_GROUP_DIM)
            pos = first_pos + (lax.broadcasted_iota(jnp.int32, (tile, 1), 0) & (hl - 1))
            cnt = jnp.minimum(win, pos + 1).astype(F32)
            diff = acc / cnt - u[:, lanes]
        st["z"][tr, lanes] = _mm(diff, wp_ref[gi]) * ps_ref[:, lanes]


def _mixer_stage2(r0, hl, states, x_ref, on_ref, wout_ref, y_ref, qkv_s, st, *, nseq, ls, chunk):
    tile = nseq * hl
    tr = slice(nseq * r0, nseq * r0 + tile)
    chunks = range(r0 // chunk, (r0 + hl) // chunk)
    cr = lax.broadcasted_iota(jnp.int32, (chunk, chunk), 0)
    cc = lax.broadcasted_iota(jnp.int32, (chunk, chunk), 1)
    incl = cr >= cc
    strict = cr > cc

    probs = [(hh, s, ci) for hh in range(DN_HEADS) for s in range(nseq) for ci in chunks]
    rows = lambda pr: slice(pr[1] * ls + pr[2] * chunk, pr[1] * ls + (pr[2] + 1) * chunk)
    head = lambda pr: slice(pr[0] * HEAD_DIM, (pr[0] + 1) * HEAD_DIM)
    col = lambda base, pr: st["colf"][rows(pr), base + pr[0]:base + pr[0] + 1]
    row = lambda base, pr: st["rowf"][base + pr[0]:base + pr[0] + 1, rows(pr)]
    kq = [lax.dot_general(jnp.concatenate([st["kb"][rows(pr), head(pr)], st["qb"][rows(pr), head(pr)]], axis=0),
                          st["kb"][rows(pr), head(pr)], NT_DIMS, preferred_element_type=F32) for pr in probs]
    ns = [jnp.where(strict, kq_p[:chunk] * jnp.exp(jnp.where(strict, col(COL_N, pr) - row(ROW_N, pr), -1e30)), 0.0)
          for pr, kq_p in zip(probs, kq)]
    qkd = [kq_p[chunk:] * jnp.exp(jnp.where(incl, col(COL_Q, pr) - row(ROW_N, pr), -1e30))
           for pr, kq_p in zip(probs, kq)]
    a_inv = _unit_lower_inverses(ns, chunk)
    sol = [jnp.concatenate([_mm(inv_p * row(ROW_BETA, pr), st["vb"][rows(pr), head(pr)]),
                            _mm(inv_p * row(ROW_W, pr), st["kb"][rows(pr), head(pr)])], axis=1).astype(BF16)
           for pr, inv_p in zip(probs, a_inv)]
    qw = [_mm(qkd_p, sol_p) for qkd_p, sol_p in zip(qkd, sol)]
    kw = [jnp.dot(st["kdt"][pr[0], :, rows(pr)], sol_p, preferred_element_type=F32) for pr, sol_p in zip(probs, sol)]

    lhs = {pr: jnp.concatenate([qkv_s[pr[0], rows(pr), :] * col(COL_QDEC, pr) - qw_p[:, HEAD_DIM:],
                                kw_p[:, HEAD_DIM:]], axis=0).astype(BF16)
           for pr, qw_p, kw_p in zip(probs, qw, kw)}
    o_add = {pr: qw_p[:, :HEAD_DIM] for pr, qw_p in zip(probs, qw)}
    s_add = {pr: kw_p[:, :HEAD_DIM] for pr, kw_p in zip(probs, kw)}
    o_chunk = {}
    for ci in chunks:
        for hh in range(DN_HEADS):
            for s in range(nseq):
                pr = (hh, s, ci)
                both = jnp.dot(lhs[pr], states[hh, s].astype(BF16), preferred_element_type=F32)
                o_chunk[pr] = o_add[pr] + both[:chunk]
                first = s * ls + ci * chunk
                e_chunk = st["rowf"][ROW_ETOT + hh:ROW_ETOT + hh + 1, first:first + 1]
                states[hh, s] = states[hh, s] * e_chunk - both[chunk:] + s_add[pr]
    o_heads = []
    for hh in range(DN_HEADS):
        o_parts = [o_chunk[hh, s, ci] for s in range(nseq) for ci in chunks]
        o_h = jnp.concatenate(o_parts, axis=0) if len(o_parts) > 1 else o_parts[0]
        o_heads.append(_rms(o_h, on_ref[...]) * st["gact"][tr, hh * HEAD_DIM:(hh + 1) * HEAD_DIM])

    mixed = jnp.concatenate(o_heads + [st["z"][tr, :]], axis=1).astype(BF16)
    x = x_ref[:, r0:r0 + hl, :].reshape(tile, x_ref.shape[-1])
    y = x + jnp.dot(mixed, wout_ref[...], preferred_element_type=F32)
    y_ref[:, r0:r0 + hl, :] = y.reshape(nseq, hl, y.shape[-1])


def _stage_buffers(tile):
    return {
        "qb": pltpu.VMEM((tile, DN_WIDTH), BF16), "kb": pltpu.VMEM((tile, DN_WIDTH), BF16),
        "vb": pltpu.VMEM((tile, DN_WIDTH), BF16), "kdt": pltpu.VMEM((DN_HEADS, HEAD_DIM, tile), BF16),
        "gact": pltpu.VMEM((tile, DN_WIDTH), F32), "z": pltpu.VMEM((tile, POOL_WIDTH), F32),
        "colf": pltpu.VMEM((tile, LANES), F32), "rowf": pltpu.VMEM((N_ROW_FACTORS * SUBLANES, tile), F32),
    }


def _mixer_kernel(x_ref, nw_ref, win_ref, wab_ref, sel_ref, cum_ref, cw_ref, alog_ref, dtb_ref, on_ref,
                  wp_ref, ps_ref, wout_ref, s0_ref, c0_ref, p0_ref, sprev_ref, cprev_ref, pprev_ref,
                  y_ref, sn_ref, cn_ref, pn_ref,
                  s_scr, xpad, upad, qkv_s, d_s, wu_s, *stage_refs, nseq, ls, sub, chunk, pos0):
    del sprev_ref, cprev_ref, pprev_ref
    step = pl.program_id(1)
    st = dict(zip(_stage_buffers(nseq * ls), stage_refs))

    @pl.when(step == 0)
    def _():
        s_scr[...] = s0_ref[...]
        for sl in range(CONV_CH // LANES):
            xpad[:, sl, CONV_PAD - (CONV_WIDTH - 1):CONV_PAD, :] = c0_ref[:, :, sl * LANES:(sl + 1) * LANES]
        for gi in range(len(POOL_WINDOWS)):
            upad[:, gi, POOL_PAD - POOL_HIST:POOL_PAD, :] = p0_ref[:, :, gi * POOL_GROUP_DIM:(gi + 1) * POOL_GROUP_DIM]
        wu_s[...] = win_ref[:, COL_POOL_RAW:COL_POOL_RAW + POOL_WIDTH]

    stage1 = lambda r0: _mixer_stage1(step, r0, sub, x_ref, nw_ref, win_ref, wu_s, wab_ref, sel_ref, cum_ref, cw_ref,
                                      alog_ref, dtb_ref, wp_ref, ps_ref, xpad, upad, qkv_s, d_s, st,
                                      nseq=nseq, ls=ls, pos0=pos0)
    stage2 = lambda r0: _mixer_stage2(r0, sub, states, x_ref, on_ref, wout_ref, y_ref, qkv_s, st,
                                      nseq=nseq, ls=ls, chunk=chunk)
    states = {(hh, s): s_scr[s, hh] for hh in range(DN_HEADS) for s in range(nseq)}
    for r0 in range(0, ls, sub):
        stage1(r0)
        stage2(r0)
    for (hh, s), state in states.items():
        s_scr[s, hh] = state
    sn_ref[...] = s_scr[...]

    for sl in range(CONV_CH // LANES):
        tail = xpad[:, sl, CONV_PAD + ls - (CONV_WIDTH - 1):CONV_PAD + ls, :]
        xpad[:, sl, CONV_PAD - (CONV_WIDTH - 1):CONV_PAD, :] = tail
        cn_ref[:, :, sl * LANES:(sl + 1) * LANES] = tail
    for gi in range(len(POOL_WINDOWS)):
        tail = upad[:, gi, POOL_PAD + ls - POOL_HIST:POOL_PAD + ls, :]
        upad[:, gi, POOL_PAD - POOL_HIST:POOL_PAD, :] = tail
        pn_ref[:, :, gi * POOL_GROUP_DIM:(gi + 1) * POOL_GROUP_DIM] = tail


def _chunk_sum_matrix(tile, chunk):
    j = np.arange(tile)[:, None]
    i = np.arange(tile)[None, :]
    same = (j // chunk) == (i // chunk)
    return jnp.asarray(np.concatenate([same & (j <= i), same], axis=1), dtype=BF16)


def _head_select_matrix():
    sel = np.zeros((2 * SUBLANES, 2 * DN_WIDTH), np.float32)
    for hh in range(DN_HEADS):
        sel[hh, hh * HEAD_DIM:(hh + 1) * HEAD_DIM] = 1.0
        sel[SUBLANES + hh, DN_WIDTH + hh * HEAD_DIM:DN_WIDTH + (hh + 1) * HEAD_DIM] = 1.0
    return jnp.asarray(sel, dtype=BF16)


def _mixer(x, params, states, new_states, *, layer, state_layer, nseq, ls, sub, chunk, pos0):
    nw, w_in, w_ab_t, conv_w, alog, dtb, o_norm, w_pool, pool_scale, w_out = params
    b, l, d = x.shape
    assert sub & (sub - 1) == 0 and chunk & (chunk - 1) == 0 and ls % sub == 0 and sub % chunk == 0
    assert b % nseq == 0 and l % ls == 0 and (nseq == 1 or ls == sub)
    tile = nseq * ls
    sel = _head_select_matrix()
    cum = _chunk_sum_matrix(nseq * sub, chunk)
    stacked = lambda a: _resident(a.shape[1:], layer)
    seq_block = lambda a, which: pl.BlockSpec((None, nseq) + a.shape[2:],
                                              lambda i, j: (which, i) + (0,) * (a.ndim - 2))
    x_spec = pl.BlockSpec((nseq, ls, d), lambda i, j: (i, j, 0))
    untouched = pl.BlockSpec(memory_space=pl.ANY)
    n_in = 16
    return pl.pallas_call(
        functools.partial(_mixer_kernel, nseq=nseq, ls=ls, sub=sub, chunk=chunk, pos0=pos0),
        grid=(b // nseq, l // ls),
        in_specs=[x_spec, stacked(nw), stacked(w_in), stacked(w_ab_t), _resident(sel.shape), _resident(cum.shape),
                  stacked(conv_w), stacked(alog), stacked(dtb), stacked(o_norm), stacked(w_pool),
                  stacked(pool_scale), stacked(w_out),
                  *[seq_block(a, state_layer) for a in states], untouched, untouched, untouched],
        out_specs=(x_spec, *[seq_block(a, layer) for a in new_states]),
        out_shape=(jax.ShapeDtypeStruct(x.shape, x.dtype),
                   *[jax.ShapeDtypeStruct(a.shape, a.dtype) for a in new_states]),
        input_output_aliases={n_in + k: 1 + k for k in range(len(new_states))},
        scratch_shapes=[pltpu.VMEM((nseq, DN_HEADS, HEAD_DIM, HEAD_DIM), F32),
                        pltpu.VMEM((nseq, CONV_CH // LANES, CONV_PAD + ls, LANES), F32),
                        pltpu.VMEM((nseq, len(POOL_WINDOWS), POOL_PAD + ls, POOL_GROUP_DIM), F32),
                        pltpu.VMEM((CONV_CH // LANES, tile, LANES), F32),
                        pltpu.VMEM((len(POOL_WINDOWS), tile, POOL_GROUP_DIM), F32),
                        pltpu.VMEM((d, POOL_WIDTH), BF16)] + list(_stage_buffers(tile).values()),
        compiler_params=pltpu.CompilerParams(dimension_semantics=("arbitrary", "arbitrary"),
                                             vmem_limit_bytes=VMEM_LIMIT_BYTES),
        name="mixer",
    )(x, nw, w_in, w_ab_t, sel, cum, conv_w, alog, dtb, o_norm, w_pool, pool_scale, w_out, *states, *new_states)


def _head_col(v):
    return jnp.pad(v.astype(F32), ((0, 0), (0, SUBLANES - DN_HEADS)))[:, :, None]


def kernel(x_prompt, x_sample, state_delta, state_conv, state_pool, norm_ffn1, w_ffn1_gate, w_ffn1_up, w_ffn1_down, norm_mix, w_in, conv_w, a_log, dt_bias, o_norm, w_pool, pool_scale, w_out, norm_ffn2, w_ffn2_gate, w_ffn2_up, w_ffn2_down, norm_final):
    depth = w_in.shape[0]
    batch, seq, d_model = x_prompt.shape
    dec_batch, dec_seq, _ = x_sample.shape

    w_in_b = w_in.astype(BF16)
    w_a_t = jnp.swapaxes(w_in[..., COL_POOL:COL_POOL + DN_HEADS], 1, 2)
    w_b_t = jnp.swapaxes(w_in[..., COL_POOL + DN_HEADS:COL_POOL_RAW], 1, 2)
    head_pad = jnp.zeros((depth, SUBLANES - DN_HEADS, d_model), w_in.dtype)
    w_ab_t = jnp.concatenate([w_a_t, head_pad, w_b_t, head_pad], axis=1).astype(BF16)
    row = lambda v: v[:, None, :]
    ffn1 = (row(norm_ffn1), w_ffn1_gate.astype(BF16), w_ffn1_up.astype(BF16), w_ffn1_down.astype(BF16))
    ffn2 = (row(norm_ffn2), w_ffn2_gate.astype(BF16), w_ffn2_up.astype(BF16), w_ffn2_down.astype(BF16))
    mix = (row(norm_mix), w_in_b, w_ab_t, conv_w, _head_col(a_log), _head_col(dt_bias), row(o_norm),
           w_pool.astype(BF16), row(pool_scale), w_out.astype(BF16))
    nf = norm_final[None, :]

    state_in = (state_delta, state_conv, state_pool)
    prompt = dict(nseq=batch, ls=PROMPT_TILE, sub=PROMPT_TILE, chunk=PROMPT_CHUNK, pos0=0, state_layer=lambda layer: 0,
                  states=tuple(jnp.zeros((1, batch) + s.shape[2:], s.dtype) for s in state_in))
    sample = dict(nseq=dec_batch, ls=dec_seq, sub=dec_seq, chunk=dec_seq, pos0=PAST_LEN, state_layer=lambda layer: layer,
                  states=state_in)
    xp, xs = x_prompt.reshape(batch * seq, d_model), x_sample.reshape(dec_batch * dec_seq, d_model)
    new_p = tuple(jnp.zeros((depth, batch) + s.shape[2:], s.dtype) for s in state_in)
    new_s = tuple(jnp.zeros_like(s) for s in state_in)
    for layer in range(depth):
        xp, xs = _ffn(xp, xs, *ffn1, layer=layer, tm=FFN_TILE)
        outs = []
        for grp, x, new in ((prompt, xp.reshape(x_prompt.shape), new_p), (sample, xs.reshape(x_sample.shape), new_s)):
            outs.append(_mixer(x, mix, grp["states"], new, layer=layer, state_layer=grp["state_layer"](layer),
                               nseq=grp["nseq"], ls=grp["ls"], sub=grp["sub"], chunk=grp["chunk"], pos0=grp["pos0"]))
        (xp, *new_p), (xs, *new_s) = outs
        fw = nf if layer == depth - 1 else None
        xp, xs = _ffn(xp.reshape(batch * seq, d_model), xs.reshape(dec_batch * dec_seq, d_model), *ffn2, fw,
                      layer=layer, tm=FFN_TILE)
    return (xp.reshape(x_prompt.shape), xs.reshape(x_sample.shape), *new_p, *new_s)
```

```python
import functools

import numpy as np
import jax
import jax.numpy as jnp
from jax import lax
from jax.experimental import pallas as pl
from jax.experimental.pallas import tpu as pltpu

EPS = 1e-6
PAST_LEN = 4096
DN_HEADS = 4
HEAD_DIM = 128
DN_WIDTH = DN_HEADS * HEAD_DIM
CONV_WIDTH = 4
CONV_CH = 3 * DN_WIDTH
POOL_WINDOWS = (2, 4, 8, 16)
POOL_GROUP_DIM = 128
POOL_WIDTH = len(POOL_WINDOWS) * POOL_GROUP_DIM
POOL_HIST = max(POOL_WINDOWS) - 1
PROMPT_CHUNK = 128
LANES = 128
SUBLANES = 8
INV_BASE = 16
ROW_STRIDE = 4
MIN_STRIDED_ROWS = ROW_STRIDE * SUBLANES
COL_GATE = CONV_CH
COL_POOL = COL_GATE + DN_WIDTH
COL_POOL_RAW = COL_POOL + 2 * DN_HEADS
CONV_PAD = SUBLANES
POOL_PAD = 2 * SUBLANES
ROW_N, ROW_ETOT, ROW_BETA, ROW_W = (SUBLANES * i for i in range(4))
N_ROW_FACTORS = 4
COL_N, COL_Q, COL_QDEC = (SUBLANES * i for i in range(3))
VMEM_LIMIT_BYTES = 56 * 1024 * 1024
FFN_TILE = 1024
PROMPT_TILE = 256

F32 = jnp.float32
BF16 = jnp.bfloat16
NT_DIMS = (((1,), (1,)), ((), ()))


def _rms(x, w):
    return x * lax.rsqrt(jnp.mean(x * x, axis=-1, keepdims=True) + EPS) * w


def _mm(a, b):
    return jnp.dot(a.astype(BF16), b.astype(BF16), preferred_element_type=F32)


def _resident(shape, layer=None):
    zeros = (0,) * len(shape)
    if layer is None:
        return pl.BlockSpec(shape, lambda *_: zeros, pipeline_mode=pl.Buffered(1))
    return pl.BlockSpec((None,) + shape, lambda *_: (layer,) + zeros, pipeline_mode=pl.Buffered(1))


def _ffn_kernel(*refs, f_chunk, n_main, final_norm):
    if final_norm:
        x_ref, xe_ref, nw_ref, wg_ref, wu_ref, wd_ref, fw_ref, o_ref, oe_ref, h_ref = refs
    else:
        x_ref, xe_ref, nw_ref, wg_ref, wu_ref, wd_ref, o_ref, oe_ref, h_ref = refs
    d_ff = wg_ref.shape[1]

    def tile(src_ref, dst_ref):
        rows = src_ref.shape[0]
        x = src_ref[...]
        xn = _rms(x, nw_ref[...]).astype(BF16)
        for c in range(d_ff // f_chunk):
            sl = slice(c * f_chunk, (c + 1) * f_chunk)
            g = jnp.dot(xn, wg_ref[:, sl], preferred_element_type=F32)
            u = jnp.dot(xn, wu_ref[:, sl], preferred_element_type=F32)
            h_ref[:rows, sl] = (g * jax.nn.sigmoid(g) * u).astype(BF16)
        y = x + 0.5 * jnp.dot(h_ref[:rows, :], wd_ref[...], preferred_element_type=F32)
        if final_norm:
            y = _rms(y, fw_ref[...])
        dst_ref[...] = y

    step = pl.program_id(0)
    pl.when(step < n_main)(lambda: tile(x_ref, o_ref))
    pl.when(step == n_main)(lambda: tile(xe_ref, oe_ref))


def _ffn(x, x_extra, nw, wg, wu, wd, fw=None, *, layer, tm):
    m, d = x.shape
    me = x_extra.shape[0]
    d_ff = wg.shape[-1]
    n_main = m // tm
    assert m % tm == 0 and me <= tm
    final_norm = fw is not None
    args = [x, x_extra, nw, wg, wu, wd] + ([fw] if final_norm else [])
    main_tile = lambda i: (jnp.minimum(i, n_main - 1), 0)
    in_specs = [pl.BlockSpec((tm, d), main_tile), pl.BlockSpec((me, d), lambda i: (0, 0)),
                _resident((1, d), layer), _resident((d, d_ff), layer), _resident((d, d_ff), layer),
                _resident((d_ff, d), layer)]
    in_specs += [_resident((1, d))] if final_norm else []
    return pl.pallas_call(
        functools.partial(_ffn_kernel, f_chunk=256, n_main=n_main, final_norm=final_norm),
        grid=(n_main + 1,),
        in_specs=in_specs,
        out_specs=(pl.BlockSpec((tm, d), main_tile), pl.BlockSpec((me, d), lambda i: (0, 0))),
        out_shape=(jax.ShapeDtypeStruct((m, d), x.dtype), jax.ShapeDtypeStruct((me, d), x.dtype)),
        scratch_shapes=[pltpu.VMEM((tm, d_ff), BF16)],
        compiler_params=pltpu.CompilerParams(dimension_semantics=("arbitrary",),
                                             vmem_limit_bytes=VMEM_LIMIT_BYTES),
        name="ffn",
    )(*args)


def _unit_lower_inverses(ns, chunk):
    r = lax.broadcasted_iota(jnp.int32, (chunk, chunk), 0)
    c = lax.broadcasted_iota(jnp.int32, (chunk, chunk), 1)
    base = min(chunk, INV_BASE)
    shift = base.bit_length() - 1
    if chunk > base:
        diag = (r >> shift) == (c >> shift)
        nds = [jnp.where(diag, n, 0.0) for n in ns]
    else:
        nds = ns
    eye = jnp.where(r == c, 1.0, 0.0)
    invs = [eye - nd for nd in nds]
    pws = [nd.astype(BF16) for nd in nds]
    pws = [jnp.dot(pw, pw, preferred_element_type=F32).astype(BF16) for pw in pws]
    order = 2
    while 2 * order < base:
        both = [jnp.dot(jnp.concatenate([inv.astype(BF16), pw], axis=0), pw, preferred_element_type=F32)
                for inv, pw in zip(invs, pws)]
        invs = [inv + bo[:chunk] for inv, bo in zip(invs, both)]
        pws = [bo[chunk:].astype(BF16) for bo in both]
        order *= 2
    invs = [inv + _mm(inv, pw) for inv, pw in zip(invs, pws)]
    blk = base
    while blk < chunk:
        s = blk.bit_length() - 1
        lower = ((r >> (s + 1)) == (c >> (s + 1))) & ((r >> s) > (c >> s))
        tmp = [_mm(inv, jnp.where(lower, n, 0.0)) for inv, n in zip(invs, ns)]
        invs = [inv - _mm(t, inv) for inv, t in zip(invs, tmp)]
        blk *= 2
    return invs


def _mixer_stage1(step, r0, hl, x_ref, nw_ref, win_ref, wu_s, wab_ref, sel_ref, cum_ref, cw_ref, alog_ref, dtb_ref,
                  wp_ref, ps_ref, xpad, upad, qkv_s, d_s, st, *, nseq, ls, pos0):
    tile = nseq * hl
    tr = slice(nseq * r0, nseq * r0 + tile)
    x = x_ref[:, r0:r0 + hl, :].reshape(tile, x_ref.shape[-1])
    h = _rms(x, nw_ref[...]).astype(BF16)
    project = lambda c0, c1: jnp.dot(h, win_ref[:, c0:c1], preferred_element_type=F32)

    def conv_group(part, name):
        pre = project(part * DN_WIDTH, (part + 1) * DN_WIDTH)
        for hh in range(DN_HEADS):
            sl = part * DN_HEADS + hh
            xpad[:, sl, CONV_PAD + r0:CONV_PAD + r0 + hl, :] = (
                pre[:, hh * LANES:(hh + 1) * LANES].reshape(nseq, hl, LANES))
            cw = cw_ref[:, sl * LANES:(sl + 1) * LANES]
            if hl >= MIN_STRIDED_ROWS:
                for s in range(nseq):
                    for r in range(ROW_STRIDE):
                        conv = None
                        for j in range(CONV_WIDTH):
                            rows_j = pl.ds(CONV_PAD + r0 - (CONV_WIDTH - 1) + j + r, hl // ROW_STRIDE,
                                           stride=ROW_STRIDE)
                            term = xpad[s, sl, rows_j, :] * cw[j:j + 1]
                            conv = term if conv is None else conv + term
                        rows_out = pl.ds(nseq * r0 + s * hl + r, hl // ROW_STRIDE, stride=ROW_STRIDE)
                        qkv_s[sl, rows_out, :] = conv * jax.nn.sigmoid(conv)
            else:
                conv = None
                for j in range(CONV_WIDTH):
                    start = CONV_PAD + r0 - (CONV_WIDTH - 1) + j
                    term = xpad[:, sl, start:start + hl, :].reshape(tile, LANES) * cw[j:j + 1]
                    conv = term if conv is None else conv + term
                qkv_s[sl, tr, :] = conv * jax.nn.sigmoid(conv)
            st[name][tr, hh * HEAD_DIM:(hh + 1) * HEAD_DIM] = qkv_s[sl, tr, :].astype(BF16)

    conv_group(0, "qb")
    conv_group(1, "kb")
    conv_group(2, "vb")
    gate = project(COL_GATE, COL_POOL)
    st["gact"][tr, :] = gate * jax.nn.sigmoid(gate)
    u = jnp.dot(h, wu_s[...], preferred_element_type=F32)
    for gi in range(len(POOL_WINDOWS)):
        upad[:, gi, POOL_PAD + r0:POOL_PAD + r0 + hl, :] = (
            u[:, gi * POOL_GROUP_DIM:(gi + 1) * POOL_GROUP_DIM].reshape(nseq, hl, POOL_GROUP_DIM))

    ab_t = lax.dot_general(wab_ref[...], h, NT_DIMS, preferred_element_type=F32)
    qk_part = jnp.concatenate([qkv_s[sl, tr, :] for sl in range(2 * DN_HEADS)], axis=1)
    ss_t = lax.dot_general(sel_ref[...], (qk_part * qk_part).astype(BF16), NT_DIMS,
                           preferred_element_type=F32)
    a_pre = ab_t[:SUBLANES] + dtb_ref[...]
    softplus = jnp.maximum(a_pre, 0.0) + jnp.log1p(jnp.exp(-jnp.abs(a_pre)))
    g = -jnp.exp(alog_ref[...]) * softplus
    beta = jax.nn.sigmoid(ab_t[SUBLANES:])
    r_q = lax.rsqrt(ss_t[:SUBLANES] + EPS) * (HEAD_DIM ** -0.5)
    r_k = lax.rsqrt(ss_t[SUBLANES:] + EPS)
    g_hi = g.astype(BF16).astype(F32)
    g_mid = (g - g_hi).astype(BF16).astype(F32)
    g_lo = g - g_hi - g_mid
    pieces = jnp.concatenate([g_hi, g_mid, g_lo, jnp.zeros_like(g)], axis=0).astype(BF16)
    sums = jnp.dot(pieces, cum_ref[...], preferred_element_type=F32)
    part = lambda k, half: sums[k * SUBLANES:(k + 1) * SUBLANES, half * tile:(half + 1) * tile]
    gc = part(0, 0) + part(1, 0) + part(2, 0)
    gs = part(0, 1) + part(1, 1) + part(2, 1)
    e_gc = jnp.exp(gc)
    row_n = gc - jnp.log(r_k)
    st["rowf"][:, tr] = jnp.concatenate([row_n, jnp.exp(gs), beta, r_k * (beta * e_gc)], axis=0)
    cols = [gc + jnp.log(beta * r_k), gc + jnp.log(r_q), r_q * e_gc]
    st["colf"][tr, :] = jnp.concatenate(cols + [jnp.zeros((LANES - SUBLANES * len(cols), tile), F32)], axis=0).T
    k_dec_row = r_k * jnp.exp(gs - gc)
    for hh in range(DN_HEADS):
        st["kdt"][hh, :, tr] = (qkv_s[DN_HEADS + hh, tr, :].T * k_dec_row[hh:hh + 1, :]).astype(BF16)

    first_pos = pos0 + step * ls + r0
    for gi, win in enumerate(POOL_WINDOWS):
        lanes = slice(gi * POOL_GROUP_DIM, (gi + 1) * POOL_GROUP_DIM)
        if hl >= MIN_STRIDED_ROWS:
            n_rows = hl // ROW_STRIDE
            for s in range(nseq):
                for r in range(ROW_STRIDE):
                    cur = upad[s, gi, pl.ds(POOL_PAD + r0 + r, n_rows, stride=ROW_STRIDE), :]
                    acc = cur
                    for back in range(1, win):
                        acc = acc + upad[s, gi, pl.ds(POOL_PAD + r0 + r - back, n_rows, stride=ROW_STRIDE), :]
                    pos = first_pos + r + ROW_STRIDE * lax.broadcasted_iota(jnp.int32, (n_rows, 1), 0)
                    cnt = jnp.minimum(win, pos + 1).astype(F32)
                    d_s[gi, pl.ds(nseq * r0 + s * hl + r, n_rows, stride=ROW_STRIDE), :] = acc / cnt - cur
            diff = d_s[gi, tr, :]
        else:
            acc = u[:, lanes]
            for back in range(1, win):
                start = POOL_PAD + r0 - back
                acc = acc + upad[:, gi, start:start + hl, :].reshape(tile, POOL_GROUP_DIM)
            pos = first_pos + (lax.broadcasted_iota(jnp.int32, (tile, 1), 0) & (hl - 1))
            cnt = jnp.minimum(win, pos + 1).astype(F32)
            diff = acc / cnt - u[:, lanes]
        st["z"][tr, lanes] = _mm(diff, wp_ref[gi]) * ps_ref[:, lanes]


def _mixer_stage2(r0, hl, states, x_ref, on_ref, wout_ref, y_ref, qkv_s, st, *, nseq, ls, chunk):
    tile = nseq * hl
    tr = slice(nseq * r0, nseq * r0 + tile)
    chunks = range(r0 // chunk, (r0 + hl) // chunk)
    cr = lax.broadcasted_iota(jnp.int32, (chunk, chunk), 0)
    cc = lax.broadcasted_iota(jnp.int32, (chunk, chunk), 1)
    incl = cr >= cc
    strict = cr > cc

    probs = [(hh, s, ci) for hh in range(DN_HEADS) for s in range(nseq) for ci in chunks]
    rows = lambda pr: slice(pr[1] * ls + pr[2] * chunk, pr[1] * ls + (pr[2] + 1) * chunk)
    head = lambda pr: slice(pr[0] * HEAD_DIM, (pr[0] + 1) * HEAD_DIM)
    col = lambda base, pr: st["colf"][rows(pr), base + pr[0]:base + pr[0] + 1]
    row = lambda base, pr: st["rowf"][base + pr[0]:base + pr[0] + 1, rows(pr)]
    kq = [lax.dot_general(jnp.concatenate([st["kb"][rows(pr), head(pr)], st["qb"][rows(pr), head(pr)]], axis=0),
                          st["kb"][rows(pr), head(pr)], NT_DIMS, preferred_element_type=F32) for pr in probs]
    ns = [jnp.where(strict, kq_p[:chunk] * jnp.exp(jnp.where(strict, col(COL_N, pr) - row(ROW_N, pr), -1e30)), 0.0)
          for pr, kq_p in zip(probs, kq)]
    qkd = [kq_p[chunk:] * jnp.exp(jnp.where(incl, col(COL_Q, pr) - row(ROW_N, pr), -1e30))
           for pr, kq_p in zip(probs, kq)]
    a_inv = _unit_lower_inverses(ns, chunk)
    sol = [jnp.concatenate([_mm(inv_p * row(ROW_BETA, pr), st["vb"][rows(pr), head(pr)]),
                            _mm(inv_p * row(ROW_W, pr), st["kb"][rows(pr), head(pr)])], axis=1).astype(BF16)
           for pr, inv_p in zip(probs, a_inv)]
    qw = [_mm(qkd_p, sol_p) for qkd_p, sol_p in zip(qkd, sol)]
    kw = [jnp.dot(st["kdt"][pr[0], :, rows(pr)], sol_p, preferred_element_type=F32) for pr, sol_p in zip(probs, sol)]

    lhs = {pr: jnp.concatenate([qkv_s[pr[0], rows(pr), :] * col(COL_QDEC, pr) - qw_p[:, HEAD_DIM:],
                                kw_p[:, HEAD_DIM:]], axis=0).astype(BF16)
           for pr, qw_p, kw_p in zip(probs, qw, kw)}
    o_add = {pr: qw_p[:, :HEAD_DIM] for pr, qw_p in zip(probs, qw)}
    s_add = {pr: kw_p[:, :HEAD_DIM] for pr, kw_p in zip(probs, kw)}
    o_chunk = {}
    for ci in chunks:
        for hh in range(DN_HEADS):
            for s in range(nseq):
                pr = (hh, s, ci)
                both = jnp.dot(lhs[pr], states[hh, s].astype(BF16), preferred_element_type=F32)
                o_chunk[pr] = o_add[pr] + both[:chunk]
                first = s * ls + ci * chunk
                e_chunk = st["rowf"][ROW_ETOT + hh:ROW_ETOT + hh + 1, first:first + 1]
                states[hh, s] = states[hh, s] * e_chunk - both[chunk:] + s_add[pr]
    o_heads = []
    for hh in range(DN_HEADS):
        o_parts = [o_chunk[hh, s, ci] for s in range(nseq) for ci in chunks]
        o_h = jnp.concatenate(o_parts, axis=0) if len(o_parts) > 1 else o_parts[0]
        o_heads.append(_rms(o_h, on_ref[...]) * st["gact"][tr, hh * HEAD_DIM:(hh + 1) * HEAD_DIM])

    mixed = jnp.concatenate(o_heads + [st["z"][tr, :]], axis=1).astype(BF16)
    x = x_ref[:, r0:r0 + hl, :].reshape(tile, x_ref.shape[-1])
    y = x + jnp.dot(mixed, wout_ref[...], preferred_element_type=F32)
    y_ref[:, r0:r0 + hl, :] = y.reshape(nseq, hl, y.shape[-1])


def _stage_buffers(tile):
    return {
        "qb": pltpu.VMEM((tile, DN_WIDTH), BF16), "kb": pltpu.VMEM((tile, DN_WIDTH), BF16),
        "vb": pltpu.VMEM((tile, DN_WIDTH), BF16), "kdt": pltpu.VMEM((DN_HEADS, HEAD_DIM, tile), BF16),
        "gact": pltpu.VMEM((tile, DN_WIDTH), F32), "z": pltpu.VMEM((tile, POOL_WIDTH), F32),
        "colf": pltpu.VMEM((tile, LANES), F32), "rowf": pltpu.VMEM((N_ROW_FACTORS * SUBLANES, tile), F32),
    }


def _mixer_kernel(x_ref, nw_ref, win_ref, wab_ref, sel_ref, cum_ref, cw_ref, alog_ref, dtb_ref, on_ref,
                  wp_ref, ps_ref, wout_ref, s0_ref, c0_ref, p0_ref, sprev_ref, cprev_ref, pprev_ref,
                  y_ref, sn_ref, cn_ref, pn_ref,
                  s_scr, xpad, upad, qkv_s, d_s, wu_s, *stage_refs, nseq, ls, sub, chunk, pos0):
    del sprev_ref, cprev_ref, pprev_ref
    step = pl.program_id(1)
    st = dict(zip(_stage_buffers(nseq * ls), stage_refs))

    @pl.when(step == 0)
    def _():
        s_scr[...] = s0_ref[...]
        for sl in range(CONV_CH // LANES):
            xpad[:, sl, CONV_PAD - (CONV_WIDTH - 1):CONV_PAD, :] = c0_ref[:, :, sl * LANES:(sl + 1) * LANES]
        for gi in range(len(POOL_WINDOWS)):
            upad[:, gi, POOL_PAD - POOL_HIST:POOL_PAD, :] = p0_ref[:, :, gi * POOL_GROUP_DIM:(gi + 1) * POOL_GROUP_DIM]
        wu_s[...] = win_ref[:, COL_POOL_RAW:COL_POOL_RAW + POOL_WIDTH]

    stage1 = lambda r0: _mixer_stage1(step, r0, sub, x_ref, nw_ref, win_ref, wu_s, wab_ref, sel_ref, cum_ref, cw_ref,
                                      alog_ref, dtb_ref, wp_ref, ps_ref, xpad, upad, qkv_s, d_s, st,
                                      nseq=nseq, ls=ls, pos0=pos0)
    stage2 = lambda r0: _mixer_stage2(r0, sub, states, x_ref, on_ref, wout_ref, y_ref, qkv_s, st,
                                      nseq=nseq, ls=ls, chunk=chunk)
    states = {(hh, s): s_scr[s, hh] for hh in range(DN_HEADS) for s in range(nseq)}
    for r0 in range(0, ls, sub):
        stage1(r0)
        stage2(r0)
    for (hh, s), state in states.items():
        s_scr[s, hh] = state
    sn_ref[...] = s_scr[...]

    for sl in range(CONV_CH // LANES):
        tail = xpad[:, sl, CONV_PAD + ls - (CONV_WIDTH - 1):CONV_PAD + ls, :]
        xpad[:, sl, CONV_PAD - (CONV_WIDTH - 1):CONV_PAD, :] = tail
        cn_ref[:, :, sl * LANES:(sl + 1) * LANES] = tail
    for gi in range(len(POOL_WINDOWS)):
        tail = upad[:, gi, POOL_PAD + ls - POOL_HIST:POOL_PAD + ls, :]
        upad[:, gi, POOL_PAD - POOL_HIST:POOL_PAD, :] = tail
        pn_ref[:, :, gi * POOL_GROUP_DIM:(gi + 1) * POOL_GROUP_DIM] = tail


def _chunk_sum_matrix(tile, chunk):
    j = np.arange(tile)[:, None]
    i = np.arange(tile)[None, :]
    same = (j // chunk) == (i // chunk)
    return jnp.asarray(np.concatenate([same & (j <= i), same], axis=1), dtype=BF16)


def _head_select_matrix():
    sel = np.zeros((2 * SUBLANES, 2 * DN_WIDTH), np.float32)
    for hh in range(DN_HEADS):
        sel[hh, hh * HEAD_DIM:(hh + 1) * HEAD_DIM] = 1.0
        sel[SUBLANES + hh, DN_WIDTH + hh * HEAD_DIM:DN_WIDTH + (hh + 1) * HEAD_DIM] = 1.0
    return jnp.asarray(sel, dtype=BF16)


def _mixer(x, params, states, new_states, *, layer, state_layer, nseq, ls, sub, chunk, pos0):
    nw, w_in, w_ab_t, conv_w, alog, dtb, o_norm, w_pool, pool_scale, w_out = params
    b, l, d = x.shape
    assert sub & (sub - 1) == 0 and chunk & (chunk - 1) == 0 and ls % sub == 0 and sub % chunk == 0
    assert b % nseq == 0 and l % ls == 0 and (nseq == 1 or ls == sub)
    tile = nseq * ls
    sel = _head_select_matrix()
    cum = _chunk_sum_matrix(nseq * sub, chunk)
    stacked = lambda a: _resident(a.shape[1:], layer)
    seq_block = lambda a, which: pl.BlockSpec((None, nseq) + a.shape[2:],
                                              lambda i, j: (which, i) + (0,) * (a.ndim - 2))
    x_spec = pl.BlockSpec((nseq, ls, d), lambda i, j: (i, j, 0))
    untouched = pl.BlockSpec(memory_space=pl.ANY)
    n_in = 16
    return pl.pallas_call(
        functools.partial(_mixer_kernel, nseq=nseq, ls=ls, sub=sub, chunk=chunk, pos0=pos0),
        grid=(b // nseq, l // ls),
        in_specs=[x_spec, stacked(nw), stacked(w_in), stacked(w_ab_t), _resident(sel.shape), _resident(cum.shape),
                  stacked(conv_w), stacked(alog), stacked(dtb), stacked(o_norm), stacked(w_pool),
                  stacked(pool_scale), stacked(w_out),
                  *[seq_block(a, state_layer) for a in states], untouched, untouched, untouched],
        out_specs=(x_spec, *[seq_block(a, layer) for a in new_states]),
        out_shape=(jax.ShapeDtypeStruct(x.shape, x.dtype),
                   *[jax.ShapeDtypeStruct(a.shape, a.dtype) for a in new_states]),
        input_output_aliases={n_in + k: 1 + k for k in range(len(new_states))},
        scratch_shapes=[pltpu.VMEM((nseq, DN_HEADS, HEAD_DIM, HEAD_DIM), F32),
                        pltpu.VMEM((nseq, CONV_CH // LANES, CONV_PAD + ls, LANES), F32),
                        pltpu.VMEM((nseq, len(POOL_WINDOWS), POOL_PAD + ls, POOL_GROUP_DIM), F32),
                        pltpu.VMEM((CONV_CH // LANES, tile, LANES), F32),
                        pltpu.VMEM((len(POOL_WINDOWS), tile, POOL_GROUP_DIM), F32),
                        pltpu.VMEM((d, POOL_WIDTH), BF16)] + list(_stage_buffers(tile).values()),
        compiler_params=pltpu.CompilerParams(dimension_semantics=("arbitrary", "arbitrary"),
                                             vmem_limit_bytes=VMEM_LIMIT_BYTES),
        name="mixer",
    )(x, nw, w_in, w_ab_t, sel, cum, conv_w, alog, dtb, o_norm, w_pool, pool_scale, w_out, *states, *new_states)


def _head_col(v):
    return jnp.pad(v.astype(F32), ((0, 0), (0, SUBLANES - DN_HEADS)))[:, :, None]


def kernel(x_prompt, x_sample, state_delta, state_conv, state_pool, norm_ffn1, w_ffn1_gate, w_ffn1_up, w_ffn1_down, norm_mix, w_in, conv_w, a_log, dt_bias, o_norm, w_pool, pool_scale, w_out, norm_ffn2, w_ffn2_gate, w_ffn2_up, w_ffn2_down, norm_final):
    depth = w_in.shape[0]
    batch, seq, d_model = x_prompt.shape
    dec_batch, dec_seq, _ = x_sample.shape

    w_in_b = w_in.astype(BF16)
    w_a_t = jnp.swapaxes(w_in[..., COL_POOL:COL_POOL + DN_HEADS], 1, 2)
    w_b_t = jnp.swapaxes(w_in[..., COL_POOL + DN_HEADS:COL_POOL_RAW], 1, 2)
    head_pad = jnp.zeros((depth, SUBLANES - DN_HEADS, d_model), w_in.dtype)
    w_ab_t = jnp.concatenate([w_a_t, head_pad, w_b_t, head_pad], axis=1).astype(BF16)
    row = lambda v: v[:, None, :]
    ffn1 = (row(norm_ffn1), w_ffn1_gate.astype(BF16), w_ffn1_up.astype(BF16), w_ffn1_down.astype(BF16))
    ffn2 = (row(norm_ffn2), w_ffn2_gate.astype(BF16), w_ffn2_up.astype(BF16), w_ffn2_down.astype(BF16))
    mix = (row(norm_mix), w_in_b, w_ab_t, conv_w, _head_col(a_log), _head_col(dt_bias), row(o_norm),
           w_pool.astype(BF16), row(pool_scale), w_out.astype(BF16))
    nf = norm_final[None, :]

    state_in = (state_delta, state_conv, state_pool)
    prompt = dict(nseq=batch, ls=PROMPT_TILE, sub=PROMPT_TILE, chunk=PROMPT_CHUNK, pos0=0, state_layer=lambda layer: 0,
                  states=tuple(jnp.zeros((1, batch) + s.shape[2:], s.dtype) for s in state_in))
    sample = dict(nseq=dec_batch, ls=dec_seq, sub=dec_seq, chunk=dec_seq, pos0=PAST_LEN, state_layer=lambda layer: layer,
                  states=state_in)
    xp, xs = x_prompt.reshape(batch * seq, d_model), x_sample.reshape(dec_batch * dec_seq, d_model)
    new_p = tuple(jnp.zeros((depth, batch) + s.shape[2:], s.dtype) for s in state_in)
    new_s = tuple(jnp.zeros_like(s) for s in state_in)
    for layer in range(depth):
        xp, xs = _ffn(xp, xs, *ffn1, layer=layer, tm=FFN_TILE)
        outs = []
        for grp, x, new in ((prompt, xp.reshape(x_prompt.shape), new_p), (sample, xs.reshape(x_sample.shape), new_s)):
            outs.append(_mixer(x, mix, grp["states"], new, layer=layer, state_layer=grp["state_layer"](layer),
                               nseq=grp["nseq"], ls=grp["ls"], sub=grp["sub"], chunk=grp["chunk"], pos0=grp["pos0"]))
        (xp, *new_p), (xs, *new_s) = outs
        fw = nf if layer == depth - 1 else None
        xp, xs = _ffn(xp.reshape(batch * seq, d_model), xs.reshape(dec_batch * dec_seq, d_model), *ffn2, fw,
                      layer=layer, tm=FFN_TILE)
    return (xp.reshape(x_prompt.shape), xs.reshape(x_sample.shape), *new_p, *new_s)
```

```python
import functools

import numpy as np
import jax
import jax.numpy as jnp
from jax import lax
from jax.experimental import pallas as pl
from jax.experimental.pallas import tpu as pltpu

EPS = 1e-6
PAST_LEN = 4096
DN_HEADS = 4
HEAD_DIM = 128
DN_WIDTH = DN_HEADS * HEAD_DIM
CONV_WIDTH = 4
CONV_CH = 3 * DN_WIDTH
POOL_WINDOWS = (2, 4, 8, 16)
POOL_GROUP_DIM = 128
POOL_WIDTH = len(POOL_WINDOWS) * POOL_GROUP_DIM
POOL_HIST = max(POOL_WINDOWS) - 1
PROMPT_CHUNK = 128
LANES = 128
SUBLANES = 8
INV_BASE = 16
ROW_STRIDE = 4
MIN_STRIDED_ROWS = ROW_STRIDE * SUBLANES
COL_GATE = CONV_CH
COL_POOL = COL_GATE + DN_WIDTH
COL_POOL_RAW = COL_POOL + 2 * DN_HEADS
CONV_PAD = SUBLANES
POOL_PAD = 2 * SUBLANES
ROW_N, ROW_ETOT, ROW_BETA, ROW_W = (SUBLANES * i for i in range(4))
N_ROW_FACTORS = 4
COL_N, COL_Q, COL_QDEC = (SUBLANES * i for i in range(3))
VMEM_LIMIT_BYTES = 56 * 1024 * 1024
FFN_TILE = 1024
PROMPT_TILE = 256

F32 = jnp.float32
BF16 = jnp.bfloat16
NT_DIMS = (((1,), (1,)), ((), ()))


def _rms(x, w):
    return x * lax.rsqrt(jnp.mean(x * x, axis=-1, keepdims=True) + EPS) * w


def _mm(a, b):
    return jnp.dot(a.astype(BF16), b.astype(BF16), preferred_element_type=F32)


def _resident(shape, layer=None):
    zeros = (0,) * len(shape)
    if layer is None:
        return pl.BlockSpec(shape, lambda *_: zeros, pipeline_mode=pl.Buffered(1))
    return pl.BlockSpec((None,) + shape, lambda *_: (layer,) + zeros, pipeline_mode=pl.Buffered(1))


def _ffn_kernel(*refs, f_chunk, n_main, final_norm):
    if final_norm:
        x_ref, xe_ref, nw_ref, wg_ref, wu_ref, wd_ref, fw_ref, o_ref, oe_ref, h_ref = refs
    else:
        x_ref, xe_ref, nw_ref, wg_ref, wu_ref, wd_ref, o_ref, oe_ref, h_ref = refs
    d_ff = wg_ref.shape[1]

    def tile(src_ref, dst_ref):
        rows = src_ref.shape[0]
        x = src_ref[...]
        xn = _rms(x, nw_ref[...]).astype(BF16)
        for c in range(d_ff // f_chunk):
            sl = slice(c * f_chunk, (c + 1) * f_chunk)
            g = jnp.dot(xn, wg_ref[:, sl], preferred_element_type=F32)
            u = jnp.dot(xn, wu_ref[:, sl], preferred_element_type=F32)
            h_ref[:rows, sl] = (g * jax.nn.sigmoid(g) * u).astype(BF16)
        y = x + 0.5 * jnp.dot(h_ref[:rows, :], wd_ref[...], preferred_element_type=F32)
        if final_norm:
            y = _rms(y, fw_ref[...])
        dst_ref[...] = y

    step = pl.program_id(0)
    pl.when(step < n_main)(lambda: tile(x_ref, o_ref))
    pl.when(step == n_main)(lambda: tile(xe_ref, oe_ref))


def _ffn(x, x_extra, nw, wg, wu, wd, fw=None, *, layer, tm):
    m, d = x.shape
    me = x_extra.shape[0]
    d_ff = wg.shape[-1]
    n_main = m // tm
    assert m % tm == 0 and me <= tm
    final_norm = fw is not None
    args = [x, x_extra, nw, wg, wu, wd] + ([fw] if final_norm else [])
    main_tile = lambda i: (jnp.minimum(i, n_main - 1), 0)
    in_specs = [pl.BlockSpec((tm, d), main_tile), pl.BlockSpec((me, d), lambda i: (0, 0)),
                _resident((1, d), layer), _resident((d, d_ff), layer), _resident((d, d_ff), layer),
                _resident((d_ff, d), layer)]
    in_specs += [_resident((1, d))] if final_norm else []
    return pl.pallas_call(
        functools.partial(_ffn_kernel, f_chunk=256, n_main=n_main, final_norm=final_norm),
        grid=(n_main + 1,),
        in_specs=in_specs,
        out_specs=(pl.BlockSpec((tm, d), main_tile), pl.BlockSpec((me, d), lambda i: (0, 0))),
        out_shape=(jax.ShapeDtypeStruct((m, d), x.dtype), jax.ShapeDtypeStruct((me, d), x.dtype)),
        scratch_shapes=[pltpu.VMEM((tm, d_ff), BF16)],
        compiler_params=pltpu.CompilerParams(dimension_semantics=("arbitrary",),
                                             vmem_limit_bytes=VMEM_LIMIT_BYTES),
        name="ffn",
    )(*args)


def _unit_lower_inverses(ns, chunk):
    r = lax.broadcasted_iota(jnp.int32, (chunk, chunk), 0)
    c = lax.broadcasted_iota(jnp.int32, (chunk, chunk), 1)
    base = min(chunk, INV_BASE)
    shift = base.bit_length() - 1
    if chunk > base:
        diag = (r >> shift) == (c >> shift)
        nds = [jnp.where(diag, n, 0.0) for n in ns]
    else:
        nds = ns
    eye = jnp.where(r == c, 1.0, 0.0)
    invs = [eye - nd for nd in nds]
    pws = [nd.astype(BF16) for nd in nds]
    pws = [jnp.dot(pw, pw, preferred_element_type=F32).astype(BF16) for pw in pws]
    order = 2
    while 2 * order < base:
        both = [jnp.dot(jnp.concatenate([inv.astype(BF16), pw], axis=0), pw, preferred_element_type=F32)
                for inv, pw in zip(invs, pws)]
        invs = [inv + bo[:chunk] for inv, bo in zip(invs, both)]
        pws = [bo[chunk:].astype(BF16) for bo in both]
        order *= 2
    invs = [inv + _mm(inv, pw) for inv, pw in zip(invs, pws)]
    blk = base
    while blk < chunk:
        s = blk.bit_length() - 1
        lower = ((r >> (s + 1)) == (c >> (s + 1))) & ((r >> s) > (c >> s))
        tmp = [_mm(inv, jnp.where(lower, n, 0.0)) for inv, n in zip(invs, ns)]
        invs = [inv - _mm(t, inv) for inv, t in zip(invs, tmp)]
        blk *= 2
    return invs


def _mixer_stage1(step, r0, hl, x_ref, nw_ref, win_ref, wu_s, wab_ref, sel_ref, cum_ref, cw_ref, alog_ref, dtb_ref,
                  wp_ref, ps_ref, xpad, upad, qkv_s, d_s, st, *, nseq, ls, pos0):
    tile = nseq * hl
    tr = slice(nseq * r0, nseq * r0 + tile)
    x = x_ref[:, r0:r0 + hl, :].reshape(tile, x_ref.shape[-1])
    h = _rms(x, nw_ref[...]).astype(BF16)
    project = lambda c0, c1: jnp.dot(h, win_ref[:, c0:c1], preferred_element_type=F32)

    def conv_group(part, name):
        pre = project(part * DN_WIDTH, (part + 1) * DN_WIDTH)
        for hh in range(DN_HEADS):
            sl = part * DN_HEADS + hh
            xpad[:, sl, CONV_PAD + r0:CONV_PAD + r0 + hl, :] = (
                pre[:, hh * LANES:(hh + 1) * LANES].reshape(nseq, hl, LANES))
            cw = cw_ref[:, sl * LANES:(sl + 1) * LANES]
            if hl >= MIN_STRIDED_ROWS:
                for s in range(nseq):
                    for r in range(ROW_STRIDE):
                        conv = None
                        for j in range(CONV_WIDTH):
                            rows_j = pl.ds(CONV_PAD + r0 - (CONV_WIDTH - 1) + j + r, hl // ROW_STRIDE,
                                           stride=ROW_STRIDE)
                            term = xpad[s, sl, rows_j, :] * cw[j:j + 1]
                            conv = term if conv is None else conv + term
                        rows_out = pl.ds(nseq * r0 + s * hl + r, hl // ROW_STRIDE, stride=ROW_STRIDE)
                        qkv_s[sl, rows_out, :] = conv * jax.nn.sigmoid(conv)
            else:
                conv = None
                for j in range(CONV_WIDTH):
                    start = CONV_PAD + r0 - (CONV_WIDTH - 1) + j
                    term = xpad[:, sl, start:start + hl, :].reshape(tile, LANES) * cw[j:j + 1]
                    conv = term if conv is None else conv + term
                qkv_s[sl, tr, :] = conv * jax.nn.sigmoid(conv)
            st[name][tr, hh * HEAD_DIM:(hh + 1) * HEAD_DIM] = qkv_s[sl, tr, :].astype(BF16)

    conv_group(0, "qb")
    conv_group(1, "kb")
    conv_group(2, "vb")
    gate = project(COL_GATE, COL_POOL)
    st["gact"][tr, :] = gate * jax.nn.sigmoid(gate)
    u = jnp.dot(h, wu_s[...], preferred_element_type=F32)
    for gi in range(len(POOL_WINDOWS)):
        upad[:, gi, POOL_PAD + r0:POOL_PAD + r0 + hl, :] = (
            u[:, gi * POOL_GROUP_DIM:(gi + 1) * POOL_GROUP_DIM].reshape(nseq, hl, POOL_GROUP_DIM))

    ab_t = lax.dot_general(wab_ref[...], h, NT_DIMS, preferred_element_type=F32)
    qk_part = jnp.concatenate([qkv_s[sl, tr, :] for sl in range(2 * DN_HEADS)], axis=1)
    ss_t = lax.dot_general(sel_ref[...], (qk_part * qk_part).astype(BF16), NT_DIMS,
                           preferred_element_type=F32)
    a_pre = ab_t[:SUBLANES] + dtb_ref[...]
    softplus = jnp.maximum(a_pre, 0.0) + jnp.log1p(jnp.exp(-jnp.abs(a_pre)))
    g = -jnp.exp(alog_ref[...]) * softplus
    beta = jax.nn.sigmoid(ab_t[SUBLANES:])
    r_q = lax.rsqrt(ss_t[:SUBLANES] + EPS) * (HEAD_DIM ** -0.5)
    r_k = lax.rsqrt(ss_t[SUBLANES:] + EPS)
    g_hi = g.astype(BF16).astype(F32)
    g_mid = (g - g_hi).astype(BF16).astype(F32)
    g_lo = g - g_hi - g_mid
    pieces = jnp.concatenate([g_hi, g_mid, g_lo, jnp.zeros_like(g)], axis=0).astype(BF16)
    sums = jnp.dot(pieces, cum_ref[...], preferred_element_type=F32)
    part = lambda k, half: sums[k * SUBLANES:(k + 1) * SUBLANES, half * tile:(half + 1) * tile]
    gc = part(0, 0) + part(1, 0) + part(2, 0)
    gs = part(0, 1) + part(1, 1) + part(2, 1)
    e_gc = jnp.exp(gc)
    row_n = gc - jnp.log(r_k)
    st["rowf"][:, tr] = jnp.concatenate([row_n, jnp.exp(gs), beta, r_k * (beta * e_gc)], axis=0)
    cols = [gc + jnp.log(beta * r_k), gc + jnp.log(r_q), r_q * e_gc]
    st["colf"][tr, :] = jnp.concatenate(cols + [jnp.zeros((LANES - SUBLANES * len(cols), tile), F32)], axis=0).T
    k_dec_row = r_k * jnp.exp(gs - gc)
    for hh in range(DN_HEADS):
        st["kdt"][hh, :, tr] = (qkv_s[DN_HEADS + hh, tr, :].T * k_dec_row[hh:hh + 1, :]).astype(BF16)

    first_pos = pos0 + step * ls + r0
    for gi, win in enumerate(POOL_WINDOWS):
        lanes = slice(gi * POOL_GROUP_DIM, (gi + 1) * POOL_GROUP_DIM)
        if hl >= MIN_STRIDED_ROWS:
            n_rows = hl // ROW_STRIDE
            for s in range(nseq):
                for r in range(ROW_STRIDE):
                    cur = upad[s, gi, pl.ds(POOL_PAD + r0 + r, n_rows, stride=ROW_STRIDE), :]
                    acc = cur
                    for back in range(1, win):
                        acc = acc + upad[s, gi, pl.ds(POOL_PAD + r0 + r - back, n_rows, stride=ROW_STRIDE), :]
                    pos = first_pos + r + ROW_STRIDE * lax.broadcasted_iota(jnp.int32, (n_rows, 1), 0)
                    cnt = jnp.minimum(win, pos + 1).astype(F32)
                    d_s[gi, pl.ds(nseq * r0 + s * hl + r, n_rows, stride=ROW_STRIDE), :] = acc / cnt - cur
            diff = d_s[gi, tr, :]
        else:
            acc = u[:, lanes]
            for back in range(1, win):
                start = POOL_PAD + r0 - back
                acc = acc + upad[:, gi, start:start + hl, :].reshape(tile, POOL_GROUP_DIM)
            pos = first_pos + (lax.broadcasted_iota(jnp.int32, (tile, 1), 0) & (hl - 1))
            cnt = jnp.minimum(win, pos + 1).astype(F32)
            diff = acc / cnt - u[:, lanes]
        st["z"][tr, lanes] = _mm(diff, wp_ref[gi]) * ps_ref[:, lanes]


def _mixer_stage2(r0, hl, states, x_ref, on_ref, wout_ref, y_ref, qkv_s, st, *, nseq, ls, chunk):
    tile = nseq * hl
    tr = slice(nseq * r0, nseq * r0 + tile)
    chunks = range(r0 // chunk, (r0 + hl) // chunk)
    cr = lax.broadcasted_iota(jnp.int32, (chunk, chunk), 0)
    cc = lax.broadcasted_iota(jnp.int32, (chunk, chunk), 1)
    incl = cr >= cc
    strict = cr > cc

    probs = [(hh, s, ci) for hh in range(DN_HEADS) for s in range(nseq) for ci in chunks]
    rows = lambda pr: slice(pr[1] * ls + pr[2] * chunk, pr[1] * ls + (pr[2] + 1) * chunk)
    head = lambda pr: slice(pr[0] * HEAD_DIM, (pr[0] + 1) * HEAD_DIM)
    col = lambda base, pr: st["colf"][rows(pr), base + pr[0]:base + pr[0] + 1]
    row = lambda base, pr: st["rowf"][base + pr[0]:base + pr[0] + 1, rows(pr)]
    kq = [lax.dot_general(jnp.concatenate([st["kb"][rows(pr), head(pr)], st["qb"][rows(pr), head(pr)]], axis=0),
                          st["kb"][rows(pr), head(pr)], NT_DIMS, preferred_element_type=F32) for pr in probs]
    ns = [jnp.where(strict, kq_p[:chunk] * jnp.exp(jnp.where(strict, col(COL_N, pr) - row(ROW_N, pr), -1e30)), 0.0)
          for pr, kq_p in zip(probs, kq)]
    qkd = [kq_p[chunk:] * jnp.exp(jnp.where(incl, col(COL_Q, pr) - row(ROW_N, pr), -1e30))
           for pr, kq_p in zip(probs, kq)]
    a_inv = _unit_lower_inverses(ns, chunk)
    sol = [jnp.concatenate([_mm(inv_p * row(ROW_BETA, pr), st["vb"][rows(pr), head(pr)]),
                            _mm(inv_p * row(ROW_W, pr), st["kb"][rows(pr), head(pr)])], axis=1).astype(BF16)
           for pr, inv_p in zip(probs, a_inv)]
    qw = [_mm(qkd_p, sol_p) for qkd_p, sol_p in zip(qkd, sol)]
    kw = [jnp.dot(st["kdt"][pr[0], :, rows(pr)], sol_p, preferred_element_type=F32) for pr, sol_p in zip(probs, sol)]

    lhs = {pr: jnp.concatenate([qkv_s[pr[0], rows(pr), :] * col(COL_QDEC, pr) - qw_p[:, HEAD_DIM:],
                                kw_p[:, HEAD_DIM:]], axis=0).astype(BF16)
           for pr, qw_p, kw_p in zip(probs, qw, kw)}
    o_add = {pr: qw_p[:, :HEAD_DIM] for pr, qw_p in zip(probs, qw)}
    s_add = {pr: kw_p[:, :HEAD_DIM] for pr, kw_p in zip(probs, kw)}
    o_chunk = {}
    for ci in chunks:
        for hh in range(DN_HEADS):
            for s in range(nseq):
                pr = (hh, s, ci)
                both = jnp.dot(lhs[pr], states[hh, s].astype(BF16), preferred_element_type=F32)
                o_chunk[pr] = o_add[pr] + both[:chunk]
                first = s * ls + ci * chunk
                e_chunk = st["rowf"][ROW_ETOT + hh:ROW_ETOT + hh + 1, first:first + 1]
                states[hh, s] = states[hh, s] * e_chunk - both[chunk:] + s_add[pr]
    o_heads = []
    for hh in range(DN_HEADS):
        o_parts = [o_chunk[hh, s, ci] for s in range(nseq) for ci in chunks]
        o_h = jnp.concatenate(o_parts, axis=0) if len(o_parts) > 1 else o_parts[0]
        o_heads.append(_rms(o_h, on_ref[...]) * st["gact"][tr, hh * HEAD_DIM:(hh + 1) * HEAD_DIM])

    mixed = jnp.concatenate(o_heads + [st["z"][tr, :]], axis=1).astype(BF16)
    x = x_ref[:, r0:r0 + hl, :].reshape(tile, x_ref.shape[-1])
    y = x + jnp.dot(mixed, wout_ref[...], preferred_element_type=F32)
    y_ref[:, r0:r0 + hl, :] = y.reshape(nseq, hl, y.shape[-1])


def _stage_buffers(tile):
    return {
        "qb": pltpu.VMEM((tile, DN_WIDTH), BF16), "kb": pltpu.VMEM((tile, DN_WIDTH), BF16),
        "vb": pltpu.VMEM((tile, DN_WIDTH), BF16), "kdt": pltpu.VMEM((DN_HEADS, HEAD_DIM, tile), BF16),
        "gact": pltpu.VMEM((tile, DN_WIDTH), F32), "z": pltpu.VMEM((tile, POOL_WIDTH), F32),
        "colf": pltpu.VMEM((tile, LANES), F32), "rowf": pltpu.VMEM((N_ROW_FACTORS * SUBLANES, tile), F32),
    }


def _mixer_kernel(x_ref, nw_ref, win_ref, wab_ref, sel_ref, cum_ref, cw_ref, alog_ref, dtb_ref, on_ref,
                  wp_ref, ps_ref, wout_ref, s0_ref, c0_ref, p0_ref, sprev_ref, cprev_ref, pprev_ref,
                  y_ref, sn_ref, cn_ref, pn_ref,
                  s_scr, xpad, upad, qkv_s, d_s, wu_s, *stage_refs, nseq, ls, sub, chunk, pos0):
    del sprev_ref, cprev_ref, pprev_ref
    step = pl.program_id(1)
    st = dict(zip(_stage_buffers(nseq * ls), stage_refs))

    @pl.when(step == 0)
    def _():
        s_scr[...] = s0_ref[...]
        for sl in range(CONV_CH // LANES):
            xpad[:, sl, CONV_PAD - (CONV_WIDTH - 1):CONV_PAD, :] = c0_ref[:, :, sl * LANES:(sl + 1) * LANES]
        for gi in range(len(POOL_WINDOWS)):
            upad[:, gi, POOL_PAD - POOL_HIST:POOL_PAD, :] = p0_ref[:, :, gi * POOL_GROUP_DIM:(gi + 1) * POOL_GROUP_DIM]
        wu_s[...] = win_ref[:, COL_POOL_RAW:COL_POOL_RAW + POOL_WIDTH]

    stage1 = lambda r0: _mixer_stage1(step, r0, sub, x_ref, nw_ref, win_ref, wu_s, wab_ref, sel_ref, cum_ref, cw_ref,
                                      alog_ref, dtb_ref, wp_ref, ps_ref, xpad, upad, qkv_s, d_s, st,
                                      nseq=nseq, ls=ls, pos0=pos0)
    stage2 = lambda r0: _mixer_stage2(r0, sub, states, x_ref, on_ref, wout_ref, y_ref, qkv_s, st,
                                      nseq=nseq, ls=ls, chunk=chunk)
    states = {(hh, s): s_scr[s, hh] for hh in range(DN_HEADS) for s in range(nseq)}
    for r0 in range(0, ls, sub):
        stage1(r0)
        stage2(r0)
    for (hh, s), state in states.items():
        s_scr[s, hh] = state
    sn_ref[...] = s_scr[...]

    for sl in range(CONV_CH // LANES):
        tail = xpad[:, sl, CONV_PAD + ls - (CONV_WIDTH - 1):CONV_PAD + ls, :]
        xpad[:, sl, CONV_PAD - (CONV_WIDTH - 1):CONV_PAD, :] = tail
        cn_ref[:, :, sl * LANES:(sl + 1) * LANES] = tail
    for gi in range(len(POOL_WINDOWS)):
        tail = upad[:, gi, POOL_PAD + ls - POOL_HIST:POOL_PAD + ls, :]
        upad[:, gi, POOL_PAD - POOL_HIST:POOL_PAD, :] = tail
        pn_ref[:, :, gi * POOL_GROUP_DIM:(gi + 1) * POOL_GROUP_DIM] = tail


def _chunk_sum_matrix(tile, chunk):
    j = np.arange(tile)[:, None]
    i = np.arange(tile)[None, :]
    same = (j // chunk) == (i // chunk)
    return jnp.asarray(np.concatenate([same & (j <= i), same], axis=1), dtype=BF16)


def _head_select_matrix():
    sel = np.zeros((2 * SUBLANES, 2 * DN_WIDTH), np.float32)
    for hh in range(DN_HEADS):
        sel[hh, hh * HEAD_DIM:(hh + 1) * HEAD_DIM] = 1.0
        sel[SUBLANES + hh, DN_WIDTH + hh * HEAD_DIM:DN_WIDTH + (hh + 1) * HEAD_DIM] = 1.0
    return jnp.asarray(sel, dtype=BF16)


def _mixer(x, params, states, new_states, *, layer, state_layer, nseq, ls, sub, chunk, pos0):
    nw, w_in, w_ab_t, conv_w, alog, dtb, o_norm, w_pool, pool_scale, w_out = params
    b, l, d = x.shape
    assert sub & (sub - 1) == 0 and chunk & (chunk - 1) == 0 and ls % sub == 0 and sub % chunk == 0
    assert b % nseq == 0 and l % ls == 0 and (nseq == 1 or ls == sub)
    tile = nseq * ls
    sel = _head_select_matrix()
    cum = _chunk_sum_matrix(nseq * sub, chunk)
    stacked = lambda a: _resident(a.shape[1:], layer)
    seq_block = lambda a, which: pl.BlockSpec((None, nseq) + a.shape[2:],
                                              lambda i, j: (which, i) + (0,) * (a.ndim - 2))
    x_spec = pl.BlockSpec((nseq, ls, d), lambda i, j: (i, j, 0))
    untouched = pl.BlockSpec(memory_space=pl.ANY)
    n_in = 16
    return pl.pallas_call(
        functools.partial(_mixer_kernel, nseq=nseq, ls=ls, sub=sub, chunk=chunk, pos0=pos0),
        grid=(b // nseq, l // ls),
        in_specs=[x_spec, stacked(nw), stacked(w_in), stacked(w_ab_t), _resident(sel.shape), _resident(cum.shape),
                  stacked(conv_w), stacked(alog), stacked(dtb), stacked(o_norm), stacked(w_pool),
                  stacked(pool_scale), stacked(w_out),
                  *[seq_block(a, state_layer) for a in states], untouched, untouched, untouched],
        out_specs=(x_spec, *[seq_block(a, layer) for a in new_states]),
        out_shape=(jax.ShapeDtypeStruct(x.shape, x.dtype),
                   *[jax.ShapeDtypeStruct(a.shape, a.dtype) for a in new_states]),
        input_output_aliases={n_in + k: 1 + k for k in range(len(new_states))},
        scratch_shapes=[pltpu.VMEM((nseq, DN_HEADS, HEAD_DIM, HEAD_DIM), F32),
                        pltpu.VMEM((nseq, CONV_CH // LANES, CONV_PAD + ls, LANES), F32),
                        pltpu.VMEM((nseq, len(POOL_WINDOWS), POOL_PAD + ls, POOL_GROUP_DIM), F32),
                        pltpu.VMEM((CONV_CH // LANES, tile, LANES), F32),
                        pltpu.VMEM((len(POOL_WINDOWS), tile, POOL_GROUP_DIM), F32),
                        pltpu.VMEM((d, POOL_WIDTH), BF16)] + list(_stage_buffers(tile).values()),
        compiler_params=pltpu.CompilerParams(dimension_semantics=("arbitrary", "arbitrary"),
                                             vmem_limit_bytes=VMEM_LIMIT_BYTES),
        name="mixer",
    )(x, nw, w_in, w_ab_t, sel, cum, conv_w, alog, dtb, o_norm, w_pool, pool_scale, w_out, *states, *new_states)


def _head_col(v):
    return jnp.pad(v.astype(F32), ((0, 0), (0, SUBLANES - DN_HEADS)))[:, :, None]


def kernel(x_prompt, x_sample, state_delta, state_conv, state_pool, norm_ffn1, w_ffn1_gate, w_ffn1_up, w_ffn1_down, norm_mix, w_in, conv_w, a_log, dt_bias, o_norm, w_pool, pool_scale, w_out, norm_ffn2, w_ffn2_gate, w_ffn2_up, w_ffn2_down, norm_final):
    depth = w_in.shape[0]
    batch, seq, d_model = x_prompt.shape
    dec_batch, dec_seq, _ = x_sample.shape

    w_in_b = jnp.pad(w_in, ((0, 0), (0, 0), (0, -w_in.shape[-1] % LANES))).astype(BF16)
    w_a_t = jnp.swapaxes(w_in[..., COL_POOL:COL_POOL + DN_HEADS], 1, 2)
    w_b_t = jnp.swapaxes(w_in[..., COL_POOL + DN_HEADS:COL_POOL_RAW], 1, 2)
    head_pad = jnp.zeros((depth, SUBLANES - DN_HEADS, d_model), w_in.dtype)
    w_ab_t = jnp.concatenate([w_a_t, head_pad, w_b_t, head_pad], axis=1).astype(BF16)
    row = lambda v: v[:, None, :]
    ffn1 = (row(norm_ffn1), w_ffn1_gate.astype(BF16), w_ffn1_up.astype(BF16), w_ffn1_down.astype(BF16))
    ffn2 = (row(norm_ffn2), w_ffn2_gate.astype(BF16), w_ffn2_up.astype(BF16), w_ffn2_down.astype(BF16))
    mix = (row(norm_mix), w_in_b, w_ab_t, conv_w, _head_col(a_log), _head_col(dt_bias), row(o_norm),
           w_pool.astype(BF16), row(pool_scale), w_out.astype(BF16))
    nf = norm_final[None, :]

    state_in = (state_delta, state_conv, state_pool)
    prompt = dict(nseq=batch, ls=PROMPT_TILE, sub=PROMPT_TILE, chunk=PROMPT_CHUNK, pos0=0, state_layer=lambda layer: 0,
                  states=tuple(jnp.zeros((1, batch) + s.shape[2:], s.dtype) for s in state_in))
    sample = dict(nseq=dec_batch, ls=dec_seq, sub=dec_seq, chunk=dec_seq, pos0=PAST_LEN, state_layer=lambda layer: layer,
                  states=state_in)
    xp, xs = x_prompt.reshape(batch * seq, d_model), x_sample.reshape(dec_batch * dec_seq, d_model)
    new_p = tuple(jnp.zeros((depth, batch) + s.shape[2:], s.dtype) for s in state_in)
    new_s = tuple(jnp.zeros_like(s) for s in state_in)
    for layer in range(depth):
        xp, xs = _ffn(xp, xs, *ffn1, layer=layer, tm=FFN_TILE)
        outs = []
        for grp, x, new in ((prompt, xp.reshape(x_prompt.shape), new_p), (sample, xs.reshape(x_sample.shape), new_s)):
            outs.append(_mixer(x, mix, grp["states"], new, layer=layer, state_layer=grp["state_layer"](layer),
                               nseq=grp["nseq"], ls=grp["ls"], sub=grp["sub"], chunk=grp["chunk"], pos0=grp["pos0"]))
        (xp, *new_p), (xs, *new_s) = outs
        fw = nf if layer == depth - 1 else None
        xp, xs = _ffn(xp.reshape(batch * seq, d_model), xs.reshape(dec_batch * dec_seq, d_model), *ffn2, fw,
                      layer=layer, tm=FFN_TILE)
    return (xp.reshape(x_prompt.shape), xs.reshape(x_sample.shape), *new_p, *new_s)
```

```python
import functools

import numpy as np
import jax
import jax.numpy as jnp
from jax import lax
from jax.experimental import pallas as pl
from jax.experimental.pallas import tpu as pltpu

EPS = 1e-6
PAST_LEN = 4096
DN_HEADS = 4
HEAD_DIM = 128
DN_WIDTH = DN_HEADS * HEAD_DIM
CONV_WIDTH = 4
CONV_CH = 3 * DN_WIDTH
POOL_WINDOWS = (2, 4, 8, 16)
POOL_GROUP_DIM = 128
POOL_WIDTH = len(POOL_WINDOWS) * POOL_GROUP_DIM
POOL_HIST = max(POOL_WINDOWS) - 1
PROMPT_CHUNK = 128
LANES = 128
SUBLANES = 8
INV_BASE = 16
ROW_STRIDE = 2
MIN_STRIDED_ROWS = ROW_STRIDE * SUBLANES
COL_GATE = CONV_CH
COL_POOL = COL_GATE + DN_WIDTH
COL_POOL_RAW = COL_POOL + 2 * DN_HEADS
CONV_PAD = SUBLANES
POOL_PAD = 2 * SUBLANES
ROW_N, ROW_ETOT, ROW_BETA, ROW_W = (SUBLANES * i for i in range(4))
N_ROW_FACTORS = 4
COL_N, COL_Q, COL_QDEC = (SUBLANES * i for i in range(3))
VMEM_LIMIT_BYTES = 56 * 1024 * 1024
FFN_TILE = 1024
PROMPT_TILE = 256

F32 = jnp.float32
BF16 = jnp.bfloat16
NT_DIMS = (((1,), (1,)), ((), ()))


def _rms(x, w):
    return x * lax.rsqrt(jnp.mean(x * x, axis=-1, keepdims=True) + EPS) * w


def _mm(a, b):
    return jnp.dot(a.astype(BF16), b.astype(BF16), preferred_element_type=F32)


def _resident(shape, layer=None):
    zeros = (0,) * len(shape)
    if layer is None:
        return pl.BlockSpec(shape, lambda *_: zeros, pipeline_mode=pl.Buffered(1))
    return pl.BlockSpec((None,) + shape, lambda *_: (layer,) + zeros, pipeline_mode=pl.Buffered(1))


def _ffn_kernel(*refs, f_chunk, n_main, final_norm):
    if final_norm:
        x_ref, xe_ref, nw_ref, wg_ref, wu_ref, wd_ref, fw_ref, o_ref, oe_ref, h_ref = refs
    else:
        x_ref, xe_ref, nw_ref, wg_ref, wu_ref, wd_ref, o_ref, oe_ref, h_ref = refs
    d_ff = wg_ref.shape[1]

    def tile(src_ref, dst_ref):
        rows = src_ref.shape[0]
        x = src_ref[...]
        xn = _rms(x, nw_ref[...]).astype(BF16)
        for c in range(d_ff // f_chunk):
            sl = slice(c * f_chunk, (c + 1) * f_chunk)
            g = jnp.dot(xn, wg_ref[:, sl], preferred_element_type=F32)
            u = jnp.dot(xn, wu_ref[:, sl], preferred_element_type=F32)
            h_ref[:rows, sl] = (g * jax.nn.sigmoid(g) * u).astype(BF16)
        y = x + 0.5 * jnp.dot(h_ref[:rows, :], wd_ref[...], preferred_element_type=F32)
        if final_norm:
            y = _rms(y, fw_ref[...])
        dst_ref[...] = y

    step = pl.program_id(0)
    pl.when(step < n_main)(lambda: tile(x_ref, o_ref))
    pl.when(step == n_main)(lambda: tile(xe_ref, oe_ref))


def _ffn(x, x_extra, nw, wg, wu, wd, fw=None, *, layer, tm):
    m, d = x.shape
    me = x_extra.shape[0]
    d_ff = wg.shape[-1]
    n_main = m // tm
    assert m % tm == 0 and me <= tm
    final_norm = fw is not None
    args = [x, x_extra, nw, wg, wu, wd] + ([fw] if final_norm else [])
    main_tile = lambda i: (jnp.minimum(i, n_main - 1), 0)
    in_specs = [pl.BlockSpec((tm, d), main_tile), pl.BlockSpec((me, d), lambda i: (0, 0)),
                _resident((1, d), layer), _resident((d, d_ff), layer), _resident((d, d_ff), layer),
                _resident((d_ff, d), layer)]
    in_specs += [_resident((1, d))] if final_norm else []
    return pl.pallas_call(
        functools.partial(_ffn_kernel, f_chunk=256, n_main=n_main, final_norm=final_norm),
        grid=(n_main + 1,),
        in_specs=in_specs,
        out_specs=(pl.BlockSpec((tm, d), main_tile), pl.BlockSpec((me, d), lambda i: (0, 0))),
        out_shape=(jax.ShapeDtypeStruct((m, d), x.dtype), jax.ShapeDtypeStruct((me, d), x.dtype)),
        scratch_shapes=[pltpu.VMEM((tm, d_ff), BF16)],
        compiler_params=pltpu.CompilerParams(dimension_semantics=("arbitrary",),
                                             vmem_limit_bytes=VMEM_LIMIT_BYTES),
        name="ffn",
    )(*args)


def _unit_lower_inverses(ns, chunk):
    r = lax.broadcasted_iota(jnp.int32, (chunk, chunk), 0)
    c = lax.broadcasted_iota(jnp.int32, (chunk, chunk), 1)
    base = min(chunk, INV_BASE)
    shift = base.bit_length() - 1
    if chunk > base:
        diag = (r >> shift) == (c >> shift)
        nds = [jnp.where(diag, n, 0.0) for n in ns]
    else:
        nds = ns
    eye = jnp.where(r == c, 1.0, 0.0)
    invs = [eye - nd for nd in nds]
    pws = [nd.astype(BF16) for nd in nds]
    pws = [jnp.dot(pw, pw, preferred_element_type=F32).astype(BF16) for pw in pws]
    order = 2
    while 2 * order < base:
        both = [jnp.dot(jnp.concatenate([inv.astype(BF16), pw], axis=0), pw, preferred_element_type=F32)
                for inv, pw in zip(invs, pws)]
        invs = [inv + bo[:chunk] for inv, bo in zip(invs, both)]
        pws = [bo[chunk:].astype(BF16) for bo in both]
        order *= 2
    invs = [inv + _mm(inv, pw) for inv, pw in zip(invs, pws)]
    blk = base
    while blk < chunk:
        s = blk.bit_length() - 1
        lower = ((r >> (s + 1)) == (c >> (s + 1))) & ((r >> s) > (c >> s))
        tmp = [_mm(inv, jnp.where(lower, n, 0.0)) for inv, n in zip(invs, ns)]
        invs = [inv - _mm(t, inv) for inv, t in zip(invs, tmp)]
        blk *= 2
    return invs


def _mixer_stage1(step, r0, hl, x_ref, nw_ref, win_ref, wu_s, wab_ref, sel_ref, cum_ref, cw_ref, alog_ref, dtb_ref,
                  wp_ref, ps_ref, xpad, upad, qkv_s, d_s, st, *, nseq, ls, pos0):
    tile = nseq * hl
    tr = slice(nseq * r0, nseq * r0 + tile)
    x = x_ref[:, r0:r0 + hl, :].reshape(tile, x_ref.shape[-1])
    h = _rms(x, nw_ref[...]).astype(BF16)
    project = lambda c0, c1: jnp.dot(h, win_ref[:, c0:c1], preferred_element_type=F32)

    def conv_group(part, name):
        pre = project(part * DN_WIDTH, (part + 1) * DN_WIDTH)
        for hh in range(DN_HEADS):
            sl = part * DN_HEADS + hh
            xpad[:, sl, CONV_PAD + r0:CONV_PAD + r0 + hl, :] = (
                pre[:, hh * LANES:(hh + 1) * LANES].reshape(nseq, hl, LANES))
            cw = cw_ref[:, sl * LANES:(sl + 1) * LANES]
            if hl >= MIN_STRIDED_ROWS:
                for s in range(nseq):
                    for r in range(ROW_STRIDE):
                        conv = None
                        for j in range(CONV_WIDTH):
                            rows_j = pl.ds(CONV_PAD + r0 - (CONV_WIDTH - 1) + j + r, hl // ROW_STRIDE,
                                           stride=ROW_STRIDE)
                            term = xpad[s, sl, rows_j, :] * cw[j:j + 1]
                            conv = term if conv is None else conv + term
                        rows_out = pl.ds(nseq * r0 + s * hl + r, hl // ROW_STRIDE, stride=ROW_STRIDE)
                        qkv_s[sl, rows_out, :] = conv * jax.nn.sigmoid(conv)
            else:
                conv = None
                for j in range(CONV_WIDTH):
                    start = CONV_PAD + r0 - (CONV_WIDTH - 1) + j
                    term = xpad[:, sl, start:start + hl, :].reshape(tile, LANES) * cw[j:j + 1]
                    conv = term if conv is None else conv + term
                qkv_s[sl, tr, :] = conv * jax.nn.sigmoid(conv)
            st[name][tr, hh * HEAD_DIM:(hh + 1) * HEAD_DIM] = qkv_s[sl, tr, :].astype(BF16)

    conv_group(0, "qb")
    conv_group(1, "kb")
    conv_group(2, "vb")
    gate = project(COL_GATE, COL_POOL)
    st["gact"][tr, :] = gate * jax.nn.sigmoid(gate)
    u = jnp.dot(h, wu_s[...], preferred_element_type=F32)
    for gi in range(len(POOL_WINDOWS)):
        upad[:, gi, POOL_PAD + r0:POOL_PAD + r0 + hl, :] = (
            u[:, gi * POOL_GROUP_DIM:(gi + 1) * POOL_GROUP_DIM].reshape(nseq, hl, POOL_GROUP_DIM))

    ab_t = lax.dot_general(wab_ref[...], h, NT_DIMS, preferred_element_type=F32)
    qk_part = jnp.concatenate([qkv_s[sl, tr, :] for sl in range(2 * DN_HEADS)], axis=1)
    ss_t = lax.dot_general(sel_ref[...], (qk_part * qk_part).astype(BF16), NT_DIMS,
                           preferred_element_type=F32)
    a_pre = ab_t[:SUBLANES] + dtb_ref[...]
    softplus = jnp.maximum(a_pre, 0.0) + jnp.log1p(jnp.exp(-jnp.abs(a_pre)))
    g = -jnp.exp(alog_ref[...]) * softplus
    beta = jax.nn.sigmoid(ab_t[SUBLANES:])
    r_q = lax.rsqrt(ss_t[:SUBLANES] + EPS) * (HEAD_DIM ** -0.5)
    r_k = lax.rsqrt(ss_t[SUBLANES:] + EPS)
    g_hi = g.astype(BF16).astype(F32)
    g_mid = (g - g_hi).astype(BF16).astype(F32)
    g_lo = g - g_hi - g_mid
    pieces = jnp.concatenate([g_hi, g_mid, g_lo, jnp.zeros_like(g)], axis=0).astype(BF16)
    sums = jnp.dot(pieces, cum_ref[...], preferred_element_type=F32)
    part = lambda k, half: sums[k * SUBLANES:(k + 1) * SUBLANES, half * tile:(half + 1) * tile]
    gc = part(0, 0) + part(1, 0) + part(2, 0)
    gs = part(0, 1) + part(1, 1) + part(2, 1)
    e_gc = jnp.exp(gc)
    row_n = gc - jnp.log(r_k)
    st["rowf"][:, tr] = jnp.concatenate([row_n, jnp.exp(gs), beta, r_k * (beta * e_gc)], axis=0)
    cols = [gc + jnp.log(beta * r_k), gc + jnp.log(r_q), r_q * e_gc]
    st["colf"][tr, :] = jnp.concatenate(cols + [jnp.zeros((LANES - SUBLANES * len(cols), tile), F32)], axis=0).T
    k_dec_row = r_k * jnp.exp(gs - gc)
    for hh in range(DN_HEADS):
        st["kdt"][hh, :, tr] = (qkv_s[DN_HEADS + hh, tr, :].T * k_dec_row[hh:hh + 1, :]).astype(BF16)

    first_pos = pos0 + step * ls + r0
    for gi, win in enumerate(POOL_WINDOWS):
        lanes = slice(gi * POOL_GROUP_DIM, (gi + 1) * POOL_GROUP_DIM)
        if hl >= MIN_STRIDED_ROWS:
            n_rows = hl // ROW_STRIDE
            for s in range(nseq):
                for r in range(ROW_STRIDE):
                    cur = upad[s, gi, pl.ds(POOL_PAD + r0 + r, n_rows, stride=ROW_STRIDE), :]
                    acc = cur
                    for back in range(1, win):
                        acc = acc + upad[s, gi, pl.ds(POOL_PAD + r0 + r - back, n_rows, stride=ROW_STRIDE), :]
                    pos = first_pos + r + ROW_STRIDE * lax.broadcasted_iota(jnp.int32, (n_rows, 1), 0)
                    cnt = jnp.minimum(win, pos + 1).astype(F32)
                    d_s[gi, pl.ds(nseq * r0 + s * hl + r, n_rows, stride=ROW_STRIDE), :] = acc / cnt - cur
            diff = d_s[gi, tr, :]
        else:
            acc = u[:, lanes]
            for back in range(1, win):
                start = POOL_PAD + r0 - back
                acc = acc + upad[:, gi, start:start + hl, :].reshape(tile, POOL_GROUP_DIM)
            pos = first_pos + (lax.broadcasted_iota(jnp.int32, (tile, 1), 0) & (hl - 1))
            cnt = jnp.minimum(win, pos + 1).astype(F32)
            diff = acc / cnt - u[:, lanes]
        st["z"][tr, lanes] = _mm(diff, wp_ref[gi]) * ps_ref[:, lanes]


def _mixer_stage2(r0, hl, states, x_ref, on_ref, wout_ref, y_ref, qkv_s, st, *, nseq, ls, chunk):
    tile = nseq * hl
    tr = slice(nseq * r0, nseq * r0 + tile)
    chunks = range(r0 // chunk, (r0 + hl) // chunk)
    cr = lax.broadcasted_iota(jnp.int32, (chunk, chunk), 0)
    cc = lax.broadcasted_iota(jnp.int32, (chunk, chunk), 1)
    incl = cr >= cc
    strict = cr > cc

    probs = [(hh, s, ci) for hh in range(DN_HEADS) for s in range(nseq) for ci in chunks]
    rows = lambda pr: slice(pr[1] * ls + pr[2] * chunk, pr[1] * ls + (pr[2] + 1) * chunk)
    head = lambda pr: slice(pr[0] * HEAD_DIM, (pr[0] + 1) * HEAD_DIM)
    col = lambda base, pr: st["colf"][rows(pr), base + pr[0]:base + pr[0] + 1]
    row = lambda base, pr: st["rowf"][base + pr[0]:base + pr[0] + 1, rows(pr)]
    kq = [lax.dot_general(jnp.concatenate([st["kb"][rows(pr), head(pr)], st["qb"][rows(pr), head(pr)]], axis=0),
                          st["kb"][rows(pr), head(pr)], NT_DIMS, preferred_element_type=F32) for pr in probs]
    ns = [jnp.where(strict, kq_p[:chunk] * jnp.exp(jnp.where(strict, col(COL_N, pr) - row(ROW_N, pr), -1e30)), 0.0)
          for pr, kq_p in zip(probs, kq)]
    qkd = [kq_p[chunk:] * jnp.exp(jnp.where(incl, col(COL_Q, pr) - row(ROW_N, pr), -1e30))
           for pr, kq_p in zip(probs, kq)]
    a_inv = _unit_lower_inverses(ns, chunk)
    sol = [jnp.concatenate([_mm(inv_p * row(ROW_BETA, pr), st["vb"][rows(pr), head(pr)]),
                            _mm(inv_p * row(ROW_W, pr), st["kb"][rows(pr), head(pr)])], axis=1).astype(BF16)
           for pr, inv_p in zip(probs, a_inv)]
    qw = [_mm(qkd_p, sol_p) for qkd_p, sol_p in zip(qkd, sol)]
    kw = [jnp.dot(st["kdt"][pr[0], :, rows(pr)], sol_p, preferred_element_type=F32) for pr, sol_p in zip(probs, sol)]

    lhs = {pr: jnp.concatenate([qkv_s[pr[0], rows(pr), :] * col(COL_QDEC, pr) - qw_p[:, HEAD_DIM:],
                                kw_p[:, HEAD_DIM:]], axis=0).astype(BF16)
           for pr, qw_p, kw_p in zip(probs, qw, kw)}
    o_add = {pr: qw_p[:, :HEAD_DIM] for pr, qw_p in zip(probs, qw)}
    s_add = {pr: kw_p[:, :HEAD_DIM] for pr, kw_p in zip(probs, kw)}
    o_chunk = {}
    for ci in chunks:
        for hh in range(DN_HEADS):
            for s in range(nseq):
                pr = (hh, s, ci)
                both = jnp.dot(lhs[pr], states[hh, s].astype(BF16), preferred_element_type=F32)
                o_chunk[pr] = o_add[pr] + both[:chunk]
                first = s * ls + ci * chunk
                e_chunk = st["rowf"][ROW_ETOT + hh:ROW_ETOT + hh + 1, first:first + 1]
                states[hh, s] = states[hh, s] * e_chunk - both[chunk:] + s_add[pr]
    o_heads = []
    for hh in range(DN_HEADS):
        o_parts = [o_chunk[hh, s, ci] for s in range(nseq) for ci in chunks]
        o_h = jnp.concatenate(o_parts, axis=0) if len(o_parts) > 1 else o_parts[0]
        o_heads.append(_rms(o_h, on_ref[...]) * st["gact"][tr, hh * HEAD_DIM:(hh + 1) * HEAD_DIM])

    mixed = jnp.concatenate(o_heads + [st["z"][tr, :]], axis=1).astype(BF16)
    x = x_ref[:, r0:r0 + hl, :].reshape(tile, x_ref.shape[-1])
    y = x + jnp.dot(mixed, wout_ref[...], preferred_element_type=F32)
    y_ref[:, r0:r0 + hl, :] = y.reshape(nseq, hl, y.shape[-1])


def _stage_buffers(tile):
    return {
        "qb": pltpu.VMEM((tile, DN_WIDTH), BF16), "kb": pltpu.VMEM((tile, DN_WIDTH), BF16),
        "vb": pltpu.VMEM((tile, DN_WIDTH), BF16), "kdt": pltpu.VMEM((DN_HEADS, HEAD_DIM, tile), BF16),
        "gact": pltpu.VMEM((tile, DN_WIDTH), F32), "z": pltpu.VMEM((tile, POOL_WIDTH), F32),
        "colf": pltpu.VMEM((tile, LANES), F32), "rowf": pltpu.VMEM((N_ROW_FACTORS * SUBLANES, tile), F32),
    }


def _mixer_kernel(x_ref, nw_ref, win_ref, wab_ref, sel_ref, cum_ref, cw_ref, alog_ref, dtb_ref, on_ref,
                  wp_ref, ps_ref, wout_ref, s0_ref, c0_ref, p0_ref, sprev_ref, cprev_ref, pprev_ref,
                  y_ref, sn_ref, cn_ref, pn_ref,
                  s_scr, xpad, upad, qkv_s, d_s, wu_s, *stage_refs, nseq, ls, sub, chunk, pos0):
    del sprev_ref, cprev_ref, pprev_ref
    step = pl.program_id(1)
    st = dict(zip(_stage_buffers(nseq * ls), stage_refs))

    @pl.when(step == 0)
    def _():
        s_scr[...] = s0_ref[...]
        for sl in range(CONV_CH // LANES):
            xpad[:, sl, CONV_PAD - (CONV_WIDTH - 1):CONV_PAD, :] = c0_ref[:, :, sl * LANES:(sl + 1) * LANES]
        for gi in range(len(POOL_WINDOWS)):
            upad[:, gi, POOL_PAD - POOL_HIST:POOL_PAD, :] = p0_ref[:, :, gi * POOL_GROUP_DIM:(gi + 1) * POOL_GROUP_DIM]
        wu_s[...] = win_ref[:, COL_POOL_RAW:COL_POOL_RAW + POOL_WIDTH]

    stage1 = lambda r0: _mixer_stage1(step, r0, sub, x_ref, nw_ref, win_ref, wu_s, wab_ref, sel_ref, cum_ref, cw_ref,
                                      alog_ref, dtb_ref, wp_ref, ps_ref, xpad, upad, qkv_s, d_s, st,
                                      nseq=nseq, ls=ls, pos0=pos0)
    stage2 = lambda r0: _mixer_stage2(r0, sub, states, x_ref, on_ref, wout_ref, y_ref, qkv_s, st,
                                      nseq=nseq, ls=ls, chunk=chunk)
    states = {(hh, s): s_scr[s, hh] for hh in range(DN_HEADS) for s in range(nseq)}
    for r0 in range(0, ls, sub):
        stage1(r0)
        stage2(r0)
    for (hh, s), state in states.items():
        s_scr[s, hh] = state
    sn_ref[...] = s_scr[...]

    for sl in range(CONV_CH // LANES):
        tail = xpad[:, sl, CONV_PAD + ls - (CONV_WIDTH - 1):CONV_PAD + ls, :]
        xpad[:, sl, CONV_PAD - (CONV_WIDTH - 1):CONV_PAD, :] = tail
        cn_ref[:, :, sl * LANES:(sl + 1) * LANES] = tail
    for gi in range(len(POOL_WINDOWS)):
        tail = upad[:, gi, POOL_PAD + ls - POOL_HIST:POOL_PAD + ls, :]
        upad[:, gi, POOL_PAD - POOL_HIST:POOL_PAD, :] = tail
        pn_ref[:, :, gi * POOL_GROUP_DIM:(gi + 1) * POOL_GROUP_DIM] = tail


def _chunk_sum_matrix(tile, chunk):
    j = np.arange(tile)[:, None]
    i = np.arange(tile)[None, :]
    same = (j // chunk) == (i // chunk)
    return jnp.asarray(np.concatenate([same & (j <= i), same], axis=1), dtype=BF16)


def _head_select_matrix():
    sel = np.zeros((2 * SUBLANES, 2 * DN_WIDTH), np.float32)
    for hh in range(DN_HEADS):
        sel[hh, hh * HEAD_DIM:(hh + 1) * HEAD_DIM] = 1.0
        sel[SUBLANES + hh, DN_WIDTH + hh * HEAD_DIM:DN_WIDTH + (hh + 1) * HEAD_DIM] = 1.0
    return jnp.asarray(sel, dtype=BF16)


def _mixer(x, params, states, new_states, *, layer, state_layer, nseq, ls, sub, chunk, pos0):
    nw, w_in, w_ab_t, conv_w, alog, dtb, o_norm, w_pool, pool_scale, w_out = params
    b, l, d = x.shape
    assert sub & (sub - 1) == 0 and chunk & (chunk - 1) == 0 and ls % sub == 0 and sub % chunk == 0
    assert b % nseq == 0 and l % ls == 0 and (nseq == 1 or ls == sub)
    tile = nseq * ls
    sel = _head_select_matrix()
    cum = _chunk_sum_matrix(nseq * sub, chunk)
    stacked = lambda a: _resident(a.shape[1:], layer)
    seq_block = lambda a, which: pl.BlockSpec((None, nseq) + a.shape[2:],
                                              lambda i, j: (which, i) + (0,) * (a.ndim - 2))
    x_spec = pl.BlockSpec((nseq, ls, d), lambda i, j: (i, j, 0))
    untouched = pl.BlockSpec(memory_space=pl.ANY)
    n_in = 16
    return pl.pallas_call(
        functools.partial(_mixer_kernel, nseq=nseq, ls=ls, sub=sub, chunk=chunk, pos0=pos0),
        grid=(b // nseq, l // ls),
        in_specs=[x_spec, stacked(nw), stacked(w_in), stacked(w_ab_t), _resident(sel.shape), _resident(cum.shape),
                  stacked(conv_w), stacked(alog), stacked(dtb), stacked(o_norm), stacked(w_pool),
                  stacked(pool_scale), stacked(w_out),
                  *[seq_block(a, state_layer) for a in states], untouched, untouched, untouched],
        out_specs=(x_spec, *[seq_block(a, layer) for a in new_states]),
        out_shape=(jax.ShapeDtypeStruct(x.shape, x.dtype),
                   *[jax.ShapeDtypeStruct(a.shape, a.dtype) for a in new_states]),
        input_output_aliases={n_in + k: 1 + k for k in range(len(new_states))},
        scratch_shapes=[pltpu.VMEM((nseq, DN_HEADS, HEAD_DIM, HEAD_DIM), F32),
                        pltpu.VMEM((nseq, CONV_CH // LANES, CONV_PAD + ls, LANES), F32),
                        pltpu.VMEM((nseq, len(POOL_WINDOWS), POOL_PAD + ls, POOL_GROUP_DIM), F32),
                        pltpu.VMEM((CONV_CH // LANES, tile, LANES), F32),
                        pltpu.VMEM((len(POOL_WINDOWS), tile, POOL_GROUP_DIM), F32),
                        pltpu.VMEM((d, POOL_WIDTH), BF16)] + list(_stage_buffers(tile).values()),
        compiler_params=pltpu.CompilerParams(dimension_semantics=("arbitrary", "arbitrary"),
                                             vmem_limit_bytes=VMEM_LIMIT_BYTES),
        name="mixer",
    )(x, nw, w_in, w_ab_t, sel, cum, conv_w, alog, dtb, o_norm, w_pool, pool_scale, w_out, *states, *new_states)


def _head_col(v):
    return jnp.pad(v.astype(F32), ((0, 0), (0, SUBLANES - DN_HEADS)))[:, :, None]


def kernel(x_prompt, x_sample, state_delta, state_conv, state_pool, norm_ffn1, w_ffn1_gate, w_ffn1_up, w_ffn1_down, norm_mix, w_in, conv_w, a_log, dt_bias, o_norm, w_pool, pool_scale, w_out, norm_ffn2, w_ffn2_gate, w_ffn2_up, w_ffn2_down, norm_final):
    depth = w_in.shape[0]
    batch, seq, d_model = x_prompt.shape
    dec_batch, dec_seq, _ = x_sample.shape

    w_in_b = w_in.astype(BF16)
    w_a_t = jnp.swapaxes(w_in[..., COL_POOL:COL_POOL + DN_HEADS], 1, 2)
    w_b_t = jnp.swapaxes(w_in[..., COL_POOL + DN_HEADS:COL_POOL_RAW], 1, 2)
    head_pad = jnp.zeros((depth, SUBLANES - DN_HEADS, d_model), w_in.dtype)
    w_ab_t = jnp.concatenate([w_a_t, head_pad, w_b_t, head_pad], axis=1).astype(BF16)
    row = lambda v: v[:, None, :]
    ffn1 = (row(norm_ffn1), w_ffn1_gate.astype(BF16), w_ffn1_up.astype(BF16), w_ffn1_down.astype(BF16))
    ffn2 = (row(norm_ffn2), w_ffn2_gate.astype(BF16), w_ffn2_up.astype(BF16), w_ffn2_down.astype(BF16))
    mix = (row(norm_mix), w_in_b, w_ab_t, conv_w, _head_col(a_log), _head_col(dt_bias), row(o_norm),
           w_pool.astype(BF16), row(pool_scale), w_out.astype(BF16))
    nf = norm_final[None, :]

    state_in = (state_delta, state_conv, state_pool)
    prompt = dict(nseq=batch, ls=PROMPT_TILE, sub=PROMPT_TILE, chunk=PROMPT_CHUNK, pos0=0, state_layer=lambda layer: 0,
                  states=tuple(jnp.zeros((1, batch) + s.shape[2:], s.dtype) for s in state_in))
    sample = dict(nseq=dec_batch, ls=dec_seq, sub=dec_seq, chunk=dec_seq, pos0=PAST_LEN, state_layer=lambda layer: layer,
                  states=state_in)
    xp, xs = x_prompt.reshape(batch * seq, d_model), x_sample.reshape(dec_batch * dec_seq, d_model)
    new_p = tuple(jnp.zeros((depth, batch) + s.shape[2:], s.dtype) for s in state_in)
    new_s = tuple(jnp.zeros_like(s) for s in state_in)
    for layer in range(depth):
        xp, xs = _ffn(xp, xs, *ffn1, layer=layer, tm=FFN_TILE)
        outs = []
        for grp, x, new in ((prompt, xp.reshape(x_prompt.shape), new_p), (sample, xs.reshape(x_sample.shape), new_s)):
            outs.append(_mixer(x, mix, grp["states"], new, layer=layer, state_layer=grp["state_layer"](layer),
                               nseq=grp["nseq"], ls=grp["ls"], sub=grp["sub"], chunk=grp["chunk"], pos0=grp["pos0"]))
        (xp, *new_p), (xs, *new_s) = outs
        fw = nf if layer == depth - 1 else None
        xp, xs = _ffn(xp.reshape(batch * seq, d_model), xs.reshape(dec_batch * dec_seq, d_model), *ffn2, fw,
                      layer=layer, tm=FFN_TILE)
    return (xp.reshape(x_prompt.shape), xs.reshape(x_sample.shape), *new_p, *new_s)
```

```python
import functools

import numpy as np
import jax
import jax.numpy as jnp
from jax import lax
from jax.experimental import pallas as pl
from jax.experimental.pallas import tpu as pltpu

EPS = 1e-6
PAST_LEN = 4096
DN_HEADS = 4
HEAD_DIM = 128
DN_WIDTH = DN_HEADS * HEAD_DIM
CONV_WIDTH = 4
CONV_CH = 3 * DN_WIDTH
POOL_WINDOWS = (2, 4, 8, 16)
POOL_GROUP_DIM = 128
POOL_WIDTH = len(POOL_WINDOWS) * POOL_GROUP_DIM
POOL_HIST = max(POOL_WINDOWS) - 1
PROMPT_CHUNK = 128
LANES = 128
SUBLANES = 8
INV_BASE = 16
ROW_STRIDE = 4
MIN_STRIDED_ROWS = ROW_STRIDE * SUBLANES
COL_GATE = CONV_CH
COL_POOL = COL_GATE + DN_WIDTH
COL_POOL_RAW = COL_POOL + 2 * DN_HEADS
CONV_PAD = SUBLANES
POOL_PAD = 2 * SUBLANES
ROW_N, ROW_ETOT, ROW_BETA, ROW_W = (SUBLANES * i for i in range(4))
N_ROW_FACTORS = 4
COL_N, COL_Q, COL_QDEC = (SUBLANES * i for i in range(3))
VMEM_LIMIT_BYTES = 56 * 1024 * 1024
FFN_TILE = 1024
PROMPT_TILE = 256

F32 = jnp.float32
BF16 = jnp.bfloat16
NT_DIMS = (((1,), (1,)), ((), ()))


def _rms(x, w):
    return x * lax.rsqrt(jnp.mean(x * x, axis=-1, keepdims=True) + EPS) * w


def _mm(a, b):
    return jnp.dot(a.astype(BF16), b.astype(BF16), preferred_element_type=F32)


def _resident(shape, layer=None):
    zeros = (0,) * len(shape)
    if layer is None:
        return pl.BlockSpec(shape, lambda *_: zeros, pipeline_mode=pl.Buffered(1))
    return pl.BlockSpec((None,) + shape, lambda *_: (layer,) + zeros, pipeline_mode=pl.Buffered(1))


def _ffn_kernel(*refs, f_chunk, n_main, final_norm):
    if final_norm:
        x_ref, xe_ref, wg_ref, wu_ref, wd_ref, fw_ref, o_ref, oe_ref, h_ref = refs
    else:
        x_ref, xe_ref, wg_ref, wu_ref, wd_ref, o_ref, oe_ref, h_ref = refs
    d_ff = wg_ref.shape[1]

    def tile(src_ref, dst_ref):
        rows = src_ref.shape[0]
        x = src_ref[...]
        xn = (x * lax.rsqrt(jnp.mean(x * x, axis=-1, keepdims=True) + EPS)).astype(BF16)
        for c in range(d_ff // f_chunk):
            sl = slice(c * f_chunk, (c + 1) * f_chunk)
            g = jnp.dot(xn, wg_ref[:, sl], preferred_element_type=F32)
            u = jnp.dot(xn, wu_ref[:, sl], preferred_element_type=F32)
            h_ref[:rows, sl] = (g * jax.nn.sigmoid(g) * u).astype(BF16)
        y = x + 0.5 * jnp.dot(h_ref[:rows, :], wd_ref[...], preferred_element_type=F32)
        if final_norm:
            y = _rms(y, fw_ref[...])
        dst_ref[...] = y

    step = pl.program_id(0)
    pl.when(step < n_main)(lambda: tile(x_ref, o_ref))
    pl.when(step == n_main)(lambda: tile(xe_ref, oe_ref))


def _ffn(x, x_extra, wg, wu, wd, fw=None, *, layer, tm):
    m, d = x.shape
    me = x_extra.shape[0]
    d_ff = wg.shape[-1]
    n_main = m // tm
    assert m % tm == 0 and me <= tm
    final_norm = fw is not None
    args = [x, x_extra, wg, wu, wd] + ([fw] if final_norm else [])
    main_tile = lambda i: (jnp.minimum(i, n_main - 1), 0)
    in_specs = [pl.BlockSpec((tm, d), main_tile), pl.BlockSpec((me, d), lambda i: (0, 0)),
                _resident((d, d_ff), layer), _resident((d, d_ff), layer), _resident((d_ff, d), layer)]
    in_specs += [_resident((1, d))] if final_norm else []
    return pl.pallas_call(
        functools.partial(_ffn_kernel, f_chunk=256, n_main=n_main, final_norm=final_norm),
        grid=(n_main + 1,),
        in_specs=in_specs,
        out_specs=(pl.BlockSpec((tm, d), main_tile), pl.BlockSpec((me, d), lambda i: (0, 0))),
        out_shape=(jax.ShapeDtypeStruct((m, d), x.dtype), jax.ShapeDtypeStruct((me, d), x.dtype)),
        scratch_shapes=[pltpu.VMEM((tm, d_ff), BF16)],
        compiler_params=pltpu.CompilerParams(dimension_semantics=("arbitrary",),
                                             vmem_limit_bytes=VMEM_LIMIT_BYTES),
        name="ffn",
    )(*args)


def _unit_lower_inverses(ns, chunk):
    r = lax.broadcasted_iota(jnp.int32, (chunk, chunk), 0)
    c = lax.broadcasted_iota(jnp.int32, (chunk, chunk), 1)
    base = min(chunk, INV_BASE)
    shift = base.bit_length() - 1
    if chunk > base:
        diag = (r >> shift) == (c >> shift)
        nds = [jnp.where(diag, n, 0.0) for n in ns]
    else:
        nds = ns
    eye = jnp.where(r == c, 1.0, 0.0)
    invs = [eye - nd for nd in nds]
    pws = [nd.astype(BF16) for nd in nds]
    pws = [jnp.dot(pw, pw, preferred_element_type=F32).astype(BF16) for pw in pws]
    order = 2
    while 2 * order < base:
        both = [jnp.dot(jnp.concatenate([inv.astype(BF16), pw], axis=0), pw, preferred_element_type=F32)
                for inv, pw in zip(invs, pws)]
        invs = [inv + bo[:chunk] for inv, bo in zip(invs, both)]
        pws = [bo[chunk:].astype(BF16) for bo in both]
        order *= 2
    invs = [inv + _mm(inv, pw) for inv, pw in zip(invs, pws)]
    blk = base
    while blk < chunk:
        s = blk.bit_length() - 1
        lower = ((r >> (s + 1)) == (c >> (s + 1))) & ((r >> s) > (c >> s))
        tmp = [_mm(inv, jnp.where(lower, n, 0.0)) for inv, n in zip(invs, ns)]
        invs = [inv - _mm(t, inv) for inv, t in zip(invs, tmp)]
        blk *= 2
    return invs


def _mixer_stage1(step, r0, hl, x_ref, nw_ref, win_ref, wu_s, wab_ref, sel_ref, cum_ref, cw_ref, alog_ref, dtb_ref,
                  wp_ref, ps_ref, xpad, upad, qkv_s, d_s, st, *, nseq, ls, pos0):
    tile = nseq * hl
    tr = slice(nseq * r0, nseq * r0 + tile)
    x = x_ref[:, r0:r0 + hl, :].reshape(tile, x_ref.shape[-1])
    h = _rms(x, nw_ref[...]).astype(BF16)
    project = lambda c0, c1: jnp.dot(h, win_ref[:, c0:c1], preferred_element_type=F32)

    def conv_group(part, name):
        pre = project(part * DN_WIDTH, (part + 1) * DN_WIDTH)
        for hh in range(DN_HEADS):
            sl = part * DN_HEADS + hh
            xpad[:, sl, CONV_PAD + r0:CONV_PAD + r0 + hl, :] = (
                pre[:, hh * LANES:(hh + 1) * LANES].reshape(nseq, hl, LANES))
            cw = cw_ref[:, sl * LANES:(sl + 1) * LANES]
            if hl >= MIN_STRIDED_ROWS:
                for s in range(nseq):
                    for r in range(ROW_STRIDE):
                        conv = None
                        for j in range(CONV_WIDTH):
                            rows_j = pl.ds(CONV_PAD + r0 - (CONV_WIDTH - 1) + j + r, hl // ROW_STRIDE,
                                           stride=ROW_STRIDE)
                            term = xpad[s, sl, rows_j, :] * cw[j:j + 1]
                            conv = term if conv is None else conv + term
                        rows_out = pl.ds(nseq * r0 + s * hl + r, hl // ROW_STRIDE, stride=ROW_STRIDE)
                        qkv_s[sl, rows_out, :] = conv * jax.nn.sigmoid(conv)
            else:
                conv = None
                for j in range(CONV_WIDTH):
                    start = CONV_PAD + r0 - (CONV_WIDTH - 1) + j
                    term = xpad[:, sl, start:start + hl, :].reshape(tile, LANES) * cw[j:j + 1]
                    conv = term if conv is None else conv + term
                qkv_s[sl, tr, :] = conv * jax.nn.sigmoid(conv)
            st[name][tr, hh * HEAD_DIM:(hh + 1) * HEAD_DIM] = qkv_s[sl, tr, :].astype(BF16)

    conv_group(0, "qb")
    conv_group(1, "kb")
    conv_group(2, "vb")
    gate = project(COL_GATE, COL_POOL)
    st["gact"][tr, :] = gate * jax.nn.sigmoid(gate)
    u = jnp.dot(h, wu_s[...], preferred_element_type=F32)
    for gi in range(len(POOL_WINDOWS)):
        upad[:, gi, POOL_PAD + r0:POOL_PAD + r0 + hl, :] = (
            u[:, gi * POOL_GROUP_DIM:(gi + 1) * POOL_GROUP_DIM].reshape(nseq, hl, POOL_GROUP_DIM))

    ab_t = lax.dot_general(wab_ref[...], h, NT_DIMS, preferred_element_type=F32)
    qk_part = jnp.concatenate([qkv_s[sl, tr, :] for sl in range(2 * DN_HEADS)], axis=1)
    ss_t = lax.dot_general(sel_ref[...], (qk_part * qk_part).astype(BF16), NT_DIMS,
                           preferred_element_type=F32)
    a_pre = ab_t[:SUBLANES] + dtb_ref[...]
    softplus = jnp.maximum(a_pre, 0.0) + jnp.log1p(jnp.exp(-jnp.abs(a_pre)))
    g = -jnp.exp(alog_ref[...]) * softplus
    beta = jax.nn.sigmoid(ab_t[SUBLANES:])
    r_q = lax.rsqrt(ss_t[:SUBLANES] + EPS) * (HEAD_DIM ** -0.5)
    r_k = lax.rsqrt(ss_t[SUBLANES:] + EPS)
    g_hi = g.astype(BF16).astype(F32)
    g_mid = (g - g_hi).astype(BF16).astype(F32)
    g_lo = g - g_hi - g_mid
    pieces = jnp.concatenate([g_hi, g_mid, g_lo, jnp.zeros_like(g)], axis=0).astype(BF16)
    sums = jnp.dot(pieces, cum_ref[...], preferred_element_type=F32)
    part = lambda k, half: sums[k * SUBLANES:(k + 1) * SUBLANES, half * tile:(half + 1) * tile]
    gc = part(0, 0) + part(1, 0) + part(2, 0)
    gs = part(0, 1) + part(1, 1) + part(2, 1)
    e_gc = jnp.exp(gc)
    row_n = gc - jnp.log(r_k)
    st["rowf"][:, tr] = jnp.concatenate([row_n, jnp.exp(gs), beta, r_k * (beta * e_gc)], axis=0)
    cols = [gc + jnp.log(beta * r_k), gc + jnp.log(r_q), r_q * e_gc]
    st["colf"][tr, :] = jnp.concatenate(cols + [jnp.zeros((LANES - SUBLANES * len(cols), tile), F32)], axis=0).T
    k_dec_row = r_k * jnp.exp(gs - gc)
    for hh in range(DN_HEADS):
        st["kdt"][hh, :, tr] = (qkv_s[DN_HEADS + hh, tr, :].T * k_dec_row[hh:hh + 1, :]).astype(BF16)

    first_pos = pos0 + step * ls + r0
    for gi, win in enumerate(POOL_WINDOWS):
        lanes = slice(gi * POOL_GROUP_DIM, (gi + 1) * POOL_GROUP_DIM)
        if hl >= MIN_STRIDED_ROWS:
            n_rows = hl // ROW_STRIDE
            for s in range(nseq):
                for r in range(ROW_STRIDE):
                    cur = upad[s, gi, pl.ds(POOL_PAD + r0 + r, n_rows, stride=ROW_STRIDE), :]
                    acc = cur
                    for back in range(1, win):
                        acc = acc + upad[s, gi, pl.ds(POOL_PAD + r0 + r - back, n_rows, stride=ROW_STRIDE), :]
                    pos = first_pos + r + ROW_STRIDE * lax.broadcasted_iota(jnp.int32, (n_rows, 1), 0)
                    cnt = jnp.minimum(win, pos + 1).astype(F32)
                    d_s[gi, pl.ds(nseq * r0 + s * hl + r, n_rows, stride=ROW_STRIDE), :] = acc / cnt - cur
            diff = d_s[gi, tr, :]
        else:
            acc = u[:, lanes]
            for back in range(1, win):
                start = POOL_PAD + r0 - back
                acc = acc + upad[:, gi, start:start + hl, :].reshape(tile, POOL_GROUP_DIM)
            pos = first_pos + (lax.broadcasted_iota(jnp.int32, (tile, 1), 0) & (hl - 1))
            cnt = jnp.minimum(win, pos + 1).astype(F32)
            diff = acc / cnt - u[:, lanes]
        st["z"][tr, lanes] = _mm(diff, wp_ref[gi]) * ps_ref[:, lanes]


def _mixer_stage2(r0, hl, states, x_ref, on_ref, wout_ref, y_ref, qkv_s, st, *, nseq, ls, chunk):
    tile = nseq * hl
    tr = slice(nseq * r0, nseq * r0 + tile)
    chunks = range(r0 // chunk, (r0 + hl) // chunk)
    cr = lax.broadcasted_iota(jnp.int32, (chunk, chunk), 0)
    cc = lax.broadcasted_iota(jnp.int32, (chunk, chunk), 1)
    incl = cr >= cc
    strict = cr > cc

    probs = [(hh, s, ci) for hh in range(DN_HEADS) for s in range(nseq) for ci in chunks]
    rows = lambda pr: slice(pr[1] * ls + pr[2] * chunk, pr[1] * ls + (pr[2] + 1) * chunk)
    head = lambda pr: slice(pr[0] * HEAD_DIM, (pr[0] + 1) * HEAD_DIM)
    col = lambda base, pr: st["colf"][rows(pr), base + pr[0]:base + pr[0] + 1]
    row = lambda base, pr: st["rowf"][base + pr[0]:base + pr[0] + 1, rows(pr)]
    kq = [lax.dot_general(jnp.concatenate([st["kb"][rows(pr), head(pr)], st["qb"][rows(pr), head(pr)]], axis=0),
                          st["kb"][rows(pr), head(pr)], NT_DIMS, preferred_element_type=F32) for pr in probs]
    ns = [jnp.where(strict, kq_p[:chunk] * jnp.exp(jnp.where(strict, col(COL_N, pr) - row(ROW_N, pr), -1e30)), 0.0)
          for pr, kq_p in zip(probs, kq)]
    qkd = [kq_p[chunk:] * jnp.exp(jnp.where(incl, col(COL_Q, pr) - row(ROW_N, pr), -1e30))
           for pr, kq_p in zip(probs, kq)]
    a_inv = _unit_lower_inverses(ns, chunk)
    sol = [jnp.concatenate([_mm(inv_p * row(ROW_BETA, pr), st["vb"][rows(pr), head(pr)]),
                            _mm(inv_p * row(ROW_W, pr), st["kb"][rows(pr), head(pr)])], axis=1).astype(BF16)
           for pr, inv_p in zip(probs, a_inv)]
    qw = [_mm(qkd_p, sol_p) for qkd_p, sol_p in zip(qkd, sol)]
    kw = [jnp.dot(st["kdt"][pr[0], :, rows(pr)], sol_p, preferred_element_type=F32) for pr, sol_p in zip(probs, sol)]

    lhs = {pr: jnp.concatenate([qkv_s[pr[0], rows(pr), :] * col(COL_QDEC, pr) - qw_p[:, HEAD_DIM:],
                                kw_p[:, HEAD_DIM:]], axis=0).astype(BF16)
           for pr, qw_p, kw_p in zip(probs, qw, kw)}
    o_add = {pr: qw_p[:, :HEAD_DIM] for pr, qw_p in zip(probs, qw)}
    s_add = {pr: kw_p[:, :HEAD_DIM] for pr, kw_p in zip(probs, kw)}
    o_chunk = {}
    for ci in chunks:
        for hh in range(DN_HEADS):
            for s in range(nseq):
                pr = (hh, s, ci)
                both = jnp.dot(lhs[pr], states[hh, s].astype(BF16), preferred_element_type=F32)
                o_chunk[pr] = o_add[pr] + both[:chunk]
                first = s * ls + ci * chunk
                e_chunk = st["rowf"][ROW_ETOT + hh:ROW_ETOT + hh + 1, first:first + 1]
                states[hh, s] = states[hh, s] * e_chunk - both[chunk:] + s_add[pr]
    o_heads = []
    for hh in range(DN_HEADS):
        o_parts = [o_chunk[hh, s, ci] for s in range(nseq) for ci in chunks]
        o_h = jnp.concatenate(o_parts, axis=0) if len(o_parts) > 1 else o_parts[0]
        o_heads.append(_rms(o_h, on_ref[...]) * st["gact"][tr, hh * HEAD_DIM:(hh + 1) * HEAD_DIM])

    mixed = jnp.concatenate(o_heads + [st["z"][tr, :]], axis=1).astype(BF16)
    x = x_ref[:, r0:r0 + hl, :].reshape(tile, x_ref.shape[-1])
    y = x + jnp.dot(mixed, wout_ref[...], preferred_element_type=F32)
    y_ref[:, r0:r0 + hl, :] = y.reshape(nseq, hl, y.shape[-1])


def _stage_buffers(tile):
    return {
        "qb": pltpu.VMEM((tile, DN_WIDTH), BF16), "kb": pltpu.VMEM((tile, DN_WIDTH), BF16),
        "vb": pltpu.VMEM((tile, DN_WIDTH), BF16), "kdt": pltpu.VMEM((DN_HEADS, HEAD_DIM, tile), BF16),
        "gact": pltpu.VMEM((tile, DN_WIDTH), F32), "z": pltpu.VMEM((tile, POOL_WIDTH), F32),
        "colf": pltpu.VMEM((tile, LANES), F32), "rowf": pltpu.VMEM((N_ROW_FACTORS * SUBLANES, tile), F32),
    }


def _mixer_kernel(x_ref, nw_ref, win_ref, wab_ref, sel_ref, cum_ref, cw_ref, alog_ref, dtb_ref, on_ref,
                  wp_ref, ps_ref, wout_ref, s0_ref, c0_ref, p0_ref, sprev_ref, cprev_ref, pprev_ref,
                  y_ref, sn_ref, cn_ref, pn_ref,
                  s_scr, xpad, upad, qkv_s, d_s, wu_s, *stage_refs, nseq, ls, sub, chunk, pos0):
    del sprev_ref, cprev_ref, pprev_ref
    step = pl.program_id(1)
    st = dict(zip(_stage_buffers(nseq * ls), stage_refs))

    @pl.when(step == 0)
    def _():
        s_scr[...] = s0_ref[...]
        for sl in range(CONV_CH // LANES):
            xpad[:, sl, CONV_PAD - (CONV_WIDTH - 1):CONV_PAD, :] = c0_ref[:, :, sl * LANES:(sl + 1) * LANES]
        for gi in range(len(POOL_WINDOWS)):
            upad[:, gi, POOL_PAD - POOL_HIST:POOL_PAD, :] = p0_ref[:, :, gi * POOL_GROUP_DIM:(gi + 1) * POOL_GROUP_DIM]
        wu_s[...] = win_ref[:, COL_POOL_RAW:COL_POOL_RAW + POOL_WIDTH]

    stage1 = lambda r0: _mixer_stage1(step, r0, sub, x_ref, nw_ref, win_ref, wu_s, wab_ref, sel_ref, cum_ref, cw_ref,
                                      alog_ref, dtb_ref, wp_ref, ps_ref, xpad, upad, qkv_s, d_s, st,
                                      nseq=nseq, ls=ls, pos0=pos0)
    stage2 = lambda r0: _mixer_stage2(r0, sub, states, x_ref, on_ref, wout_ref, y_ref, qkv_s, st,
                                      nseq=nseq, ls=ls, chunk=chunk)
    states = {(hh, s): s_scr[s, hh] for hh in range(DN_HEADS) for s in range(nseq)}
    for r0 in range(0, ls, sub):
        stage1(r0)
        stage2(r0)
    for (hh, s), state in states.items():
        s_scr[s, hh] = state
    sn_ref[...] = s_scr[...]

    for sl in range(CONV_CH // LANES):
        tail = xpad[:, sl, CONV_PAD + ls - (CONV_WIDTH - 1):CONV_PAD + ls, :]
        xpad[:, sl, CONV_PAD - (CONV_WIDTH - 1):CONV_PAD, :] = tail
        cn_ref[:, :, sl * LANES:(sl + 1) * LANES] = tail
    for gi in range(len(POOL_WINDOWS)):
        tail = upad[:, gi, POOL_PAD + ls - POOL_HIST:POOL_PAD + ls, :]
        upad[:, gi, POOL_PAD - POOL_HIST:POOL_PAD, :] = tail
        pn_ref[:, :, gi * POOL_GROUP_DIM:(gi + 1) * POOL_GROUP_DIM] = tail


def _chunk_sum_matrix(tile, chunk):
    j = np.arange(tile)[:, None]
    i = np.arange(tile)[None, :]
    same = (j // chunk) == (i // chunk)
    return jnp.asarray(np.concatenate([same & (j <= i), same], axis=1), dtype=BF16)


def _head_select_matrix():
    sel = np.zeros((2 * SUBLANES, 2 * DN_WIDTH), np.float32)
    for hh in range(DN_HEADS):
        sel[hh, hh * HEAD_DIM:(hh + 1) * HEAD_DIM] = 1.0
        sel[SUBLANES + hh, DN_WIDTH + hh * HEAD_DIM:DN_WIDTH + (hh + 1) * HEAD_DIM] = 1.0
    return jnp.asarray(sel, dtype=BF16)


def _mixer(x, params, states, new_states, *, layer, state_layer, nseq, ls, sub, chunk, pos0):
    nw, w_in, w_ab_t, conv_w, alog, dtb, o_norm, w_pool, pool_scale, w_out = params
    b, l, d = x.shape
    assert sub & (sub - 1) == 0 and chunk & (chunk - 1) == 0 and ls % sub == 0 and sub % chunk == 0
    assert b % nseq == 0 and l % ls == 0 and (nseq == 1 or ls == sub)
    tile = nseq * ls
    sel = _head_select_matrix()
    cum = _chunk_sum_matrix(nseq * sub, chunk)
    stacked = lambda a: _resident(a.shape[1:], layer)
    seq_block = lambda a, which: pl.BlockSpec((None, nseq) + a.shape[2:],
                                              lambda i, j: (which, i) + (0,) * (a.ndim - 2))
    x_spec = pl.BlockSpec((nseq, ls, d), lambda i, j: (i, j, 0))
    untouched = pl.BlockSpec(memory_space=pl.ANY)
    n_in = 16
    return pl.pallas_call(
        functools.partial(_mixer_kernel, nseq=nseq, ls=ls, sub=sub, chunk=chunk, pos0=pos0),
        grid=(b // nseq, l // ls),
        in_specs=[x_spec, stacked(nw), stacked(w_in), stacked(w_ab_t), _resident(sel.shape), _resident(cum.shape),
                  stacked(conv_w), stacked(alog), stacked(dtb), stacked(o_norm), stacked(w_pool),
                  stacked(pool_scale), stacked(w_out),
                  *[seq_block(a, state_layer) for a in states], untouched, untouched, untouched],
        out_specs=(x_spec, *[seq_block(a, layer) for a in new_states]),
        out_shape=(jax.ShapeDtypeStruct(x.shape, x.dtype),
                   *[jax.ShapeDtypeStruct(a.shape, a.dtype) for a in new_states]),
        input_output_aliases={n_in + k: 1 + k for k in range(len(new_states))},
        scratch_shapes=[pltpu.VMEM((nseq, DN_HEADS, HEAD_DIM, HEAD_DIM), F32),
                        pltpu.VMEM((nseq, CONV_CH // LANES, CONV_PAD + ls, LANES), F32),
                        pltpu.VMEM((nseq, len(POOL_WINDOWS), POOL_PAD + ls, POOL_GROUP_DIM), F32),
                        pltpu.VMEM((CONV_CH // LANES, tile, LANES), F32),
                        pltpu.VMEM((len(POOL_WINDOWS), tile, POOL_GROUP_DIM), F32),
                        pltpu.VMEM((d, POOL_WIDTH), BF16)] + list(_stage_buffers(tile).values()),
        compiler_params=pltpu.CompilerParams(dimension_semantics=("arbitrary", "arbitrary"),
                                             vmem_limit_bytes=VMEM_LIMIT_BYTES),
        name="mixer",
    )(x, nw, w_in, w_ab_t, sel, cum, conv_w, alog, dtb, o_norm, w_pool, pool_scale, w_out, *states, *new_states)


def _head_col(v):
    return jnp.pad(v.astype(F32), ((0, 0), (0, SUBLANES - DN_HEADS)))[:, :, None]


def kernel(x_prompt, x_sample, state_delta, state_conv, state_pool, norm_ffn1, w_ffn1_gate, w_ffn1_up, w_ffn1_down, norm_mix, w_in, conv_w, a_log, dt_bias, o_norm, w_pool, pool_scale, w_out, norm_ffn2, w_ffn2_gate, w_ffn2_up, w_ffn2_down, norm_final):
    depth = w_in.shape[0]
    batch, seq, d_model = x_prompt.shape
    dec_batch, dec_seq, _ = x_sample.shape

    w_in_b = w_in.astype(BF16)
    w_a_t = jnp.swapaxes(w_in[..., COL_POOL:COL_POOL + DN_HEADS], 1, 2)
    w_b_t = jnp.swapaxes(w_in[..., COL_POOL + DN_HEADS:COL_POOL_RAW], 1, 2)
    head_pad = jnp.zeros((depth, SUBLANES - DN_HEADS, d_model), w_in.dtype)
    w_ab_t = jnp.concatenate([w_a_t, head_pad, w_b_t, head_pad], axis=1).astype(BF16)
    row = lambda v: v[:, None, :]
    gain = lambda norm, w: (w * norm[:, :, None]).astype(BF16)
    ffn1 = (gain(norm_ffn1, w_ffn1_gate), gain(norm_ffn1, w_ffn1_up), w_ffn1_down.astype(BF16))
    ffn2 = (gain(norm_ffn2, w_ffn2_gate), gain(norm_ffn2, w_ffn2_up), w_ffn2_down.astype(BF16))
    mix = (row(norm_mix), w_in_b, w_ab_t, conv_w, _head_col(a_log), _head_col(dt_bias), row(o_norm),
           w_pool.astype(BF16), row(pool_scale), w_out.astype(BF16))
    nf = norm_final[None, :]

    state_in = (state_delta, state_conv, state_pool)
    prompt = dict(nseq=batch, ls=PROMPT_TILE, sub=PROMPT_TILE, chunk=PROMPT_CHUNK, pos0=0, state_layer=lambda layer: 0,
                  states=tuple(jnp.zeros((1, batch) + s.shape[2:], s.dtype) for s in state_in))
    sample = dict(nseq=dec_batch, ls=dec_seq, sub=dec_seq, chunk=dec_seq, pos0=PAST_LEN, state_layer=lambda layer: layer,
                  states=state_in)
    xp, xs = x_prompt.reshape(batch * seq, d_model), x_sample.reshape(dec_batch * dec_seq, d_model)
    new_p = tuple(jnp.zeros((depth, batch) + s.shape[2:], s.dtype) for s in state_in)
    new_s = tuple(jnp.zeros_like(s) for s in state_in)
    for layer in range(depth):
        xp, xs = _ffn(xp, xs, *ffn1, layer=layer, tm=FFN_TILE)
        outs = []
        for grp, x, new in ((prompt, xp.reshape(x_prompt.shape), new_p), (sample, xs.reshape(x_sample.shape), new_s)):
            outs.append(_mixer(x, mix, grp["states"], new, layer=layer, state_layer=grp["state_layer"](layer),
                               nseq=grp["nseq"], ls=grp["ls"], sub=grp["sub"], chunk=grp["chunk"], pos0=grp["pos0"]))
        (xp, *new_p), (xs, *new_s) = outs
        fw = nf if layer == depth - 1 else None
        xp, xs = _ffn(xp.reshape(batch * seq, d_model), xs.reshape(dec_batch * dec_seq, d_model), *ffn2, fw,
                      layer=layer, tm=FFN_TILE)
    return (xp.reshape(x_prompt.shape), xs.reshape(x_sample.shape), *new_p, *new_s)
```
